```python
import math
import jax, jax.numpy as jnp
from jax import lax
import numpy as np

D_MODEL = 1024
BATCH = 8
SEQ = 2048
DEPTH = 1

N_HEADS = 8
HEAD_DIM = 128
ATTN_WIDTH = N_HEADS * HEAD_DIM
CONV_WIDTH = D_MODEL
CONV_K = 3
MOBA_BLOCK = 256
MOBA_TOPK = 3
QUERY_CHUNK = 64
D_FF = 2816
LN_EPS = 1e-5
ALPHA = (2.0 * DEPTH) ** 0.25
BETA = (8.0 * DEPTH) ** -0.25
SPLITS = (ATTN_WIDTH, 2 * ATTN_WIDTH, 3 * ATTN_WIDTH,
          3 * ATTN_WIDTH + CONV_WIDTH, 3 * ATTN_WIDTH + 2 * CONV_WIDTH,
          3 * ATTN_WIDTH + 3 * CONV_WIDTH)
PROJ_COLS = 3 * ATTN_WIDTH + 3 * CONV_WIDTH + 2 * D_MODEL

kernel_name = "hybrid_moba_shortconv_macaron_deepnorm"


def layer_norm(x, g, b):
    xf = x.astype(jnp.float32)
    mu = jnp.mean(xf, axis=-1, keepdims=True)
    var = jnp.mean(jnp.square(xf - mu), axis=-1, keepdims=True)
    return ((xf - mu) * lax.rsqrt(var + LN_EPS) * g + b).astype(x.dtype)


def swiglu(x, w_up, w_down):
    gate, up = jnp.split(x @ w_up, 2, axis=-1)
    return (jax.nn.silu(gate) * up) @ w_down


def moba_attention(q, k, v):
    B, H, S, hd = q.shape
    nb = -(-S // MOBA_BLOCK)
    s_pad = nb * MOBA_BLOCK
    pad = [(0, 0), (0, 0), (0, s_pad - S), (0, 0)]
    q, k, v = (jnp.pad(t, pad) for t in (q, k, v))
    scale = hd ** -0.5
    kb = k.reshape(B, H, nb, MOBA_BLOCK, hd)
    vb = v.reshape(B, H, nb, MOBA_BLOCK, hd)
    k_mean = jnp.mean(kb.astype(jnp.float32), axis=3)
    gate = jnp.einsum('bhsd,bhnd->bhsn', q.astype(jnp.float32), k_mean)
    q_blk = jnp.arange(s_pad) // MOBA_BLOCK
    past = jnp.arange(nb)[None, :] < q_blk[:, None]
    gate = jnp.where(past, gate, -jnp.inf)
    n_slots = max(1, min(MOBA_TOPK, nb))
    _, sel = lax.top_k(gate, n_slots)
    sel = sel.astype(jnp.int32)

    n_chunks = s_pad // QUERY_CHUNK
    q_items = q.reshape(B, H, n_chunks, QUERY_CHUNK, hd).transpose(0, 2, 1, 3, 4)
    q_items = q_items.reshape(B * n_chunks, H, QUERY_CHUNK, hd)
    idx_items = sel.reshape(B, H, n_chunks, QUERY_CHUNK, n_slots).transpose(0, 2, 1, 3, 4)
    idx_items = idx_items.reshape(B * n_chunks, H, QUERY_CHUNK, n_slots)
    b_ids = jnp.repeat(jnp.arange(B, dtype=jnp.int32), n_chunks)
    c_ids = jnp.tile(jnp.arange(n_chunks, dtype=jnp.int32), B)

    def one_chunk(args):
        qc, idx, b, c = args
        k_b = lax.dynamic_index_in_dim(kb, b, 0, keepdims=False)
        v_b = lax.dynamic_index_in_dim(vb, b, 0, keepdims=False)
        q_start = c * QUERY_CHUNK
        blk = q_start // MOBA_BLOCK
        k_own = lax.dynamic_index_in_dim(k_b, blk, 1, keepdims=False)
        v_own = lax.dynamic_index_in_dim(v_b, blk, 1, keepdims=False)
        k_sel = jax.vmap(lambda kh, ih: kh[ih])(k_b, idx)
        v_sel = jax.vmap(lambda vh, ih: vh[ih])(v_b, idx)
        s_sel = jnp.einsum('hqd,hqnkd->hqnk', qc, k_sel).astype(jnp.float32) * scale
        slot_ok = jnp.arange(n_slots) < blk
        s_sel = jnp.where(slot_ok[None, None, :, None], s_sel, -jnp.inf)
        s_sel = s_sel.reshape(H, QUERY_CHUNK, n_slots * MOBA_BLOCK)
        s_own = jnp.einsum('hqd,hkd->hqk', qc, k_own).astype(jnp.float32) * scale
        q_pos = q_start + jnp.arange(QUERY_CHUNK)
        k_pos = blk * MOBA_BLOCK + jnp.arange(MOBA_BLOCK)
        own_ok = k_pos[None, :] <= q_pos[:, None]
        s_own = jnp.where(own_ok[None], s_own, -jnp.inf)
        p = jax.nn.softmax(jnp.concatenate([s_sel, s_own], axis=-1), axis=-1)
        p_sel = p[..., :n_slots * MOBA_BLOCK].reshape(H, QUERY_CHUNK, n_slots, MOBA_BLOCK)
        p_own = p[..., n_slots * MOBA_BLOCK:]
        out = (jnp.einsum('hqnk,hqnkd->hqd', p_sel.astype(v_sel.dtype), v_sel)
               + jnp.einsum('hqk,hkd->hqd', p_own.astype(v_own.dtype), v_own))
        return out.astype(qc.dtype)

    out = lax.map(one_chunk, (q_items, idx_items, b_ids, c_ids))
    out = out.reshape(B, n_chunks, H, QUERY_CHUNK, hd).transpose(0, 1, 3, 2, 4)
    return out.reshape(B, s_pad, H * hd)[:, :S]


def short_conv(u, w, bias):
    y = lax.conv_general_dilated(
        u, w[:, None, :], window_strides=(1,), padding=[(CONV_K - 1, 0)],
        dimension_numbers=('NWC', 'WIO', 'NWC'), feature_group_count=u.shape[-1])
    return y + bias


def hybrid_mixer(x, w_in, b_gate, conv_w, conv_b, w_proj_attn, w_proj_conv, w_out):
    B, S, _ = x.shape
    z = x @ w_in
    q, k, v, h, g_b, g_c, gates = jnp.split(z, SPLITS, axis=-1)

    def heads(t):
        return t.reshape(B, S, N_HEADS, HEAD_DIM).transpose(0, 2, 1, 3)

    y_attn = moba_attention(heads(q), heads(k), heads(v)) @ w_proj_attn
    y_conv = (g_b * short_conv(g_c * h, conv_w, conv_b)) @ w_proj_conv
    gate_attn, gate_conv = jnp.split(jax.nn.sigmoid(gates + b_gate), 2, axis=-1)
    return (gate_attn * y_attn + gate_conv * y_conv) @ w_out


def setup_inputs(seed: int = 0) -> dict:
    key = jax.random.key(seed)
    ks = jax.random.split(key, 20)
    nrm = jax.random.normal
    L = DEPTH
    x = nrm(ks[0], (BATCH, SEQ, D_MODEL), jnp.float32)
    ffn1_w_up = nrm(ks[1], (L, D_MODEL, 2 * D_FF), jnp.float32) * D_MODEL ** -0.5
    ffn1_w_down = nrm(ks[2], (L, D_FF, D_MODEL), jnp.float32) * (D_FF ** -0.5 * BETA)
    ln1_g = 1.0 + 0.02 * nrm(ks[3], (L, D_MODEL), jnp.float32)
    ln1_b = 0.02 * nrm(ks[4], (L, D_MODEL), jnp.float32)
    col_scale = jnp.ones((PROJ_COLS,), jnp.float32).at[2 * ATTN_WIDTH:3 * ATTN_WIDTH].set(BETA)
    w_in = nrm(ks[5], (L, D_MODEL, PROJ_COLS), jnp.float32) * D_MODEL ** -0.5 * col_scale
    b_gate = 0.1 * nrm(ks[6], (L, 2 * D_MODEL), jnp.float32)
    conv_w = nrm(ks[7], (L, CONV_K, CONV_WIDTH), jnp.float32) * CONV_K ** -0.5
    conv_b = 0.02 * nrm(ks[8], (L, CONV_WIDTH), jnp.float32)
    w_proj_attn = nrm(ks[9], (L, ATTN_WIDTH, D_MODEL), jnp.float32) * ATTN_WIDTH ** -0.5
    w_proj_conv = nrm(ks[10], (L, CONV_WIDTH, D_MODEL), jnp.float32) * CONV_WIDTH ** -0.5
    w_out = nrm(ks[11], (L, D_MODEL, D_MODEL), jnp.float32) * (D_MODEL ** -0.5 * BETA)
    ln2_g = 1.0 + 0.02 * nrm(ks[12], (L, D_MODEL), jnp.float32)
    ln2_b = 0.02 * nrm(ks[13], (L, D_MODEL), jnp.float32)
    ffn2_w_up = nrm(ks[14], (L, D_MODEL, 2 * D_FF), jnp.float32) * D_MODEL ** -0.5
    ffn2_w_down = nrm(ks[15], (L, D_FF, D_MODEL), jnp.float32) * (D_FF ** -0.5 * BETA)
    ln3_g = 1.0 + 0.02 * nrm(ks[16], (L, D_MODEL), jnp.float32)
    ln3_b = 0.02 * nrm(ks[17], (L, D_MODEL), jnp.float32)
    return {"x": x, "ffn1_w_up": ffn1_w_up, "ffn1_w_down": ffn1_w_down,
            "ln1_g": ln1_g, "ln1_b": ln1_b, "w_in": w_in, "b_gate": b_gate,
            "conv_w": conv_w, "conv_b": conv_b, "w_proj_attn": w_proj_attn,
            "w_proj_conv": w_proj_conv, "w_out": w_out, "ln2_g": ln2_g, "ln2_b": ln2_b,
            "ffn2_w_up": ffn2_w_up, "ffn2_w_down": ffn2_w_down,
            "ln3_g": ln3_g, "ln3_b": ln3_b}


def reference(x, ffn1_w_up, ffn1_w_down, ln1_g, ln1_b, w_in, b_gate, conv_w, conv_b,
              w_proj_attn, w_proj_conv, w_out, ln2_g, ln2_b, ffn2_w_up, ffn2_w_down,
              ln3_g, ln3_b):
    for l in range(DEPTH):
        x = layer_norm(ALPHA * x + 0.5 * swiglu(x, ffn1_w_up[l], ffn1_w_down[l]), ln1_g[l], ln1_b[l])
        mix = hybrid_mixer(x, w_in[l], b_gate[l], conv_w[l], conv_b[l],
                           w_proj_attn[l], w_proj_conv[l], w_out[l])
        x = layer_norm(ALPHA * x + mix, ln2_g[l], ln2_b[l])
        x = layer_norm(ALPHA * x + 0.5 * swiglu(x, ffn2_w_up[l], ffn2_w_down[l]), ln3_g[l], ln3_b[l])
    return x
```

```python
import functools
import math

import jax
import jax.numpy as jnp
from jax import lax
from jax.experimental import pallas as pl
from jax.experimental.pallas import tpu as pltpu

N_HEADS = 8
HEAD_DIM = 128
MOBA_BLOCK = 256
MOBA_TOPK = 3
CONV_K = 3
LN_EPS = 1e-5

F32 = jnp.float32
BF16 = jnp.bfloat16

TOKEN_TILE = 256
FF_CHUNK = 256
COL_CHUNK = 256
CONV_HALO = 8
VMEM_LIMIT = 48 * 1024 * 1024


def _resident(shape):
    nd = len(shape)
    return pl.BlockSpec(shape, lambda *_: (0,) * nd, pipeline_mode=pl.Buffered(1))


def _layer_norm(y, g, b):
    mu = jnp.mean(y, axis=-1, keepdims=True)
    d = y - mu
    var = jnp.mean(d * d, axis=-1, keepdims=True)
    return d * lax.rsqrt(var + LN_EPS) * g + b


def _dot(a, b):
    return jnp.dot(a, b, preferred_element_type=F32)


def _ffn_ln_kernel(x_ref, wup_ref, wdn_ref, g_ref, b_ref, o_ref, *, alpha, d_ff):
    x = x_ref[...]
    xb = x.astype(BF16)
    acc = jnp.zeros(x.shape, F32)
    for c0 in range(0, d_ff, FF_CHUNK):
        gate = _dot(xb, wup_ref[:, c0:c0 + FF_CHUNK])
        up = _dot(xb, wup_ref[:, d_ff + c0:d_ff + c0 + FF_CHUNK])
        h = (gate * jax.nn.sigmoid(gate) * up).astype(BF16)
        acc = acc + _dot(h, wdn_ref[c0:c0 + FF_CHUNK, :])
    o_ref[...] = _layer_norm(alpha * x + 0.5 * acc, g_ref[...], b_ref[...])


def _ffn_ln(x, w_up, w_down, g, b, alpha):
    t, d = x.shape
    d_ff = w_down.shape[0]
    assert t % TOKEN_TILE == 0 and d_ff % FF_CHUNK == 0
    return pl.pallas_call(
        functools.partial(_ffn_ln_kernel, alpha=alpha, d_ff=d_ff),
        grid=(t // TOKEN_TILE,),
        in_specs=[
            pl.BlockSpec((TOKEN_TILE, d), lambda i: (i, 0)),
            _resident((d, 2 * d_ff)),
            _resident((d_ff, d)),
            _resident((1, d)),
            _resident((1, d)),
        ],
        out_specs=pl.BlockSpec((TOKEN_TILE, d), lambda i: (i, 0)),
        out_shape=jax.ShapeDtypeStruct((t, d), F32),
        compiler_params=pltpu.CompilerParams(
            dimension_semantics=("parallel",), vmem_limit_bytes=VMEM_LIMIT),
        name="ffn_ln",
    )(x, w_up, w_down, g, b)


def _qkv_kernel(x_ref, wk_ref, wqvt_ref, k_ref, qvt_ref, kmean_ref):
    xb = x_ref[...].astype(BF16)
    k = _dot(xb, wk_ref[...])
    k_ref[...] = k.astype(BF16)
    for j in range(TOKEN_TILE // MOBA_BLOCK):
        blk = k[j * MOBA_BLOCK:(j + 1) * MOBA_BLOCK]
        kmean_ref[0, j:j + 1, :] = jnp.sum(blk, axis=0, keepdims=True) * (1.0 / MOBA_BLOCK)
    qvt = lax.dot_general(wqvt_ref[...], xb, (((1,), (1,)), ((), ())),
                          preferred_element_type=F32)
    qvt_ref[...] = qvt.astype(BF16)


def _qkv_proj(x, w_k, w_qvt):
    t, d = x.shape
    a = w_k.shape[1]
    assert t % TOKEN_TILE == 0 and TOKEN_TILE % MOBA_BLOCK == 0
    n_tiles, blocks_per_tile = t // TOKEN_TILE, TOKEN_TILE // MOBA_BLOCK
    return pl.pallas_call(
        _qkv_kernel,
        grid=(n_tiles,),
        in_specs=[
            pl.BlockSpec((TOKEN_TILE, d), lambda i: (i, 0)),
            _resident((d, a)),
            _resident((2 * a, d)),
        ],
        out_specs=[
            pl.BlockSpec((TOKEN_TILE, a), lambda i: (i, 0)),
            pl.BlockSpec((2 * a, TOKEN_TILE), lambda i: (0, i)),
            pl.BlockSpec((1, blocks_per_tile, a), lambda i: (i, 0, 0)),
        ],
        out_shape=[
            jax.ShapeDtypeStruct((t, a), BF16),
            jax.ShapeDtypeStruct((2 * a, t), BF16),
            jax.ShapeDtypeStruct((n_tiles, blocks_per_tile, a), F32),
        ],
        compiler_params=pltpu.CompilerParams(
            dimension_semantics=("parallel",), vmem_limit_bytes=VMEM_LIMIT),
        name="qkv_proj",
    )(x, w_k, w_qvt)


def _moba_kernel(qt_ref, k_ref, vt_ref, km_ref, o_ref, s_scr, p_scr, *, n_blocks, exp2_scale):
    blk = MOBA_BLOCK
    km = km_ref[...]
    km_hi = km.astype(BF16)
    km_lo = (km - km_hi.astype(F32)).astype(BF16)
    blk_id = lax.broadcasted_iota(jnp.int32, (n_blocks, blk), 0)
    key_pos = lax.broadcasted_iota(jnp.int32, (blk, blk), 0)
    qry_pos = lax.broadcasted_iota(jnp.int32, (blk, blk), 1)
    causal = key_pos <= qry_pos

    for qb in range(n_blocks):
        qt = qt_ref[:, qb * blk:(qb + 1) * blk]
        choose = qb > MOBA_TOPK
        if choose:
            gate = _dot(km_hi, qt) + _dot(km_lo, qt)
            past = blk_id < qb
            gate = jnp.where(past, gate, -jnp.inf)
            beaten = jnp.zeros(gate.shape, jnp.int32)
            for m in range(qb):
                gm = gate[m:m + 1, :]
                wins = (gm > gate) | ((gm == gate) & (blk_id > m))
                beaten = beaten + wins.astype(jnp.int32)
            bias = jnp.where(past & (beaten < MOBA_TOPK), 0.0, -jnp.inf)

        m_run = jnp.full((1, blk), -jnp.inf, F32)
        for n in range(qb + 1):
            s = _dot(k_ref[n * blk:(n + 1) * blk, :], qt)
            if n == qb:
                s = jnp.where(causal, s, -jnp.inf)
            elif choose:
                s = s + bias[n:n + 1, :]
            s_scr[n] = s
            m_run = jnp.maximum(m_run, jnp.max(s, axis=0, keepdims=True))

        l_sum = jnp.zeros((1, blk), F32)
        for n in range(qb + 1):
            p = jnp.exp2((s_scr[n] - m_run) * exp2_scale)
            l_sum = l_sum + jnp.sum(p, axis=0, keepdims=True)
            p_scr[n * blk:(n + 1) * blk, :] = p.astype(BF16)

        nk = (qb + 1) * blk
        acc_t = _dot(vt_ref[:, :nk], p_scr[:nk, :])
        out = (acc_t * (1.0 / l_sum)).T
        o_ref[qb * blk:(qb + 1) * blk, :] = out.astype(o_ref.dtype)


def _moba_attention(k, qvt, kmean, batch, seq):
    t, a = k.shape
    hd = HEAD_DIM
    n_heads = a // hd
    n_blocks = seq // MOBA_BLOCK
    assert seq % MOBA_BLOCK == 0 and kmean.shape == (batch, n_blocks, a)
    exp2_scale = (hd ** -0.5) * math.log2(math.e)
    return pl.pallas_call(
        functools.partial(_moba_kernel, n_blocks=n_blocks, exp2_scale=exp2_scale),
        grid=(batch, n_heads),
        in_specs=[
            pl.BlockSpec((hd, seq), lambda b, h: (h, b)),
            pl.BlockSpec((seq, hd), lambda b, h: (b, h)),
            pl.BlockSpec((hd, seq), lambda b, h: (n_heads + h, b)),
            pl.BlockSpec((None, n_blocks, hd), lambda b, h: (b, 0, h)),
        ],
        out_specs=pl.BlockSpec((seq, hd), lambda b, h: (b, h)),
        out_shape=jax.ShapeDtypeStruct((t, a), BF16),
        scratch_shapes=[
            pltpu.VMEM((n_blocks, MOBA_BLOCK, MOBA_BLOCK), F32),
            pltpu.VMEM((seq, MOBA_BLOCK), BF16),
        ],
        compiler_params=pltpu.CompilerParams(
            dimension_semantics=("parallel", "parallel"), vmem_limit_bytes=VMEM_LIMIT),
        name="moba_attn",
    )(qvt, k, qvt, kmean)


def _tail_kernel(x_ref, a_ref, wc_ref, wg_ref, bg_ref, cw_ref, cb_ref, wpa_ref, wpc_ref,
                 wo_ref, g_ref, b_ref, o_ref, u_scr, c_scr, mix_scr, *, alpha, tiles_per_seq):
    tm, d = x_ref.shape
    i = pl.program_id(0)
    x = x_ref[...]
    xb = x.astype(BF16)

    @pl.when(i % tiles_per_seq == 0)
    def _():
        u_scr[0:CONV_HALO, :] = jnp.zeros((CONV_HALO, d), F32)

    @pl.when(i % tiles_per_seq != 0)
    def _():
        u_scr[0:CONV_HALO, :] = u_scr[tm:tm + CONV_HALO, :]

    for c0 in range(0, d, COL_CHUNK):
        cs = slice(c0, c0 + COL_CHUNK)
        h = _dot(xb, wc_ref[:, c0:c0 + COL_CHUNK])
        g_b = _dot(xb, wc_ref[:, d + c0:d + c0 + COL_CHUNK])
        g_c = _dot(xb, wc_ref[:, 2 * d + c0:2 * d + c0 + COL_CHUNK])
        u = g_c * h
        u_scr[CONV_HALO:CONV_HALO + tm, cs] = u
        u1 = u_scr[CONV_HALO - 1:CONV_HALO - 1 + tm, cs]
        u2 = u_scr[CONV_HALO - 2:CONV_HALO - 2 + tm, cs]
        y = (cw_ref[0:1, cs] * u2 + cw_ref[1:2, cs] * u1 + cw_ref[2:3, cs] * u
             + cb_ref[:, cs])
        c_scr[:, cs] = (g_b * y).astype(BF16)

    att = a_ref[...]
    cc = c_scr[...]
    for c0 in range(0, d, COL_CHUNK):
        cs = slice(c0, c0 + COL_CHUNK)
        y_attn = _dot(att, wpa_ref[:, cs])
        y_conv = _dot(cc, wpc_ref[:, cs])
        gate_a = jax.nn.sigmoid(_dot(xb, wg_ref[:, c0:c0 + COL_CHUNK]) + bg_ref[:, cs])
        gate_c = jax.nn.sigmoid(_dot(xb, wg_ref[:, d + c0:d + c0 + COL_CHUNK])
                                + bg_ref[:, d + c0:d + c0 + COL_CHUNK])
        mix_scr[:, cs] = (gate_a * y_attn + gate_c * y_conv).astype(BF16)

    mixed = _dot(mix_scr[...], wo_ref[...])
    o_ref[...] = _layer_norm(alpha * x + mixed, g_ref[...], b_ref[...])


def _mixer_tail(x, attn, w_c, w_g, b_gate, conv_w, conv_b, w_pa, w_pc, w_out, g, b, alpha, seq):
    t, d = x.shape
    assert t % TOKEN_TILE == 0 and seq % TOKEN_TILE == 0 and d % COL_CHUNK == 0
    assert w_pa.shape == (d, d) and w_pc.shape == (d, d) and conv_w.shape == (CONV_K, d)
    tile = lambda i: (i, 0)
    return pl.pallas_call(
        functools.partial(_tail_kernel, alpha=alpha, tiles_per_seq=seq // TOKEN_TILE),
        grid=(t // TOKEN_TILE,),
        in_specs=[
            pl.BlockSpec((TOKEN_TILE, d), tile),
            pl.BlockSpec((TOKEN_TILE, d), tile),
            _resident((d, 3 * d)),
            _resident((d, 2 * d)),
            _resident((1, 2 * d)),
            _resident((CONV_K, d)),
            _resident((1, d)),
            _resident((d, d)),
            _resident((d, d)),
            _resident((d, d)),
            _resident((1, d)),
            _resident((1, d)),
        ],
        out_specs=pl.BlockSpec((TOKEN_TILE, d), tile),
        out_shape=jax.ShapeDtypeStruct((t, d), F32),
        scratch_shapes=[
            pltpu.VMEM((TOKEN_TILE + CONV_HALO, d), F32),
            pltpu.VMEM((TOKEN_TILE, d), BF16),
            pltpu.VMEM((TOKEN_TILE, d), BF16),
        ],
        compiler_params=pltpu.CompilerParams(
            dimension_semantics=("arbitrary",), vmem_limit_bytes=VMEM_LIMIT),
        name="mixer_tail",
    )(x, attn, w_c, w_g, b_gate, conv_w, conv_b, w_pa, w_pc, w_out, g, b)


def kernel(x, ffn1_w_up, ffn1_w_down, ln1_g, ln1_b, w_in, b_gate, conv_w, conv_b,
           w_proj_attn, w_proj_conv, w_out, ln2_g, ln2_b, ffn2_w_up, ffn2_w_down,
           ln3_g, ln3_b):
    batch, seq, d = x.shape
    depth = w_in.shape[0]
    a = N_HEADS * HEAD_DIM
    assert w_in.shape[2] == 3 * a + 3 * d + 2 * d and w_proj_attn.shape[1] == a
    alpha = (2.0 * depth) ** 0.25
    row = lambda v: v.reshape(1, -1)

    h = x.reshape(batch * seq, d)
    for l in range(depth):
        w = w_in[l]
        w_k = w[:, a:2 * a].astype(BF16)
        w_qvt = jnp.concatenate([w[:, :a], w[:, 2 * a:3 * a]], axis=1).T.astype(BF16)
        w_c = w[:, 3 * a:3 * a + 3 * d].astype(BF16)
        w_g = w[:, 3 * a + 3 * d:].astype(BF16)

        h = _ffn_ln(h, ffn1_w_up[l].astype(BF16), ffn1_w_down[l].astype(BF16),
                    row(ln1_g[l]), row(ln1_b[l]), alpha)
        k, qvt, kmean = _qkv_proj(h, w_k, w_qvt)
        attn = _moba_attention(k, qvt, kmean.reshape(batch, seq // MOBA_BLOCK, a), batch, seq)
        h = _mixer_tail(h, attn, w_c, w_g, row(b_gate[l]), conv_w[l], row(conv_b[l]),
                        w_proj_attn[l].astype(BF16), w_proj_conv[l].astype(BF16),
                        w_out[l].astype(BF16), row(ln2_g[l]), row(ln2_b[l]), alpha, seq)
        h = _ffn_ln(h, ffn2_w_up[l].astype(BF16), ffn2_w_down[l].astype(BF16),
                    row(ln3_g[l]), row(ln3_b[l]), alpha)
    return h.reshape(batch, seq, d)
```

```python
import functools
import math

import jax
import jax.numpy as jnp
from jax import lax
from jax.experimental import pallas as pl
from jax.experimental.pallas import tpu as pltpu

N_HEADS = 8
HEAD_DIM = 128
MOBA_BLOCK = 256
MOBA_TOPK = 3
CONV_K = 3
LN_EPS = 1e-5

F32 = jnp.float32
BF16 = jnp.bfloat16

TOKEN_TILE = 256
FFN_TILE = 512
SUB_TILE = 256
FF_CHUNK = 256
COL_CHUNK = 256
CONV_HALO = 8
VMEM_LIMIT = 48 * 1024 * 1024


def _resident(shape):
    nd = len(shape)
    return pl.BlockSpec(shape, lambda *_: (0,) * nd, pipeline_mode=pl.Buffered(1))


def _layer_norm(y, g, b):
    mu = jnp.mean(y, axis=-1, keepdims=True)
    d = y - mu
    var = jnp.mean(d * d, axis=-1, keepdims=True)
    return d * lax.rsqrt(var + LN_EPS) * g + b


def _dot(a, b):
    return jnp.dot(a, b, preferred_element_type=F32)


def _ffn_ln_kernel(x_ref, wup_ref, wdn_ref, g_ref, b_ref, o_ref, *, alpha, d_ff):
    chunks = list(range(0, d_ff, FF_CHUNK))
    for r0 in range(0, x_ref.shape[0], SUB_TILE):
        x = x_ref[r0:r0 + SUB_TILE, :]
        xb = x.astype(BF16)

        def gate_up(c0):
            return (_dot(xb, wup_ref[:, c0:c0 + FF_CHUNK]),
                    _dot(xb, wup_ref[:, d_ff + c0:d_ff + c0 + FF_CHUNK]))

        pending = gate_up(chunks[0])
        acc = None
        for j, c0 in enumerate(chunks):
            gate, up = pending
            if j + 1 < len(chunks):
                pending = gate_up(chunks[j + 1])
            h = (gate * jax.nn.sigmoid(gate) * up).astype(BF16)
            down = _dot(h, wdn_ref[c0:c0 + FF_CHUNK, :])
            acc = down if acc is None else acc + down
        o_ref[r0:r0 + SUB_TILE, :] = _layer_norm(alpha * x + 0.5 * acc, g_ref[...], b_ref[...])


def _ffn_ln(x, w_up, w_down, g, b, alpha):
    t, d = x.shape
    d_ff = w_down.shape[0]
    assert t % FFN_TILE == 0 and FFN_TILE % SUB_TILE == 0 and d_ff % FF_CHUNK == 0
    return pl.pallas_call(
        functools.partial(_ffn_ln_kernel, alpha=alpha, d_ff=d_ff),
        grid=(t // FFN_TILE,),
        in_specs=[
            pl.BlockSpec((FFN_TILE, d), lambda i: (i, 0)),
            _resident((d, 2 * d_ff)),
            _resident((d_ff, d)),
            _resident((1, d)),
            _resident((1, d)),
        ],
        out_specs=pl.BlockSpec((FFN_TILE, d), lambda i: (i, 0)),
        out_shape=jax.ShapeDtypeStruct((t, d), F32),
        compiler_params=pltpu.CompilerParams(
            dimension_semantics=("parallel",), vmem_limit_bytes=VMEM_LIMIT),
        name="ffn_ln",
    )(x, w_up, w_down, g, b)


def _qkv_kernel(x_ref, wk_ref, wqvt_ref, k_ref, qvt_ref, kmean_ref, *, q_scale):
    a = k_ref.shape[1]
    xb = x_ref[...].astype(BF16)
    k = _dot(xb, wk_ref[...])
    k_ref[...] = k.astype(BF16)
    for j in range(TOKEN_TILE // MOBA_BLOCK):
        blk = k[j * MOBA_BLOCK:(j + 1) * MOBA_BLOCK]
        kmean_ref[0, j:j + 1, :] = jnp.sum(blk, axis=0, keepdims=True) * (1.0 / MOBA_BLOCK)
    qvt = lax.dot_general(wqvt_ref[...], xb, (((1,), (1,)), ((), ())),
                          preferred_element_type=F32)
    qvt_ref[0:a, :] = (qvt[0:a] * q_scale).astype(BF16)
    qvt_ref[a:, :] = qvt[a:].astype(BF16)


def _qkv_proj(x, w_k, w_qvt):
    t, d = x.shape
    a = w_k.shape[1]
    q_scale = (HEAD_DIM ** -0.5) * math.log2(math.e)
    assert t % TOKEN_TILE == 0 and TOKEN_TILE % MOBA_BLOCK == 0
    n_tiles, blocks_per_tile = t // TOKEN_TILE, TOKEN_TILE // MOBA_BLOCK
    return pl.pallas_call(
        functools.partial(_qkv_kernel, q_scale=q_scale),
        grid=(n_tiles,),
        in_specs=[
            pl.BlockSpec((TOKEN_TILE, d), lambda i: (i, 0)),
            _resident((d, a)),
            _resident((2 * a, d)),
        ],
        out_specs=[
            pl.BlockSpec((TOKEN_TILE, a), lambda i: (i, 0)),
            pl.BlockSpec((2 * a, TOKEN_TILE), lambda i: (0, i)),
            pl.BlockSpec((1, blocks_per_tile, a), lambda i: (i, 0, 0)),
        ],
        out_shape=[
            jax.ShapeDtypeStruct((t, a), BF16),
            jax.ShapeDtypeStruct((2 * a, t), BF16),
            jax.ShapeDtypeStruct((n_tiles, blocks_per_tile, a), F32),
        ],
        compiler_params=pltpu.CompilerParams(
            dimension_semantics=("parallel",), vmem_limit_bytes=VMEM_LIMIT),
        name="qkv_proj",
    )(x, w_k, w_qvt)


def _moba_kernel(qt_ref, k_ref, vt_ref, km_ref, o_ref, s_scr, p_scr, *, n_blocks):
    blk = MOBA_BLOCK
    km = km_ref[...]
    km_hi = km.astype(BF16)
    km_lo = (km - km_hi.astype(F32)).astype(BF16)
    blk_id = lax.broadcasted_iota(jnp.int32, (n_blocks, blk), 0)
    key_pos = lax.broadcasted_iota(jnp.int32, (blk, blk), 0)
    qry_pos = lax.broadcasted_iota(jnp.int32, (blk, blk), 1)
    causal = key_pos <= qry_pos

    def q_tile(qb):
        return qt_ref[:, qb * blk:(qb + 1) * blk]

    def choice_bias(qb):
        qt = q_tile(qb)
        gate = _dot(km_hi, qt) + _dot(km_lo, qt)
        past = blk_id < qb
        gate = jnp.where(past, gate, -jnp.inf)
        beaten = jnp.zeros(gate.shape, jnp.int32)
        for m in range(qb):
            gm = gate[m:m + 1, :]
            wins = (gm > gate) | ((gm == gate) & (blk_id > m))
            beaten = beaten + wins.astype(jnp.int32)
        return jnp.where(past & (beaten < MOBA_TOPK), 0.0, -jnp.inf)

    biases = {qb: choice_bias(qb) for qb in range(MOBA_TOPK + 1, n_blocks)}

    def scores(qb):
        qt = q_tile(qb)
        m_run = None
        for n in range(qb + 1):
            s = _dot(k_ref[n * blk:(n + 1) * blk, :], qt)
            if n == qb:
                s = jnp.where(causal, s, -jnp.inf)
            elif qb in biases:
                s = s + biases[qb][n:n + 1, :]
            s_scr[qb % 2, n] = s
            mx = jnp.max(s, axis=0, keepdims=True)
            m_run = mx if m_run is None else jnp.maximum(m_run, mx)
        return m_run

    def finish(qb, m_run):
        l_sum = None
        for n in range(qb + 1):
            p = jnp.exp2(s_scr[qb % 2, n] - m_run)
            ps = jnp.sum(p, axis=0, keepdims=True)
            l_sum = ps if l_sum is None else l_sum + ps
            p_scr[qb % 2, n * blk:(n + 1) * blk, :] = p.astype(BF16)
        nk = (qb + 1) * blk
        acc_t = _dot(vt_ref[:, :nk], p_scr[qb % 2, :nk, :])
        out = (acc_t * (1.0 / l_sum)).T
        o_ref[qb * blk:(qb + 1) * blk, :] = out.astype(o_ref.dtype)

    m_next = scores(0)
    for qb in range(n_blocks):
        m_cur = m_next
        if qb + 1 < n_blocks:
            m_next = scores(qb + 1)
        finish(qb, m_cur)


def _moba_attention(k, qvt, kmean, batch, seq):
    t, a = k.shape
    hd = HEAD_DIM
    n_heads = a // hd
    n_blocks = seq // MOBA_BLOCK
    assert seq % MOBA_BLOCK == 0 and kmean.shape == (batch, n_blocks, a)
    return pl.pallas_call(
        functools.partial(_moba_kernel, n_blocks=n_blocks),
        grid=(batch, n_heads),
        in_specs=[
            pl.BlockSpec((hd, seq), lambda b, h: (h, b)),
            pl.BlockSpec((seq, hd), lambda b, h: (b, h)),
            pl.BlockSpec((hd, seq), lambda b, h: (n_heads + h, b)),
            pl.BlockSpec((None, n_blocks, hd), lambda b, h: (b, 0, h)),
        ],
        out_specs=pl.BlockSpec((seq, hd), lambda b, h: (b, h)),
        out_shape=jax.ShapeDtypeStruct((t, a), BF16),
        scratch_shapes=[
            pltpu.VMEM((2, n_blocks, MOBA_BLOCK, MOBA_BLOCK), F32),
            pltpu.VMEM((2, seq, MOBA_BLOCK), BF16),
        ],
        compiler_params=pltpu.CompilerParams(
            dimension_semantics=("parallel", "parallel"), vmem_limit_bytes=VMEM_LIMIT),
        name="moba_attn",
    )(qvt, k, qvt, kmean)


def _tail_kernel(x_ref, a_ref, wc_ref, wg_ref, bg_ref, cw_ref, cb_ref, wpa_ref, wpc_ref,
                 wo_ref, g_ref, b_ref, o_ref, u_scr, c_scr, mix_scr, *, alpha, tiles_per_seq):
    tm, d = x_ref.shape
    i = pl.program_id(0)
    x = x_ref[...]
    xb = x.astype(BF16)

    @pl.when(i % tiles_per_seq == 0)
    def _():
        u_scr[0:CONV_HALO, :] = jnp.zeros((CONV_HALO, d), F32)

    @pl.when(i % tiles_per_seq != 0)
    def _():
        u_scr[0:CONV_HALO, :] = u_scr[tm:tm + CONV_HALO, :]

    for c0 in range(0, d, COL_CHUNK):
        cs = slice(c0, c0 + COL_CHUNK)
        h = _dot(xb, wc_ref[:, c0:c0 + COL_CHUNK])
        g_b = _dot(xb, wc_ref[:, d + c0:d + c0 + COL_CHUNK])
        g_c = _dot(xb, wc_ref[:, 2 * d + c0:2 * d + c0 + COL_CHUNK])
        u = g_c * h
        u_scr[CONV_HALO:CONV_HALO + tm, cs] = u
        u1 = u_scr[CONV_HALO - 1:CONV_HALO - 1 + tm, cs]
        u2 = u_scr[CONV_HALO - 2:CONV_HALO - 2 + tm, cs]
        y = (cw_ref[0:1, cs] * u2 + cw_ref[1:2, cs] * u1 + cw_ref[2:3, cs] * u
             + cb_ref[:, cs])
        c_scr[:, cs] = (g_b * y).astype(BF16)

    att = a_ref[...]
    cc = c_scr[...]
    for c0 in range(0, d, COL_CHUNK):
        cs = slice(c0, c0 + COL_CHUNK)
        y_attn = _dot(att, wpa_ref[:, cs])
        y_conv = _dot(cc, wpc_ref[:, cs])
        gate_a = jax.nn.sigmoid(_dot(xb, wg_ref[:, c0:c0 + COL_CHUNK]) + bg_ref[:, cs])
        gate_c = jax.nn.sigmoid(_dot(xb, wg_ref[:, d + c0:d + c0 + COL_CHUNK])
                                + bg_ref[:, d + c0:d + c0 + COL_CHUNK])
        mix_scr[:, cs] = (gate_a * y_attn + gate_c * y_conv).astype(BF16)

    mixed = _dot(mix_scr[...], wo_ref[...])
    o_ref[...] = _layer_norm(alpha * x + mixed, g_ref[...], b_ref[...])


def _mixer_tail(x, attn, w_c, w_g, b_gate, conv_w, conv_b, w_pa, w_pc, w_out, g, b, alpha, seq):
    t, d = x.shape
    assert t % TOKEN_TILE == 0 and seq % TOKEN_TILE == 0 and d % COL_CHUNK == 0
    assert w_pa.shape == (d, d) and w_pc.shape == (d, d) and conv_w.shape == (CONV_K, d)
    tile = lambda i: (i, 0)
    return pl.pallas_call(
        functools.partial(_tail_kernel, alpha=alpha, tiles_per_seq=seq // TOKEN_TILE),
        grid=(t // TOKEN_TILE,),
        in_specs=[
            pl.BlockSpec((TOKEN_TILE, d), tile),
            pl.BlockSpec((TOKEN_TILE, d), tile),
            _resident((d, 3 * d)),
            _resident((d, 2 * d)),
            _resident((1, 2 * d)),
            _resident((CONV_K, d)),
            _resident((1, d)),
            _resident((d, d)),
            _resident((d, d)),
            _resident((d, d)),
            _resident((1, d)),
            _resident((1, d)),
        ],
        out_specs=pl.BlockSpec((TOKEN_TILE, d), tile),
        out_shape=jax.ShapeDtypeStruct((t, d), F32),
        scratch_shapes=[
            pltpu.VMEM((TOKEN_TILE + CONV_HALO, d), F32),
            pltpu.VMEM((TOKEN_TILE, d), BF16),
            pltpu.VMEM((TOKEN_TILE, d), BF16),
        ],
        compiler_params=pltpu.CompilerParams(
            dimension_semantics=("arbitrary",), vmem_limit_bytes=VMEM_LIMIT),
        name="mixer_tail",
    )(x, attn, w_c, w_g, b_gate, conv_w, conv_b, w_pa, w_pc, w_out, g, b)


def kernel(x, ffn1_w_up, ffn1_w_down, ln1_g, ln1_b, w_in, b_gate, conv_w, conv_b,
           w_proj_attn, w_proj_conv, w_out, ln2_g, ln2_b, ffn2_w_up, ffn2_w_down,
           ln3_g, ln3_b):
    batch, seq, d = x.shape
    depth = w_in.shape[0]
    a = N_HEADS * HEAD_DIM
    assert w_in.shape[2] == 3 * a + 3 * d + 2 * d and w_proj_attn.shape[1] == a
    alpha = (2.0 * depth) ** 0.25
    row = lambda v: v.reshape(1, -1)

    h = x.reshape(batch * seq, d)
    for l in range(depth):
        w = w_in[l]
        w_k = w[:, a:2 * a].astype(BF16)
        w_qvt = jnp.concatenate([w[:, :a], w[:, 2 * a:3 * a]], axis=1).T.astype(BF16)
        w_c = w[:, 3 * a:3 * a + 3 * d].astype(BF16)
        w_g = w[:, 3 * a + 3 * d:].astype(BF16)

        h = _ffn_ln(h, ffn1_w_up[l].astype(BF16), ffn1_w_down[l].astype(BF16),
                    row(ln1_g[l]), row(ln1_b[l]), alpha)
        k, qvt, kmean = _qkv_proj(h, w_k, w_qvt)
        attn = _moba_attention(k, qvt, kmean.reshape(batch, seq // MOBA_BLOCK, a), batch, seq)
        h = _mixer_tail(h, attn, w_c, w_g, row(b_gate[l]), conv_w[l], row(conv_b[l]),
                        w_proj_attn[l].astype(BF16), w_proj_conv[l].astype(BF16),
                        w_out[l].astype(BF16), row(ln2_g[l]), row(ln2_b[l]), alpha, seq)
        h = _ffn_ln(h, ffn2_w_up[l].astype(BF16), ffn2_w_down[l].astype(BF16),
                    row(ln3_g[l]), row(ln3_b[l]), alpha)
    return h.reshape(batch, seq, d)
```

```python
import functools
import math

import jax
import jax.numpy as jnp
from jax import lax
from jax.experimental import pallas as pl
from jax.experimental.pallas import tpu as pltpu

N_HEADS = 8
HEAD_DIM = 128
MOBA_BLOCK = 256
MOBA_TOPK = 3
CONV_K = 3
LN_EPS = 1e-5

F32 = jnp.float32
BF16 = jnp.bfloat16

TOKEN_TILE = 256
FFN_TILE = 1024
TAIL_TILE = 512
SUB_TILE = 256
FF_CHUNK = 256
COL_CHUNK = 256
CONV_HALO = 8
BF16_SUBLANES = 16
VMEM_LIMIT = 48 * 1024 * 1024


def _resident(shape):
    nd = len(shape)
    return pl.BlockSpec(shape, lambda *_: (0,) * nd, pipeline_mode=pl.Buffered(1))


def _layer_norm(y, g, b):
    mu = jnp.mean(y, axis=-1, keepdims=True)
    d = y - mu
    var = jnp.mean(d * d, axis=-1, keepdims=True)
    return d * lax.rsqrt(var + LN_EPS) * g + b


def _dot(a, b):
    return jnp.dot(a, b, preferred_element_type=F32)


def _ffn_ln_kernel(x_ref, wup_ref, wdn_ref, g_ref, b_ref, o_ref, *, alpha, d_ff):
    chunks = list(range(0, d_ff, FF_CHUNK))
    for r0 in range(0, x_ref.shape[0], SUB_TILE):
        x = x_ref[r0:r0 + SUB_TILE, :]
        xb = x.astype(BF16)

        def gate_up(c0):
            return (_dot(xb, wup_ref[:, c0:c0 + FF_CHUNK]),
                    _dot(xb, wup_ref[:, d_ff + c0:d_ff + c0 + FF_CHUNK]))

        pending = gate_up(chunks[0])
        acc = None
        for j, c0 in enumerate(chunks):
            gate, up = pending
            if j + 1 < len(chunks):
                pending = gate_up(chunks[j + 1])
            h = (gate * jax.nn.sigmoid(gate) * up).astype(BF16)
            down = _dot(h, wdn_ref[c0:c0 + FF_CHUNK, :])
            acc = down if acc is None else acc + down
        o_ref[r0:r0 + SUB_TILE, :] = _layer_norm(alpha * x + 0.5 * acc, g_ref[...], b_ref[...])


def _ffn_ln(x, w_up, w_down, g, b, alpha):
    t, d = x.shape
    d_ff = w_down.shape[0]
    assert t % FFN_TILE == 0 and FFN_TILE % SUB_TILE == 0 and d_ff % FF_CHUNK == 0
    return pl.pallas_call(
        functools.partial(_ffn_ln_kernel, alpha=alpha, d_ff=d_ff),
        grid=(t // FFN_TILE,),
        in_specs=[
            pl.BlockSpec((FFN_TILE, d), lambda i: (i, 0)),
            _resident((d, 2 * d_ff)),
            _resident((d_ff, d)),
            _resident((1, d)),
            _resident((1, d)),
        ],
        out_specs=pl.BlockSpec((FFN_TILE, d), lambda i: (i, 0)),
        out_shape=jax.ShapeDtypeStruct((t, d), F32),
        compiler_params=pltpu.CompilerParams(
            dimension_semantics=("parallel",), vmem_limit_bytes=VMEM_LIMIT),
        name="ffn_ln",
    )(x, w_up, w_down, g, b)


def _qkv_kernel(x_ref, wk_ref, wqvt_ref, k_ref, qvt_ref, kmean_ref, *, q_scale):
    a = k_ref.shape[1]
    xb = x_ref[...].astype(BF16)
    k = _dot(xb, wk_ref[...])
    k_ref[...] = k.astype(BF16)
    for j in range(TOKEN_TILE // MOBA_BLOCK):
        blk = k[j * MOBA_BLOCK:(j + 1) * MOBA_BLOCK]
        kmean_ref[0, j:j + 1, :] = jnp.sum(blk, axis=0, keepdims=True) * (1.0 / MOBA_BLOCK)
    qvt = lax.dot_general(wqvt_ref[...], xb, (((1,), (1,)), ((), ())),
                          preferred_element_type=F32)
    qvt_ref[0:a, :] = (qvt[0:a] * q_scale).astype(BF16)
    qvt_ref[a:, :] = qvt[a:].astype(BF16)


def _qkv_proj(x, w_k, w_qvt):
    t, d = x.shape
    a = w_k.shape[1]
    q_scale = (HEAD_DIM ** -0.5) * math.log2(math.e)
    assert t % TOKEN_TILE == 0 and TOKEN_TILE % MOBA_BLOCK == 0
    n_tiles, blocks_per_tile = t // TOKEN_TILE, TOKEN_TILE // MOBA_BLOCK
    return pl.pallas_call(
        functools.partial(_qkv_kernel, q_scale=q_scale),
        grid=(n_tiles,),
        in_specs=[
            pl.BlockSpec((TOKEN_TILE, d), lambda i: (i, 0)),
            _resident((d, a)),
            _resident((2 * a, d)),
        ],
        out_specs=[
            pl.BlockSpec((TOKEN_TILE, a), lambda i: (i, 0)),
            pl.BlockSpec((2 * a, TOKEN_TILE), lambda i: (0, i)),
            pl.BlockSpec((1, blocks_per_tile, a), lambda i: (i, 0, 0)),
        ],
        out_shape=[
            jax.ShapeDtypeStruct((t, a), BF16),
            jax.ShapeDtypeStruct((2 * a, t), BF16),
            jax.ShapeDtypeStruct((n_tiles, blocks_per_tile, a), F32),
        ],
        compiler_params=pltpu.CompilerParams(
            dimension_semantics=("parallel",), vmem_limit_bytes=VMEM_LIMIT),
        name="qkv_proj",
    )(x, w_k, w_qvt)


def _moba_kernel(qt_ref, k_ref, vt_ref, km_ref, o_ref, s_scr, p_scr, vt1_scr, *, n_blocks):
    blk = MOBA_BLOCK
    km = km_ref[...]
    km_hi = km.astype(BF16).astype(F32)
    km_split = jnp.concatenate([km_hi, km - km_hi], axis=0).astype(BF16)
    blk_id = lax.broadcasted_iota(jnp.int32, (n_blocks, blk), 0)
    key_pos = lax.broadcasted_iota(jnp.int32, (blk, blk), 0)
    qry_pos = lax.broadcasted_iota(jnp.int32, (blk, blk), 1)
    causal = key_pos <= qry_pos

    def q_tile(qb):
        return qt_ref[:, qb * blk:(qb + 1) * blk]

    def choice_bias(qb):
        g2 = _dot(km_split, q_tile(qb))
        gate = g2[:n_blocks] + g2[n_blocks:]
        past = blk_id < qb
        gate = jnp.where(past, gate, -jnp.inf)
        beaten = jnp.zeros(gate.shape, jnp.int32)
        for m in range(qb):
            gm = gate[m:m + 1, :]
            wins = (gm > gate) | ((gm == gate) & (blk_id > m))
            beaten = beaten + wins.astype(jnp.int32)
        return jnp.where(past & (beaten < MOBA_TOPK), 0.0, -jnp.inf)

    biases = {qb: choice_bias(qb) for qb in range(MOBA_TOPK + 1, n_blocks)}

    hd = vt_ref.shape[0]
    vt1_scr[0:hd, :] = vt_ref[...]
    vt1_scr[hd:, :] = jnp.ones((vt1_scr.shape[0] - hd, vt1_scr.shape[1]), BF16)

    def slots(group):
        base, out = 0, []
        for qb in group:
            out.append(base)
            base += qb + 1
        assert base <= s_scr.shape[1]
        return out

    def scores(group, half):
        maxes = []
        for qb, base in zip(group, slots(group)):
            qt = q_tile(qb)
            m8 = None
            for n in range(qb + 1):
                s = _dot(k_ref[n * blk:(n + 1) * blk, :], qt)
                if n == qb:
                    s = jnp.where(causal, s, -jnp.inf)
                elif qb in biases:
                    s = s + biases[qb][n:n + 1, :]
                s_scr[half, base + n] = s
                mx = jnp.max(s.reshape(blk // 8, 8, blk), axis=0)
                m8 = mx if m8 is None else jnp.maximum(m8, mx)
            maxes.append(jnp.max(m8, axis=0, keepdims=True))
        return maxes

    def finish(group, half, maxes):
        for qb, base, m_run in zip(group, slots(group), maxes):
            m8 = jnp.broadcast_to(m_run, (8, blk))
            for n in range(qb + 1):
                p = jnp.exp2(s_scr[half, base + n].reshape(blk // 8, 8, blk) - m8)
                p_scr[half, (base + n) * blk:(base + n + 1) * blk, :] = (
                    p.reshape(blk, blk).astype(BF16))
            nk = (qb + 1) * blk
            acc = _dot(vt1_scr[:, :nk], p_scr[half, base * blk:base * blk + nk, :])
            out = (acc[:hd] * (1.0 / acc[hd:hd + 1])).T
            o_ref[qb * blk:(qb + 1) * blk, :] = out.astype(o_ref.dtype)

    first = MOBA_TOPK
    rest = [qb for qb in range(n_blocks - 1, -1, -1) if qb != first]
    groups = [[first]]
    while rest:
        big = rest.pop(0)
        small = [qb for qb in rest if big + qb + 2 <= n_blocks][:1]
        for qb in small:
            rest.remove(qb)
        groups.append([big] + small)
    m_next = scores(groups[0], 0)
    for j, group in enumerate(groups):
        m_cur = m_next
        if j + 1 < len(groups):
            m_next = scores(groups[j + 1], (j + 1) % 2)
        finish(group, j % 2, m_cur)


def _moba_attention(k, qvt, kmean, batch, seq):
    t, a = k.shape
    hd = HEAD_DIM
    n_heads = a // hd
    n_blocks = seq // MOBA_BLOCK
    assert seq % MOBA_BLOCK == 0 and kmean.shape == (batch, n_blocks, a)
    return pl.pallas_call(
        functools.partial(_moba_kernel, n_blocks=n_blocks),
        grid=(batch, n_heads),
        in_specs=[
            pl.BlockSpec((hd, seq), lambda b, h: (h, b)),
            pl.BlockSpec((seq, hd), lambda b, h: (b, h)),
            pl.BlockSpec((hd, seq), lambda b, h: (n_heads + h, b)),
            pl.BlockSpec((None, n_blocks, hd), lambda b, h: (b, 0, h)),
        ],
        out_specs=pl.BlockSpec((seq, hd), lambda b, h: (b, h)),
        out_shape=jax.ShapeDtypeStruct((t, a), BF16),
        scratch_shapes=[
            pltpu.VMEM((2, n_blocks, MOBA_BLOCK, MOBA_BLOCK), F32),
            pltpu.VMEM((2, seq, MOBA_BLOCK), BF16),
            pltpu.VMEM((hd + BF16_SUBLANES, seq), BF16),
        ],
        compiler_params=pltpu.CompilerParams(
            dimension_semantics=("parallel", "parallel"), vmem_limit_bytes=VMEM_LIMIT),
        name="moba_attn",
    )(qvt, k, qvt, kmean)


def _tail_kernel(x_ref, a_ref, wc_ref, wg_ref, bg_ref, cw_ref, cb_ref, wpa_ref, wpc_ref,
                 wo_ref, g_ref, b_ref, o_ref, u_scr, c_scr, mix_scr, *, alpha, tiles_per_seq):
    tm, d = x_ref.shape
    i = pl.program_id(0)

    @pl.when(i % tiles_per_seq == 0)
    def _():
        u_scr[0:CONV_HALO, :] = jnp.zeros((CONV_HALO, d), F32)

    @pl.when(i % tiles_per_seq != 0)
    def _():
        u_scr[0:CONV_HALO, :] = u_scr[tm:tm + CONV_HALO, :]

    for r0 in range(0, tm, SUB_TILE):
        rows = slice(r0, r0 + SUB_TILE)
        x = x_ref[rows, :]
        xb = x.astype(BF16)
        for c0 in range(0, d, COL_CHUNK):
            cs = slice(c0, c0 + COL_CHUNK)
            h = _dot(xb, wc_ref[:, c0:c0 + COL_CHUNK])
            g_b = _dot(xb, wc_ref[:, d + c0:d + c0 + COL_CHUNK])
            g_c = _dot(xb, wc_ref[:, 2 * d + c0:2 * d + c0 + COL_CHUNK])
            u = g_c * h
            top = CONV_HALO + r0
            u_scr[top:top + SUB_TILE, cs] = u
            u1 = u_scr[top - 1:top - 1 + SUB_TILE, cs]
            u2 = u_scr[top - 2:top - 2 + SUB_TILE, cs]
            y = (cw_ref[0:1, cs] * u2 + cw_ref[1:2, cs] * u1 + cw_ref[2:3, cs] * u
                 + cb_ref[:, cs])
            c_scr[rows, cs] = (g_b * y).astype(BF16)

        att = a_ref[rows, :]
        cc = c_scr[rows, :]
        for c0 in range(0, d, COL_CHUNK):
            cs = slice(c0, c0 + COL_CHUNK)
            y_attn = _dot(att, wpa_ref[:, cs])
            y_conv = _dot(cc, wpc_ref[:, cs])
            gate_a = jax.nn.sigmoid(_dot(xb, wg_ref[:, c0:c0 + COL_CHUNK]) + bg_ref[:, cs])
            gate_c = jax.nn.sigmoid(_dot(xb, wg_ref[:, d + c0:d + c0 + COL_CHUNK])
                                    + bg_ref[:, d + c0:d + c0 + COL_CHUNK])
            mix_scr[rows, cs] = (gate_a * y_attn + gate_c * y_conv).astype(BF16)

        mixed = _dot(mix_scr[rows, :], wo_ref[...])
        o_ref[rows, :] = _layer_norm(alpha * x + mixed, g_ref[...], b_ref[...])


def _mixer_tail(x, attn, w_c, w_g, b_gate, conv_w, conv_b, w_pa, w_pc, w_out, g, b, alpha, seq):
    t, d = x.shape
    assert t % TAIL_TILE == 0 and seq % TAIL_TILE == 0 and d % COL_CHUNK == 0
    assert TAIL_TILE % SUB_TILE == 0
    assert w_pa.shape == (d, d) and w_pc.shape == (d, d) and conv_w.shape == (CONV_K, d)
    tile = lambda i: (i, 0)
    return pl.pallas_call(
        functools.partial(_tail_kernel, alpha=alpha, tiles_per_seq=seq // TAIL_TILE),
        grid=(t // TAIL_TILE,),
        in_specs=[
            pl.BlockSpec((TAIL_TILE, d), tile),
            pl.BlockSpec((TAIL_TILE, d), tile),
            _resident((d, 3 * d)),
            _resident((d, 2 * d)),
            _resident((1, 2 * d)),
            _resident((CONV_K, d)),
            _resident((1, d)),
            _resident((d, d)),
            _resident((d, d)),
            _resident((d, d)),
            _resident((1, d)),
            _resident((1, d)),
        ],
        out_specs=pl.BlockSpec((TAIL_TILE, d), tile),
        out_shape=jax.ShapeDtypeStruct((t, d), F32),
        scratch_shapes=[
            pltpu.VMEM((TAIL_TILE + CONV_HALO, d), F32),
            pltpu.VMEM((TAIL_TILE, d), BF16),
            pltpu.VMEM((TAIL_TILE, d), BF16),
        ],
        compiler_params=pltpu.CompilerParams(
            dimension_semantics=("arbitrary",), vmem_limit_bytes=VMEM_LIMIT),
        name="mixer_tail",
    )(x, attn, w_c, w_g, b_gate, conv_w, conv_b, w_pa, w_pc, w_out, g, b)


def kernel(x, ffn1_w_up, ffn1_w_down, ln1_g, ln1_b, w_in, b_gate, conv_w, conv_b,
           w_proj_attn, w_proj_conv, w_out, ln2_g, ln2_b, ffn2_w_up, ffn2_w_down,
           ln3_g, ln3_b):
    batch, seq, d = x.shape
    depth = w_in.shape[0]
    a = N_HEADS * HEAD_DIM
    assert w_in.shape[2] == 3 * a + 3 * d + 2 * d and w_proj_attn.shape[1] == a
    alpha = (2.0 * depth) ** 0.25
    row = lambda v: v.reshape(1, -1)

    h = x.reshape(batch * seq, d)
    for l in range(depth):
        w = w_in[l]
        w_k = w[:, a:2 * a].astype(BF16)
        w_qvt = jnp.concatenate([w[:, :a], w[:, 2 * a:3 * a]], axis=1).T.astype(BF16)
        w_c = w[:, 3 * a:3 * a + 3 * d].astype(BF16)
        w_g = w[:, 3 * a + 3 * d:].astype(BF16)

        h = _ffn_ln(h, ffn1_w_up[l].astype(BF16), ffn1_w_down[l].astype(BF16),
                    row(ln1_g[l]), row(ln1_b[l]), alpha)
        k, qvt, kmean = _qkv_proj(h, w_k, w_qvt)
        attn = _moba_attention(k, qvt, kmean.reshape(batch, seq // MOBA_BLOCK, a), batch, seq)
        h = _mixer_tail(h, attn, w_c, w_g, row(b_gate[l]), conv_w[l], row(conv_b[l]),
                        w_proj_attn[l].astype(BF16), w_proj_conv[l].astype(BF16),
                        w_out[l].astype(BF16), row(ln2_g[l]), row(ln2_b[l]), alpha, seq)
        h = _ffn_ln(h, ffn2_w_up[l].astype(BF16), ffn2_w_down[l].astype(BF16),
                    row(ln3_g[l]), row(ln3_b[l]), alpha)
    return h.reshape(batch, seq, d)
```

```python
import functools
import math

import jax
import jax.numpy as jnp
from jax import lax
from jax.experimental import pallas as pl
from jax.experimental.pallas import tpu as pltpu

N_HEADS = 8
HEAD_DIM = 128
MOBA_BLOCK = 256
MOBA_TOPK = 3
CONV_K = 3
LN_EPS = 1e-5

F32 = jnp.float32
BF16 = jnp.bfloat16

TOKEN_TILE = 256
FFN_TILE = 512
TAIL_TILE = 512
SUB_TILE = 256
FF_CHUNK = 256
COL_CHUNK = 256
CONV_HALO = 8
BF16_SUBLANES = 16
VMEM_LIMIT = 56 * 1024 * 1024


def _resident(block, index):
    return pl.BlockSpec(block, lambda *_: index, pipeline_mode=pl.Buffered(1))


def _layer_norm(y, g, b):
    mu = jnp.mean(y, axis=-1, keepdims=True)
    d = y - mu
    var = jnp.mean(d * d, axis=-1, keepdims=True)
    return d * lax.rsqrt(var + LN_EPS) * g + b


def _dot(a, b):
    return jnp.dot(a, b, preferred_element_type=F32)


def _ffn_ln_kernel(x_ref, wup_ref, wdn_ref, g_ref, b_ref, o_ref, *, alpha, d_ff):
    chunks = list(range(0, d_ff, FF_CHUNK))
    for r0 in range(0, x_ref.shape[0], SUB_TILE):
        x = x_ref[r0:r0 + SUB_TILE, :]
        xb = x.astype(BF16)

        def gate_up(c0):
            return (_dot(xb, wup_ref[:, c0:c0 + FF_CHUNK]),
                    _dot(xb, wup_ref[:, d_ff + c0:d_ff + c0 + FF_CHUNK]))

        pending = gate_up(chunks[0])
        acc = None
        for j, c0 in enumerate(chunks):
            gate, up = pending
            if j + 1 < len(chunks):
                pending = gate_up(chunks[j + 1])
            h = (gate * jax.nn.sigmoid(gate) * up).astype(BF16)
            down = _dot(h, wdn_ref[c0:c0 + FF_CHUNK, :])
            acc = down if acc is None else acc + down
        o_ref[r0:r0 + SUB_TILE, :] = _layer_norm(alpha * x + 0.5 * acc, g_ref[...], b_ref[...])


def _ffn_ln(x, w_up, w_down, g, b, layer, alpha):
    t, d = x.shape
    d_ff = w_down.shape[1]
    assert t % FFN_TILE == 0 and FFN_TILE % SUB_TILE == 0 and d_ff % FF_CHUNK == 0
    return pl.pallas_call(
        functools.partial(_ffn_ln_kernel, alpha=alpha, d_ff=d_ff),
        grid=(t // FFN_TILE,),
        in_specs=[
            pl.BlockSpec((FFN_TILE, d), lambda i: (i, 0)),
            _resident((None, d, 2 * d_ff), (layer, 0, 0)),
            _resident((None, d_ff, d), (layer, 0, 0)),
            _resident((None, 1, d), (layer, 0, 0)),
            _resident((None, 1, d), (layer, 0, 0)),
        ],
        out_specs=pl.BlockSpec((FFN_TILE, d), lambda i: (i, 0)),
        out_shape=jax.ShapeDtypeStruct((t, d), F32),
        compiler_params=pltpu.CompilerParams(
            dimension_semantics=("parallel",), vmem_limit_bytes=VMEM_LIMIT),
        name="ffn_ln",
    )(x, w_up, w_down, g, b)


def _qkv_kernel(x_ref, wk_ref, wqvt_ref, k_ref, qvt_ref, kmean_ref, *, q_scale):
    a = k_ref.shape[1]
    xb = x_ref[...].astype(BF16)
    k = _dot(xb, wk_ref[...])
    k_ref[...] = k.astype(BF16)
    for j in range(TOKEN_TILE // MOBA_BLOCK):
        blk = k[j * MOBA_BLOCK:(j + 1) * MOBA_BLOCK]
        kmean_ref[0, j:j + 1, :] = jnp.sum(blk, axis=0, keepdims=True) * (1.0 / MOBA_BLOCK)
    qvt = lax.dot_general(wqvt_ref[...], xb, (((1,), (1,)), ((), ())),
                          preferred_element_type=F32)
    qvt_ref[0:a, :] = (qvt[0:a] * q_scale).astype(BF16)
    qvt_ref[a:, :] = qvt[a:].astype(BF16)


def _qkv_proj(x, w_in, w_qvt, layer):
    t, d = x.shape
    a = w_qvt.shape[0] // 2
    q_scale = (HEAD_DIM ** -0.5) * math.log2(math.e)
    assert t % TOKEN_TILE == 0 and TOKEN_TILE % MOBA_BLOCK == 0
    n_tiles, blocks_per_tile = t // TOKEN_TILE, TOKEN_TILE // MOBA_BLOCK
    return pl.pallas_call(
        functools.partial(_qkv_kernel, q_scale=q_scale),
        grid=(n_tiles,),
        in_specs=[
            pl.BlockSpec((TOKEN_TILE, d), lambda i: (i, 0)),
            _resident((None, d, a), (layer, 0, 1)),
            _resident((2 * a, d), (0, 0)),
        ],
        out_specs=[
            pl.BlockSpec((TOKEN_TILE, a), lambda i: (i, 0)),
            pl.BlockSpec((2 * a, TOKEN_TILE), lambda i: (0, i)),
            pl.BlockSpec((1, blocks_per_tile, a), lambda i: (i, 0, 0)),
        ],
        out_shape=[
            jax.ShapeDtypeStruct((t, a), BF16),
            jax.ShapeDtypeStruct((2 * a, t), BF16),
            jax.ShapeDtypeStruct((n_tiles, blocks_per_tile, a), F32),
        ],
        compiler_params=pltpu.CompilerParams(
            dimension_semantics=("parallel",), vmem_limit_bytes=VMEM_LIMIT),
        name="qkv_proj",
    )(x, w_in, w_qvt)


def _moba_kernel(qt_ref, k_ref, vt_ref, km_ref, o_ref, s_scr, p_scr, vt1_scr, *, n_blocks):
    blk = MOBA_BLOCK
    km = km_ref[...]
    km_hi = km.astype(BF16).astype(F32)
    km_split = jnp.concatenate([km_hi, km - km_hi], axis=0).astype(BF16)
    blk_id = lax.broadcasted_iota(jnp.int32, (n_blocks, blk), 0)
    key_pos = lax.broadcasted_iota(jnp.int32, (blk, blk), 0)
    qry_pos = lax.broadcasted_iota(jnp.int32, (blk, blk), 1)
    causal = key_pos <= qry_pos

    def q_tile(qb):
        return qt_ref[:, qb * blk:(qb + 1) * blk]

    def choice_bias(qb):
        g2 = _dot(km_split, q_tile(qb))
        gate = g2[:n_blocks] + g2[n_blocks:]
        past = blk_id < qb
        gate = jnp.where(past, gate, -jnp.inf)
        beaten = jnp.zeros(gate.shape, jnp.int32)
        for m in range(qb):
            gm = gate[m:m + 1, :]
            wins = (gm > gate) | ((gm == gate) & (blk_id > m))
            beaten = beaten + wins.astype(jnp.int32)
        return jnp.where(past & (beaten < MOBA_TOPK), 0.0, -jnp.inf)

    biases = {qb: choice_bias(qb) for qb in range(MOBA_TOPK + 1, n_blocks)}

    hd = vt_ref.shape[0]
    vt1_scr[0:hd, :] = vt_ref[...]
    vt1_scr[hd:, :] = jnp.ones((vt1_scr.shape[0] - hd, vt1_scr.shape[1]), BF16)

    def slots(group):
        base, out = 0, []
        for qb in group:
            out.append(base)
            base += qb + 1
        assert base <= s_scr.shape[1]
        return out

    def scores(group, half):
        maxes = []
        for qb, base in zip(group, slots(group)):
            qt = q_tile(qb)
            m8 = None
            for n in range(qb + 1):
                s = _dot(k_ref[n * blk:(n + 1) * blk, :], qt)
                if n == qb:
                    s = jnp.where(causal, s, -jnp.inf)
                elif qb in biases:
                    s = s + biases[qb][n:n + 1, :]
                s_scr[half, base + n] = s
                mx = jnp.max(s.reshape(blk // 8, 8, blk), axis=0)
                m8 = mx if m8 is None else jnp.maximum(m8, mx)
            maxes.append(jnp.max(m8, axis=0, keepdims=True))
        return maxes

    def finish(group, half, maxes):
        for qb, base, m_run in zip(group, slots(group), maxes):
            m8 = jnp.broadcast_to(m_run, (8, blk))
            for n in range(qb + 1):
                p = jnp.exp2(s_scr[half, base + n].reshape(blk // 8, 8, blk) - m8)
                p_scr[half, (base + n) * blk:(base + n + 1) * blk, :] = (
                    p.reshape(blk, blk).astype(BF16))
            nk = (qb + 1) * blk
            acc = _dot(vt1_scr[:, :nk], p_scr[half, base * blk:base * blk + nk, :])
            out = (acc[:hd] * (1.0 / acc[hd:hd + 1])).T
            o_ref[qb * blk:(qb + 1) * blk, :] = out.astype(o_ref.dtype)

    first = MOBA_TOPK
    rest = [qb for qb in range(n_blocks - 1, -1, -1) if qb != first]
    groups = [[first]]
    while rest:
        big = rest.pop(0)
        small = [qb for qb in rest if big + qb + 2 <= n_blocks][:1]
        for qb in small:
            rest.remove(qb)
        groups.append([big] + small)
    m_next = scores(groups[0], 0)
    for j, group in enumerate(groups):
        m_cur = m_next
        if j + 1 < len(groups):
            m_next = scores(groups[j + 1], (j + 1) % 2)
        finish(group, j % 2, m_cur)


def _moba_attention(k, qvt, kmean, batch, seq):
    t, a = k.shape
    hd = HEAD_DIM
    n_heads = a // hd
    n_blocks = seq // MOBA_BLOCK
    assert seq % MOBA_BLOCK == 0 and kmean.shape == (batch, n_blocks, a)
    return pl.pallas_call(
        functools.partial(_moba_kernel, n_blocks=n_blocks),
        grid=(batch, n_heads),
        in_specs=[
            pl.BlockSpec((hd, seq), lambda b, h: (h, b)),
            pl.BlockSpec((seq, hd), lambda b, h: (b, h)),
            pl.BlockSpec((hd, seq), lambda b, h: (n_heads + h, b)),
            pl.BlockSpec((None, n_blocks, hd), lambda b, h: (b, 0, h)),
        ],
        out_specs=pl.BlockSpec((seq, hd), lambda b, h: (b, h)),
        out_shape=jax.ShapeDtypeStruct((t, a), BF16),
        scratch_shapes=[
            pltpu.VMEM((2, n_blocks, MOBA_BLOCK, MOBA_BLOCK), F32),
            pltpu.VMEM((2, seq, MOBA_BLOCK), BF16),
            pltpu.VMEM((hd + BF16_SUBLANES, seq), BF16),
        ],
        compiler_params=pltpu.CompilerParams(
            dimension_semantics=("parallel", "parallel"), vmem_limit_bytes=VMEM_LIMIT),
        name="moba_attn",
    )(qvt, k, qvt, kmean)


def _tail_kernel(x_ref, a_ref, wc_ref, wg_ref, bg_ref, cw_ref, cb_ref, wpa_ref, wpc_ref,
                 wo_ref, g_ref, b_ref, o_ref, u_scr, c_scr, mix_scr, *, alpha, tiles_per_seq):
    tm, d = x_ref.shape
    i = pl.program_id(0)

    @pl.when(i % tiles_per_seq == 0)
    def _():
        u_scr[0:CONV_HALO, :] = jnp.zeros((CONV_HALO, d), F32)

    @pl.when(i % tiles_per_seq != 0)
    def _():
        u_scr[0:CONV_HALO, :] = u_scr[tm:tm + CONV_HALO, :]

    for r0 in range(0, tm, SUB_TILE):
        rows = slice(r0, r0 + SUB_TILE)
        x = x_ref[rows, :]
        xb = x.astype(BF16)
        for c0 in range(0, d, COL_CHUNK):
            cs = slice(c0, c0 + COL_CHUNK)
            h = _dot(xb, wc_ref[:, c0:c0 + COL_CHUNK])
            g_b = _dot(xb, wc_ref[:, d + c0:d + c0 + COL_CHUNK])
            g_c = _dot(xb, wc_ref[:, 2 * d + c0:2 * d + c0 + COL_CHUNK])
            u = g_c * h
            top = CONV_HALO + r0
            u_scr[top:top + SUB_TILE, cs] = u
            u1 = u_scr[top - 1:top - 1 + SUB_TILE, cs]
            u2 = u_scr[top - 2:top - 2 + SUB_TILE, cs]
            y = (cw_ref[0:1, cs] * u2 + cw_ref[1:2, cs] * u1 + cw_ref[2:3, cs] * u
                 + cb_ref[:, cs])
            c_scr[rows, cs] = (g_b * y).astype(BF16)

        att = a_ref[rows, :]
        cc = c_scr[rows, :]
        for c0 in range(0, d, COL_CHUNK):
            cs = slice(c0, c0 + COL_CHUNK)
            y_attn = _dot(att, wpa_ref[:, cs])
            y_conv = _dot(cc, wpc_ref[:, cs])
            gate_a = jax.nn.sigmoid(_dot(xb, wg_ref[:, c0:c0 + COL_CHUNK]) + bg_ref[:, cs])
            gate_c = jax.nn.sigmoid(_dot(xb, wg_ref[:, d + c0:d + c0 + COL_CHUNK])
                                    + bg_ref[:, d + c0:d + c0 + COL_CHUNK])
            mix_scr[rows, cs] = (gate_a * y_attn + gate_c * y_conv).astype(BF16)

        mixed = _dot(mix_scr[rows, :], wo_ref[...])
        o_ref[rows, :] = _layer_norm(alpha * x + mixed, g_ref[...], b_ref[...])


def _mixer_tail(x, attn, w_in, b_gate, conv_w, conv_b, w_pa, w_pc, w_out, g, b, layer, alpha, seq):
    t, d = x.shape
    a = attn.shape[1]
    assert t % TAIL_TILE == 0 and seq % TAIL_TILE == 0 and d % COL_CHUNK == 0
    assert TAIL_TILE % SUB_TILE == 0
    assert w_pa.shape[1:] == (a, d) and a == d, "column-block indexing of w_in assumes A == D"
    assert w_pc.shape[1:] == (d, d) and conv_w.shape[1:] == (CONV_K, d)
    assert w_in.shape[2] == 3 * a + 5 * d
    tile = lambda i: (i, 0)
    lyr = (layer, 0, 0)
    return pl.pallas_call(
        functools.partial(_tail_kernel, alpha=alpha, tiles_per_seq=seq // TAIL_TILE),
        grid=(t // TAIL_TILE,),
        in_specs=[
            pl.BlockSpec((TAIL_TILE, d), tile),
            pl.BlockSpec((TAIL_TILE, a), tile),
            _resident((None, d, 3 * d), (layer, 0, 1)),
            _resident((None, d, 2 * d), (layer, 0, 3)),
            _resident((None, 1, 2 * d), lyr),
            _resident((None, CONV_K, d), lyr),
            _resident((None, 1, d), lyr),
            _resident((None, a, d), lyr),
            _resident((None, d, d), lyr),
            _resident((None, d, d), lyr),
            _resident((None, 1, d), lyr),
            _resident((None, 1, d), lyr),
        ],
        out_specs=pl.BlockSpec((TAIL_TILE, d), tile),
        out_shape=jax.ShapeDtypeStruct((t, d), F32),
        scratch_shapes=[
            pltpu.VMEM((TAIL_TILE + CONV_HALO, d), F32),
            pltpu.VMEM((TAIL_TILE, d), BF16),
            pltpu.VMEM((TAIL_TILE, d), BF16),
        ],
        compiler_params=pltpu.CompilerParams(
            dimension_semantics=("arbitrary",), vmem_limit_bytes=VMEM_LIMIT),
        name="mixer_tail",
    )(x, attn, w_in, w_in, b_gate, conv_w, conv_b, w_pa, w_pc, w_out, g, b)


def kernel(x, ffn1_w_up, ffn1_w_down, ln1_g, ln1_b, w_in, b_gate, conv_w, conv_b,
           w_proj_attn, w_proj_conv, w_out, ln2_g, ln2_b, ffn2_w_up, ffn2_w_down,
           ln3_g, ln3_b):
    batch, seq, d = x.shape
    depth = w_in.shape[0]
    a = N_HEADS * HEAD_DIM
    assert w_in.shape[2] == 3 * a + 3 * d + 2 * d and w_proj_attn.shape[1] == a
    alpha = (2.0 * depth) ** 0.25
    rows = lambda v: v.reshape(depth, 1, -1)

    h = x.reshape(batch * seq, d)
    for l in range(depth):
        w_qvt = jnp.concatenate([w_in[l, :, :a], w_in[l, :, 2 * a:3 * a]], axis=1).T.astype(BF16)

        h = _ffn_ln(h, ffn1_w_up, ffn1_w_down, rows(ln1_g), rows(ln1_b), l, alpha)
        k, qvt, kmean = _qkv_proj(h, w_in, w_qvt, l)
        attn = _moba_attention(k, qvt, kmean.reshape(batch, seq // MOBA_BLOCK, a), batch, seq)
        h = _mixer_tail(h, attn, w_in, rows(b_gate), conv_w, rows(conv_b), w_proj_attn,
                        w_proj_conv, w_out, rows(ln2_g), rows(ln2_b), l, alpha, seq)
        h = _ffn_ln(h, ffn2_w_up, ffn2_w_down, rows(ln3_g), rows(ln3_b), l, alpha)
    return h.reshape(batch, seq, d)
```

```python
import functools
import math

import jax
import jax.numpy as jnp
from jax import lax
from jax.experimental import pallas as pl
from jax.experimental.pallas import tpu as pltpu

N_HEADS = 8
HEAD_DIM = 128
MOBA_BLOCK = 256
MOBA_TOPK = 3
CONV_K = 3
LN_EPS = 1e-5

F32 = jnp.float32
BF16 = jnp.bfloat16

TOKEN_TILE = 256
FFN_TILE = 512
TAIL_TILE = 512
SUB_TILE = 256
FF_CHUNK = 256
COL_CHUNK = 256
CONV_HALO = 8
BF16_SUBLANES = 16
PV_LAG = 4
ATTN_HEADS = 2
VMEM_LIMIT = 56 * 1024 * 1024


def _resident(block, index):
    return pl.BlockSpec(block, lambda *_: index, pipeline_mode=pl.Buffered(1))


def _layer_norm(y, g, b):
    mu = jnp.mean(y, axis=-1, keepdims=True)
    d = y - mu
    var = jnp.mean(d * d, axis=-1, keepdims=True)
    return d * lax.rsqrt(var + LN_EPS) * g + b


def _dot(a, b):
    return jnp.dot(a, b, preferred_element_type=F32)


def _ffn_ln_kernel(x_ref, wup_ref, wdn_ref, g_ref, b_ref, o_ref, *, alpha, d_ff):
    chunks = list(range(0, d_ff, FF_CHUNK))
    for r0 in range(0, x_ref.shape[0], SUB_TILE):
        x = x_ref[r0:r0 + SUB_TILE, :]
        xb = x.astype(BF16)

        def gate_up(c0):
            return (_dot(xb, wup_ref[:, c0:c0 + FF_CHUNK]),
                    _dot(xb, wup_ref[:, d_ff + c0:d_ff + c0 + FF_CHUNK]))

        pending = gate_up(chunks[0])
        acc = None
        for j, c0 in enumerate(chunks):
            gate, up = pending
            if j + 1 < len(chunks):
                pending = gate_up(chunks[j + 1])
            h = (gate * jax.nn.sigmoid(gate) * up).astype(BF16)
            down = _dot(h, wdn_ref[c0:c0 + FF_CHUNK, :])
            acc = down if acc is None else acc + down
        o_ref[r0:r0 + SUB_TILE, :] = _layer_norm(alpha * x + 0.5 * acc, g_ref[...], b_ref[...])


def _ffn_ln(x, w_up, w_down, g, b, layer, alpha):
    t, d = x.shape
    d_ff = w_down.shape[1]
    assert t % FFN_TILE == 0 and FFN_TILE % SUB_TILE == 0 and d_ff % FF_CHUNK == 0
    return pl.pallas_call(
        functools.partial(_ffn_ln_kernel, alpha=alpha, d_ff=d_ff),
        grid=(t // FFN_TILE,),
        in_specs=[
            pl.BlockSpec((FFN_TILE, d), lambda i: (i, 0)),
            _resident((None, d, 2 * d_ff), (layer, 0, 0)),
            _resident((None, d_ff, d), (layer, 0, 0)),
            _resident((None, 1, d), (layer, 0, 0)),
            _resident((None, 1, d), (layer, 0, 0)),
        ],
        out_specs=pl.BlockSpec((FFN_TILE, d), lambda i: (i, 0)),
        out_shape=jax.ShapeDtypeStruct((t, d), F32),
        compiler_params=pltpu.CompilerParams(
            dimension_semantics=("parallel",), vmem_limit_bytes=VMEM_LIMIT),
        name="ffn_ln",
    )(x, w_up, w_down, g, b)


def _qkv_kernel(x_ref, wq_ref, wk_ref, wv_ref, k_ref, qvt_ref, kmean_ref, wqvt_scr, *, q_scale):
    a = k_ref.shape[1]

    @pl.when(pl.program_id(0) == 0)
    def _():
        for c0 in range(0, wq_ref.shape[0], COL_CHUNK):
            wqvt_scr[0:a, c0:c0 + COL_CHUNK] = wq_ref[c0:c0 + COL_CHUNK, :].T.astype(BF16)
            wqvt_scr[a:, c0:c0 + COL_CHUNK] = wv_ref[c0:c0 + COL_CHUNK, :].T.astype(BF16)

    xb = x_ref[...].astype(BF16)
    k = _dot(xb, wk_ref[...])
    k_ref[...] = k.astype(BF16)
    for j in range(TOKEN_TILE // MOBA_BLOCK):
        blk = k[j * MOBA_BLOCK:(j + 1) * MOBA_BLOCK]
        kmean_ref[0, j:j + 1, :] = jnp.sum(blk, axis=0, keepdims=True) * (1.0 / MOBA_BLOCK)
    qvt = lax.dot_general(wqvt_scr[...], xb, (((1,), (1,)), ((), ())),
                          preferred_element_type=F32)
    qvt_ref[0:a, :] = (qvt[0:a] * q_scale).astype(BF16)
    qvt_ref[a:, :] = qvt[a:].astype(BF16)


def _qkv_proj(x, w_in, a, layer):
    t, d = x.shape
    q_scale = (HEAD_DIM ** -0.5) * math.log2(math.e)
    assert t % TOKEN_TILE == 0 and TOKEN_TILE % MOBA_BLOCK == 0 and d % COL_CHUNK == 0
    n_tiles, blocks_per_tile = t // TOKEN_TILE, TOKEN_TILE // MOBA_BLOCK
    return pl.pallas_call(
        functools.partial(_qkv_kernel, q_scale=q_scale),
        grid=(n_tiles,),
        in_specs=[
            pl.BlockSpec((TOKEN_TILE, d), lambda i: (i, 0)),
            _resident((None, d, a), (layer, 0, 0)),
            _resident((None, d, a), (layer, 0, 1)),
            _resident((None, d, a), (layer, 0, 2)),
        ],
        out_specs=[
            pl.BlockSpec((TOKEN_TILE, a), lambda i: (i, 0)),
            pl.BlockSpec((2 * a, TOKEN_TILE), lambda i: (0, i)),
            pl.BlockSpec((1, blocks_per_tile, a), lambda i: (i, 0, 0)),
        ],
        out_shape=[
            jax.ShapeDtypeStruct((t, a), BF16),
            jax.ShapeDtypeStruct((2 * a, t), BF16),
            jax.ShapeDtypeStruct((n_tiles, blocks_per_tile, a), F32),
        ],
        scratch_shapes=[pltpu.VMEM((2 * a, d), BF16)],
        compiler_params=pltpu.CompilerParams(
            dimension_semantics=("arbitrary",), vmem_limit_bytes=VMEM_LIMIT),
        name="qkv_proj",
    )(x, w_in, w_in, w_in)


def _moba_kernel(qt_ref, k_ref, vt_ref, km_ref, o_ref, s_scr, vt1_scr, *, n_blocks, hd):
    blk = MOBA_BLOCK
    n_heads = qt_ref.shape[0] // hd
    blk_id = lax.broadcasted_iota(jnp.int32, (n_blocks, blk), 0)
    key_pos = lax.broadcasted_iota(jnp.int32, (blk, blk), 0)
    qry_pos = lax.broadcasted_iota(jnp.int32, (blk, blk), 1)
    causal = key_pos <= qry_pos

    def q_tile(h, qb):
        return qt_ref[h * hd:(h + 1) * hd, qb * blk:(qb + 1) * blk]

    def choice_bias(h, qb):
        km = km_ref[:, h * hd:(h + 1) * hd]
        km_hi = km.astype(BF16).astype(F32)
        km_split = jnp.concatenate([km_hi, km - km_hi], axis=0).astype(BF16)
        g2 = _dot(km_split, q_tile(h, qb))
        gate = g2[:n_blocks] + g2[n_blocks:]
        past = blk_id < qb
        gate = jnp.where(past, gate, -jnp.inf)
        beaten = jnp.zeros(gate.shape, jnp.int32)
        for m in range(qb):
            gm = gate[m:m + 1, :]
            wins = (gm > gate) | ((gm == gate) & (blk_id > m))
            beaten = beaten + wins.astype(jnp.int32)
        return jnp.where(past & (beaten < MOBA_TOPK), 0.0, -jnp.inf)

    biases = {(h, qb): choice_bias(h, qb)
              for h in range(n_heads) for qb in range(MOBA_TOPK + 1, n_blocks)}

    ones_rows = vt1_scr.shape[1] - hd
    for h in range(n_heads):
        vt1_scr[h, 0:hd, :] = vt_ref[h * hd:(h + 1) * hd, :]
        vt1_scr[h, hd:, :] = jnp.ones((ones_rows, vt1_scr.shape[2]), BF16)

    def slots(group):
        base, out = 0, []
        for qb in group:
            out.append(base)
            base += qb + 1
        assert base <= s_scr.shape[2]
        return out

    def items(group):
        per_head = [(qb, base + n, n) for qb, base in zip(group, slots(group))
                    for n in range(qb + 1)]
        return [(h,) + it for it in per_head for h in range(n_heads)]

    col_max8 = {}
    acc = {}

    def score_block(half, h, qb, slot, n):
        s = _dot(k_ref[n * blk:(n + 1) * blk, h * hd:(h + 1) * hd], q_tile(h, qb))
        if n == qb:
            s = jnp.where(causal, s, -jnp.inf)
        elif (h, qb) in biases:
            s = s + biases[h, qb][n:n + 1, :]
        s_scr[h, half, slot] = s
        mx = jnp.max(s.reshape(blk // 8, 8, blk), axis=0)
        col_max8[h, qb] = mx if n == 0 else jnp.maximum(col_max8[h, qb], mx)

    def weights_block(half, h, qb, slot, n):
        if n == 0:
            m_run = jnp.max(col_max8[h, qb], axis=0, keepdims=True)
            col_max8[h, qb] = jnp.broadcast_to(m_run, (8, blk))
        p = jnp.exp2(s_scr[h, half, slot].reshape(blk // 8, 8, blk) - col_max8[h, qb])
        return p.reshape(blk, blk).astype(BF16)

    def pv_block(p, h, qb, n):
        pv = _dot(vt1_scr[h, :, n * blk:(n + 1) * blk], p)
        acc[h, qb] = pv if n == 0 else acc[h, qb] + pv
        if n == qb:
            a = acc.pop((h, qb))
            out = (a[:hd] * (1.0 / a[hd:hd + 1])).T
            o_ref[qb * blk:(qb + 1) * blk, h * hd:(h + 1) * hd] = out.astype(o_ref.dtype)

    first = MOBA_TOPK
    rest = [qb for qb in range(n_blocks - 1, -1, -1) if qb != first]
    groups = [[first]]
    while rest:
        big = rest.pop(0)
        small = [qb for qb in rest if big + qb + 2 <= n_blocks][:1]
        for qb in small:
            rest.remove(qb)
        groups.append([big] + small)
    for item in items(groups[0]):
        score_block(0, *item)
    ready = []
    for j, group in enumerate(groups):
        cur = items(group)
        nxt = items(groups[j + 1]) if j + 1 < len(groups) else []
        for t in range(max(len(cur), len(nxt))):
            if t < len(nxt):
                score_block((j + 1) % 2, *nxt[t])
            if len(ready) > PV_LAG or (ready and t >= len(cur)):
                pv_block(*ready.pop(0))
            if t < len(cur):
                h, qb, slot, n = cur[t]
                ready.append((weights_block(j % 2, h, qb, slot, n), h, qb, n))
    while ready:
        pv_block(*ready.pop(0))


def _moba_attention(k, qvt, kmean, batch, seq):
    t, a = k.shape
    hd = HEAD_DIM
    n_heads = a // hd
    n_blocks = seq // MOBA_BLOCK
    assert seq % MOBA_BLOCK == 0 and kmean.shape == (batch, n_blocks, a)
    assert n_heads % ATTN_HEADS == 0
    hw = ATTN_HEADS * hd
    steps = n_heads // ATTN_HEADS
    return pl.pallas_call(
        functools.partial(_moba_kernel, n_blocks=n_blocks, hd=hd),
        grid=(batch, steps),
        in_specs=[
            pl.BlockSpec((hw, seq), lambda b, h: (h, b)),
            pl.BlockSpec((seq, hw), lambda b, h: (b, h)),
            pl.BlockSpec((hw, seq), lambda b, h: (steps + h, b)),
            pl.BlockSpec((None, n_blocks, hw), lambda b, h: (b, 0, h)),
        ],
        out_specs=pl.BlockSpec((seq, hw), lambda b, h: (b, h)),
        out_shape=jax.ShapeDtypeStruct((t, a), BF16),
        scratch_shapes=[
            pltpu.VMEM((ATTN_HEADS, 2, n_blocks, MOBA_BLOCK, MOBA_BLOCK), F32),
            pltpu.VMEM((ATTN_HEADS, hd + BF16_SUBLANES, seq), BF16),
        ],
        compiler_params=pltpu.CompilerParams(
            dimension_semantics=("parallel", "parallel"), vmem_limit_bytes=VMEM_LIMIT),
        name="moba_attn",
    )(qvt, k, qvt, kmean)


def _tail_kernel(x_ref, a_ref, wc_ref, wg_ref, bg_ref, cw_ref, cb_ref, wpa_ref, wpc_ref,
                 wo_ref, g_ref, b_ref, o_ref, u_scr, c_scr, mix_scr, *, alpha, tiles_per_seq):
    tm, d = x_ref.shape
    i = pl.program_id(0)

    @pl.when(i % tiles_per_seq == 0)
    def _():
        u_scr[0:CONV_HALO, :] = jnp.zeros((CONV_HALO, d), F32)

    @pl.when(i % tiles_per_seq != 0)
    def _():
        u_scr[0:CONV_HALO, :] = u_scr[tm:tm + CONV_HALO, :]

    for r0 in range(0, tm, SUB_TILE):
        rows = slice(r0, r0 + SUB_TILE)
        x = x_ref[rows, :]
        xb = x.astype(BF16)
        for c0 in range(0, d, COL_CHUNK):
            cs = slice(c0, c0 + COL_CHUNK)
            h = _dot(xb, wc_ref[:, c0:c0 + COL_CHUNK])
            g_b = _dot(xb, wc_ref[:, d + c0:d + c0 + COL_CHUNK])
            g_c = _dot(xb, wc_ref[:, 2 * d + c0:2 * d + c0 + COL_CHUNK])
            u = g_c * h
            top = CONV_HALO + r0
            u_scr[top:top + SUB_TILE, cs] = u
            u1 = u_scr[top - 1:top - 1 + SUB_TILE, cs]
            u2 = u_scr[top - 2:top - 2 + SUB_TILE, cs]
            y = (cw_ref[0:1, cs] * u2 + cw_ref[1:2, cs] * u1 + cw_ref[2:3, cs] * u
                 + cb_ref[:, cs])
            c_scr[rows, cs] = (g_b * y).astype(BF16)

        att = a_ref[rows, :]
        cc = c_scr[rows, :]
        for c0 in range(0, d, COL_CHUNK):
            cs = slice(c0, c0 + COL_CHUNK)
            y_attn = _dot(att, wpa_ref[:, cs])
            y_conv = _dot(cc, wpc_ref[:, cs])
            gate_a = jax.nn.sigmoid(_dot(xb, wg_ref[:, c0:c0 + COL_CHUNK]) + bg_ref[:, cs])
            gate_c = jax.nn.sigmoid(_dot(xb, wg_ref[:, d + c0:d + c0 + COL_CHUNK])
                                    + bg_ref[:, d + c0:d + c0 + COL_CHUNK])
            mix_scr[rows, cs] = (gate_a * y_attn + gate_c * y_conv).astype(BF16)

        mixed = _dot(mix_scr[rows, :], wo_ref[...])
        o_ref[rows, :] = _layer_norm(alpha * x + mixed, g_ref[...], b_ref[...])


def _mixer_tail(x, attn, w_in, b_gate, conv_w, conv_b, w_pa, w_pc, w_out, g, b, layer, alpha, seq):
    t, d = x.shape
    a = attn.shape[1]
    assert t % TAIL_TILE == 0 and seq % TAIL_TILE == 0 and d % COL_CHUNK == 0
    assert TAIL_TILE % SUB_TILE == 0
    assert w_pa.shape[1:] == (a, d) and a == d, "column-block indexing of w_in assumes A == D"
    assert w_pc.shape[1:] == (d, d) and conv_w.shape[1:] == (CONV_K, d)
    assert w_in.shape[2] == 3 * a + 5 * d
    tile = lambda i: (i, 0)
    lyr = (layer, 0, 0)
    return pl.pallas_call(
        functools.partial(_tail_kernel, alpha=alpha, tiles_per_seq=seq // TAIL_TILE),
        grid=(t // TAIL_TILE,),
        in_specs=[
            pl.BlockSpec((TAIL_TILE, d), tile),
            pl.BlockSpec((TAIL_TILE, a), tile),
            _resident((None, d, 3 * d), (layer, 0, 1)),
            _resident((None, d, 2 * d), (layer, 0, 3)),
            _resident((None, 1, 2 * d), lyr),
            _resident((None, CONV_K, d), lyr),
            _resident((None, 1, d), lyr),
            _resident((None, a, d), lyr),
            _resident((None, d, d), lyr),
            _resident((None, d, d), lyr),
            _resident((None, 1, d), lyr),
            _resident((None, 1, d), lyr),
        ],
        out_specs=pl.BlockSpec((TAIL_TILE, d), tile),
        out_shape=jax.ShapeDtypeStruct((t, d), F32),
        scratch_shapes=[
            pltpu.VMEM((TAIL_TILE + CONV_HALO, d), F32),
            pltpu.VMEM((TAIL_TILE, d), BF16),
            pltpu.VMEM((TAIL_TILE, d), BF16),
        ],
        compiler_params=pltpu.CompilerParams(
            dimension_semantics=("arbitrary",), vmem_limit_bytes=VMEM_LIMIT),
        name="mixer_tail",
    )(x, attn, w_in, w_in, b_gate, conv_w, conv_b, w_pa, w_pc, w_out, g, b)


def kernel(x, ffn1_w_up, ffn1_w_down, ln1_g, ln1_b, w_in, b_gate, conv_w, conv_b,
           w_proj_attn, w_proj_conv, w_out, ln2_g, ln2_b, ffn2_w_up, ffn2_w_down,
           ln3_g, ln3_b):
    batch, seq, d = x.shape
    depth = w_in.shape[0]
    a = N_HEADS * HEAD_DIM
    assert w_in.shape[2] == 3 * a + 3 * d + 2 * d and w_proj_attn.shape[1] == a
    alpha = (2.0 * depth) ** 0.25
    rows = lambda v: v.reshape(depth, 1, -1)

    h = x.reshape(batch * seq, d)
    for l in range(depth):
        h = _ffn_ln(h, ffn1_w_up, ffn1_w_down, rows(ln1_g), rows(ln1_b), l, alpha)
        k, qvt, kmean = _qkv_proj(h, w_in, a, l)
        attn = _moba_attention(k, qvt, kmean.reshape(batch, seq // MOBA_BLOCK, a), batch, seq)
        h = _mixer_tail(h, attn, w_in, rows(b_gate), conv_w, rows(conv_b), w_proj_attn,
                        w_proj_conv, w_out, rows(ln2_g), rows(ln2_b), l, alpha, seq)
        h = _ffn_ln(h, ffn2_w_up, ffn2_w_down, rows(ln3_g), rows(ln3_b), l, alpha)
    return h.reshape(batch, seq, d)
```

```python
import functools
import math

import jax
import jax.numpy as jnp
from jax import lax
from jax.experimental import pallas as pl
from jax.experimental.pallas import tpu as pltpu

N_HEADS = 8
HEAD_DIM = 128
MOBA_BLOCK = 256
MOBA_TOPK = 3
CONV_K = 3
LN_EPS = 1e-5

F32 = jnp.float32
BF16 = jnp.bfloat16

TOKEN_TILE = 256
FFN_TILE = 512
TAIL_TILE = 512
SUB_TILE = 256
NORM_ROWS = 32
FF_CHUNK = 256
COL_CHUNK = 256
CONV_HALO = 8
BF16_SUBLANES = 16
PV_LAG = 4
ATTN_HEADS = 2
VMEM_LIMIT = 56 * 1024 * 1024


def _resident(block, index):
    return pl.BlockSpec(block, lambda *_: index, pipeline_mode=pl.Buffered(1))


def _layer_norm(y, g, b):
    mu = jnp.mean(y, axis=-1, keepdims=True)
    d = y - mu
    var = jnp.mean(d * d, axis=-1, keepdims=True)
    return d * lax.rsqrt(var + LN_EPS) * g + b


def _dot(a, b):
    return jnp.dot(a, b, preferred_element_type=F32)


def _lagged_grid(n_tiles, tile, d):
    in_spec = pl.BlockSpec((tile, d), lambda i: (jnp.minimum(i, n_tiles - 1), 0))
    out_spec = pl.BlockSpec((tile, d), lambda i: (jnp.maximum(i - 1, 0), 0))
    return (n_tiles + 1,), in_spec, out_spec


def _run_lagged(n_tiles, y_scr, g_ref, b_ref, o_ref, sub_tile_body):
    i = pl.program_id(0)
    sub_tiles = range(0, o_ref.shape[0], SUB_TILE)

    def norm_pieces(r0):
        pieces = list(range(r0, r0 + SUB_TILE, NORM_ROWS))

        def fill():
            if pieces:
                rows = slice(pieces[0], pieces.pop(0) + NORM_ROWS)
                o_ref[rows, :] = _layer_norm(y_scr[(i + 1) % 2, rows, :], g_ref[...], b_ref[...])

        return pieces, fill

    @pl.when(i == 0)
    def _():
        y_scr[1] = jnp.zeros(y_scr.shape[1:], F32)

    @pl.when(i < n_tiles)
    def _():
        for r0 in sub_tiles:
            pieces, fill = norm_pieces(r0)
            sub_tile_body(i, r0, fill)
            while pieces:
                fill()

    @pl.when(i == n_tiles)
    def _():
        for r0 in sub_tiles:
            pieces, fill = norm_pieces(r0)
            while pieces:
                fill()


def _ffn_ln_kernel(x_ref, wup_ref, wdn_ref, g_ref, b_ref, o_ref, y_scr, h_scr, *, alpha, d_ff,
                   n_tiles):
    chunks = list(range(0, d_ff, FF_CHUNK))

    def sub_tile(i, r0, fill):
        x = x_ref[r0:r0 + SUB_TILE, :]
        xb = x.astype(BF16)

        for c0 in chunks:
            gate = _dot(xb, wup_ref[:, c0:c0 + FF_CHUNK])
            up = _dot(xb, wup_ref[:, d_ff + c0:d_ff + c0 + FF_CHUNK])
            h_scr[r0:r0 + SUB_TILE, c0:c0 + FF_CHUNK] = (
                gate * jax.nn.sigmoid(gate) * up).astype(BF16)
            fill()
        acc = _dot(h_scr[r0:r0 + SUB_TILE, :], wdn_ref[...])
        y_scr[i % 2, r0:r0 + SUB_TILE, :] = alpha * x + 0.5 * acc

    _run_lagged(n_tiles, y_scr, g_ref, b_ref, o_ref, sub_tile)


def _ffn_ln(x, w_up, w_down, g, b, layer, alpha):
    t, d = x.shape
    d_ff = w_down.shape[1]
    assert t % FFN_TILE == 0 and FFN_TILE % SUB_TILE == 0 and d_ff % FF_CHUNK == 0
    n_tiles = t // FFN_TILE
    grid, x_spec, out_spec = _lagged_grid(n_tiles, FFN_TILE, d)
    return pl.pallas_call(
        functools.partial(_ffn_ln_kernel, alpha=alpha, d_ff=d_ff, n_tiles=n_tiles),
        grid=grid,
        in_specs=[
            x_spec,
            _resident((None, d, 2 * d_ff), (layer, 0, 0)),
            _resident((None, d_ff, d), (layer, 0, 0)),
            _resident((None, 1, d), (layer, 0, 0)),
            _resident((None, 1, d), (layer, 0, 0)),
        ],
        out_specs=out_spec,
        out_shape=jax.ShapeDtypeStruct((t, d), F32),
        scratch_shapes=[pltpu.VMEM((2, FFN_TILE, d), F32), pltpu.VMEM((FFN_TILE, d_ff), BF16)],
        compiler_params=pltpu.CompilerParams(
            dimension_semantics=("arbitrary",), vmem_limit_bytes=VMEM_LIMIT),
        name="ffn_ln",
    )(x, w_up, w_down, g, b)


def _qkv_kernel(x_ref, wq_ref, wk_ref, wv_ref, k_ref, qvt_ref, kmean_ref, wqvt_scr, *, q_scale):
    a = k_ref.shape[1]

    @pl.when(pl.program_id(0) == 0)
    def _():
        for c0 in range(0, wq_ref.shape[0], COL_CHUNK):
            wqvt_scr[0:a, c0:c0 + COL_CHUNK] = wq_ref[c0:c0 + COL_CHUNK, :].T.astype(BF16)
            wqvt_scr[a:, c0:c0 + COL_CHUNK] = wv_ref[c0:c0 + COL_CHUNK, :].T.astype(BF16)

    xb = x_ref[...].astype(BF16)
    k = _dot(xb, wk_ref[...])
    k_ref[...] = k.astype(BF16)
    for j in range(TOKEN_TILE // MOBA_BLOCK):
        blk = k[j * MOBA_BLOCK:(j + 1) * MOBA_BLOCK]
        kmean_ref[0, j:j + 1, :] = jnp.sum(blk, axis=0, keepdims=True) * (1.0 / MOBA_BLOCK)
    qvt = lax.dot_general(wqvt_scr[...], xb, (((1,), (1,)), ((), ())),
                          preferred_element_type=F32)
    qvt_ref[0:a, :] = (qvt[0:a] * q_scale).astype(BF16)
    qvt_ref[a:, :] = qvt[a:].astype(BF16)


def _qkv_proj(x, w_in, a, layer):
    t, d = x.shape
    q_scale = (HEAD_DIM ** -0.5) * math.log2(math.e)
    assert t % TOKEN_TILE == 0 and TOKEN_TILE % MOBA_BLOCK == 0 and d % COL_CHUNK == 0
    n_tiles, blocks_per_tile = t // TOKEN_TILE, TOKEN_TILE // MOBA_BLOCK
    return pl.pallas_call(
        functools.partial(_qkv_kernel, q_scale=q_scale),
        grid=(n_tiles,),
        in_specs=[
            pl.BlockSpec((TOKEN_TILE, d), lambda i: (i, 0)),
            _resident((None, d, a), (layer, 0, 0)),
            _resident((None, d, a), (layer, 0, 1)),
            _resident((None, d, a), (layer, 0, 2)),
        ],
        out_specs=[
            pl.BlockSpec((TOKEN_TILE, a), lambda i: (i, 0)),
            pl.BlockSpec((2 * a, TOKEN_TILE), lambda i: (0, i)),
            pl.BlockSpec((1, blocks_per_tile, a), lambda i: (i, 0, 0)),
        ],
        out_shape=[
            jax.ShapeDtypeStruct((t, a), BF16),
            jax.ShapeDtypeStruct((2 * a, t), BF16),
            jax.ShapeDtypeStruct((n_tiles, blocks_per_tile, a), F32),
        ],
        scratch_shapes=[pltpu.VMEM((2 * a, d), BF16)],
        compiler_params=pltpu.CompilerParams(
            dimension_semantics=("arbitrary",), vmem_limit_bytes=VMEM_LIMIT),
        name="qkv_proj",
    )(x, w_in, w_in, w_in)


def _moba_kernel(qt_ref, k_ref, vt_ref, km_ref, o_ref, s_scr, vt1_scr, *, n_blocks, hd):
    blk = MOBA_BLOCK
    n_heads = qt_ref.shape[0] // hd
    blk_id = lax.broadcasted_iota(jnp.int32, (n_blocks, blk), 0)
    key_pos = lax.broadcasted_iota(jnp.int32, (blk, blk), 0)
    qry_pos = lax.broadcasted_iota(jnp.int32, (blk, blk), 1)
    causal = key_pos <= qry_pos

    def q_tile(h, qb):
        return qt_ref[h * hd:(h + 1) * hd, qb * blk:(qb + 1) * blk]

    def choice_bias(h, qb):
        km = km_ref[:, h * hd:(h + 1) * hd]
        km_hi = km.astype(BF16).astype(F32)
        km_split = jnp.concatenate([km_hi, km - km_hi], axis=0).astype(BF16)
        g2 = _dot(km_split, q_tile(h, qb))
        gate = g2[:n_blocks] + g2[n_blocks:]
        past = blk_id < qb
        gate = jnp.where(past, gate, -jnp.inf)
        beaten = jnp.zeros(gate.shape, jnp.int32)
        for m in range(qb):
            gm = gate[m:m + 1, :]
            wins = (gm > gate) | ((gm == gate) & (blk_id > m))
            beaten = beaten + wins.astype(jnp.int32)
        return jnp.where(past & (beaten < MOBA_TOPK), 0.0, -jnp.inf)

    biases = {(h, qb): choice_bias(h, qb)
              for h in range(n_heads) for qb in range(MOBA_TOPK + 1, n_blocks)}

    ones_rows = vt1_scr.shape[1] - hd
    for h in range(n_heads):
        vt1_scr[h, 0:hd, :] = vt_ref[h * hd:(h + 1) * hd, :]
        vt1_scr[h, hd:, :] = jnp.ones((ones_rows, vt1_scr.shape[2]), BF16)

    def slots(group):
        base, out = 0, []
        for qb in group:
            out.append(base)
            base += qb + 1
        assert base <= s_scr.shape[2]
        return out

    def items(group):
        per_head = [(qb, base + n, n) for qb, base in zip(group, slots(group))
                    for n in range(qb + 1)]
        return [(h,) + it for it in per_head for h in range(n_heads)]

    col_max8 = {}
    acc = {}

    def score_block(half, h, qb, slot, n):
        s = _dot(k_ref[n * blk:(n + 1) * blk, h * hd:(h + 1) * hd], q_tile(h, qb))
        if n == qb:
            s = jnp.where(causal, s, -jnp.inf)
        elif (h, qb) in biases:
            s = s + biases[h, qb][n:n + 1, :]
        s_scr[h, half, slot] = s
        mx = jnp.max(s.reshape(blk // 8, 8, blk), axis=0)
        col_max8[h, qb] = mx if n == 0 else jnp.maximum(col_max8[h, qb], mx)

    def weights_block(half, h, qb, slot, n):
        if n == 0:
            m_run = jnp.max(col_max8[h, qb], axis=0, keepdims=True)
            col_max8[h, qb] = jnp.broadcast_to(m_run, (8, blk))
        p = jnp.exp2(s_scr[h, half, slot].reshape(blk // 8, 8, blk) - col_max8[h, qb])
        return p.reshape(blk, blk).astype(BF16)

    def pv_block(p, h, qb, n):
        pv = _dot(vt1_scr[h, :, n * blk:(n + 1) * blk], p)
        acc[h, qb] = pv if n == 0 else acc[h, qb] + pv
        if n == qb:
            a = acc.pop((h, qb))
            out = (a[:hd] * (1.0 / a[hd:hd + 1])).T
            o_ref[qb * blk:(qb + 1) * blk, h * hd:(h + 1) * hd] = out.astype(o_ref.dtype)

    first = MOBA_TOPK
    rest = [qb for qb in range(n_blocks - 1, -1, -1) if qb != first]
    groups = [[first]]
    while rest:
        big = rest.pop(0)
        small = [qb for qb in rest if big + qb + 2 <= n_blocks][:1]
        for qb in small:
            rest.remove(qb)
        groups.append([big] + small)
    for item in items(groups[0]):
        score_block(0, *item)
    ready = []
    for j, group in enumerate(groups):
        cur = items(group)
        nxt = items(groups[j + 1]) if j + 1 < len(groups) else []
        for t in range(max(len(cur), len(nxt))):
            if t < len(nxt):
                score_block((j + 1) % 2, *nxt[t])
            if len(ready) > PV_LAG or (ready and t >= len(cur)):
                pv_block(*ready.pop(0))
            if t < len(cur):
                h, qb, slot, n = cur[t]
                ready.append((weights_block(j % 2, h, qb, slot, n), h, qb, n))
    while ready:
        pv_block(*ready.pop(0))


def _moba_attention(k, qvt, kmean, batch, seq):
    t, a = k.shape
    hd = HEAD_DIM
    n_heads = a // hd
    n_blocks = seq // MOBA_BLOCK
    assert seq % MOBA_BLOCK == 0 and kmean.shape == (batch, n_blocks, a)
    assert n_heads % ATTN_HEADS == 0
    hw = ATTN_HEADS * hd
    steps = n_heads // ATTN_HEADS
    return pl.pallas_call(
        functools.partial(_moba_kernel, n_blocks=n_blocks, hd=hd),
        grid=(batch, steps),
        in_specs=[
            pl.BlockSpec((hw, seq), lambda b, h: (h, b)),
            pl.BlockSpec((seq, hw), lambda b, h: (b, h)),
            pl.BlockSpec((hw, seq), lambda b, h: (steps + h, b)),
            pl.BlockSpec((None, n_blocks, hw), lambda b, h: (b, 0, h)),
        ],
        out_specs=pl.BlockSpec((seq, hw), lambda b, h: (b, h)),
        out_shape=jax.ShapeDtypeStruct((t, a), BF16),
        scratch_shapes=[
            pltpu.VMEM((ATTN_HEADS, 2, n_blocks, MOBA_BLOCK, MOBA_BLOCK), F32),
            pltpu.VMEM((ATTN_HEADS, hd + BF16_SUBLANES, seq), BF16),
        ],
        compiler_params=pltpu.CompilerParams(
            dimension_semantics=("parallel", "parallel"), vmem_limit_bytes=VMEM_LIMIT),
        name="moba_attn",
    )(qvt, k, qvt, kmean)


def _tail_kernel(x_ref, a_ref, wc_ref, wg_ref, bg_ref, cw_ref, cb_ref, wpa_ref, wpc_ref,
                 wo_ref, g_ref, b_ref, o_ref, u_scr, c_scr, mix_scr, *, alpha, tiles_per_seq):
    tm, d = x_ref.shape
    i = pl.program_id(0)

    @pl.when(i % tiles_per_seq == 0)
    def _():
        u_scr[0:CONV_HALO, :] = jnp.zeros((CONV_HALO, d), F32)

    @pl.when(i % tiles_per_seq != 0)
    def _():
        u_scr[0:CONV_HALO, :] = u_scr[tm:tm + CONV_HALO, :]

    for r0 in range(0, tm, SUB_TILE):
        rows = slice(r0, r0 + SUB_TILE)
        x = x_ref[rows, :]
        xb = x.astype(BF16)
        for c0 in range(0, d, COL_CHUNK):
            cs = slice(c0, c0 + COL_CHUNK)
            h = _dot(xb, wc_ref[:, c0:c0 + COL_CHUNK])
            g_b = _dot(xb, wc_ref[:, d + c0:d + c0 + COL_CHUNK])
            g_c = _dot(xb, wc_ref[:, 2 * d + c0:2 * d + c0 + COL_CHUNK])
            u = g_c * h
            top = CONV_HALO + r0
            u_scr[top:top + SUB_TILE, cs] = u
            u1 = u_scr[top - 1:top - 1 + SUB_TILE, cs]
            u2 = u_scr[top - 2:top - 2 + SUB_TILE, cs]
            y = (cw_ref[0:1, cs] * u2 + cw_ref[1:2, cs] * u1 + cw_ref[2:3, cs] * u
                 + cb_ref[:, cs])
            c_scr[rows, cs] = (g_b * y).astype(BF16)

        att = a_ref[rows, :]
        cc = c_scr[rows, :]
        for c0 in range(0, d, COL_CHUNK):
            cs = slice(c0, c0 + COL_CHUNK)
            y_attn = _dot(att, wpa_ref[:, cs])
            y_conv = _dot(cc, wpc_ref[:, cs])
            gate_a = jax.nn.sigmoid(_dot(xb, wg_ref[:, c0:c0 + COL_CHUNK]) + bg_ref[:, cs])
            gate_c = jax.nn.sigmoid(_dot(xb, wg_ref[:, d + c0:d + c0 + COL_CHUNK])
                                    + bg_ref[:, d + c0:d + c0 + COL_CHUNK])
            mix_scr[rows, cs] = (gate_a * y_attn + gate_c * y_conv).astype(BF16)

        mixed = _dot(mix_scr[rows, :], wo_ref[...])
        o_ref[rows, :] = _layer_norm(alpha * x + mixed, g_ref[...], b_ref[...])


def _mixer_tail(x, attn, w_in, b_gate, conv_w, conv_b, w_pa, w_pc, w_out, g, b, layer, alpha, seq):
    t, d = x.shape
    a = attn.shape[1]
    assert t % TAIL_TILE == 0 and seq % TAIL_TILE == 0 and d % COL_CHUNK == 0
    assert TAIL_TILE % SUB_TILE == 0
    assert w_pa.shape[1:] == (a, d) and a == d, "column-block indexing of w_in assumes A == D"
    assert w_pc.shape[1:] == (d, d) and conv_w.shape[1:] == (CONV_K, d)
    assert w_in.shape[2] == 3 * a + 5 * d
    tile = lambda i: (i, 0)
    lyr = (layer, 0, 0)
    return pl.pallas_call(
        functools.partial(_tail_kernel, alpha=alpha, tiles_per_seq=seq // TAIL_TILE),
        grid=(t // TAIL_TILE,),
        in_specs=[
            pl.BlockSpec((TAIL_TILE, d), tile),
            pl.BlockSpec((TAIL_TILE, a), tile),
            _resident((None, d, 3 * d), (layer, 0, 1)),
            _resident((None, d, 2 * d), (layer, 0, 3)),
            _resident((None, 1, 2 * d), lyr),
            _resident((None, CONV_K, d), lyr),
            _resident((None, 1, d), lyr),
            _resident((None, a, d), lyr),
            _resident((None, d, d), lyr),
            _resident((None, d, d), lyr),
            _resident((None, 1, d), lyr),
            _resident((None, 1, d), lyr),
        ],
        out_specs=pl.BlockSpec((TAIL_TILE, d), tile),
        out_shape=jax.ShapeDtypeStruct((t, d), F32),
        scratch_shapes=[
            pltpu.VMEM((TAIL_TILE + CONV_HALO, d), F32),
            pltpu.VMEM((TAIL_TILE, d), BF16),
            pltpu.VMEM((TAIL_TILE, d), BF16),
        ],
        compiler_params=pltpu.CompilerParams(
            dimension_semantics=("arbitrary",), vmem_limit_bytes=VMEM_LIMIT),
        name="mixer_tail",
    )(x, attn, w_in, w_in, b_gate, conv_w, conv_b, w_pa, w_pc, w_out, g, b)


def kernel(x, ffn1_w_up, ffn1_w_down, ln1_g, ln1_b, w_in, b_gate, conv_w, conv_b,
           w_proj_attn, w_proj_conv, w_out, ln2_g, ln2_b, ffn2_w_up, ffn2_w_down,
           ln3_g, ln3_b):
    batch, seq, d = x.shape
    depth = w_in.shape[0]
    a = N_HEADS * HEAD_DIM
    assert w_in.shape[2] == 3 * a + 3 * d + 2 * d and w_proj_attn.shape[1] == a
    alpha = (2.0 * depth) ** 0.25
    rows = lambda v: v.reshape(depth, 1, -1)

    h = x.reshape(batch * seq, d)
    for l in range(depth):
        h = _ffn_ln(h, ffn1_w_up, ffn1_w_down, rows(ln1_g), rows(ln1_b), l, alpha)
        k, qvt, kmean = _qkv_proj(h, w_in, a, l)
        attn = _moba_attention(k, qvt, kmean.reshape(batch, seq // MOBA_BLOCK, a), batch, seq)
        h = _mixer_tail(h, attn, w_in, rows(b_gate), conv_w, rows(conv_b), w_proj_attn,
                        w_proj_conv, w_out, rows(ln2_g), rows(ln2_b), l, alpha, seq)
        h = _ffn_ln(h, ffn2_w_up, ffn2_w_down, rows(ln3_g), rows(ln3_b), l, alpha)
    return h.reshape(batch, seq, d)
```

```python
import functools
import math

import jax
import jax.numpy as jnp
from jax import lax
from jax.experimental import pallas as pl
from jax.experimental.pallas import tpu as pltpu

N_HEADS = 8
HEAD_DIM = 128
MOBA_BLOCK = 256
MOBA_TOPK = 3
CONV_K = 3
LN_EPS = 1e-5

F32 = jnp.float32
BF16 = jnp.bfloat16

TOKEN_TILE = 512
FFN_TILE = 1024
TAIL_TILE = 512
SUB_TILE = 256
FF_CHUNK = 256
COL_CHUNK = 256
CONV_HALO = 8
BF16_SUBLANES = 16
PV_LAG = 4
ATTN_HEADS = 2
VMEM_LIMIT = 56 * 1024 * 1024


def _resident(block, index):
    return pl.BlockSpec(block, lambda *_: index, pipeline_mode=pl.Buffered(1))


def _layer_norm(y, g, b):
    mu = jnp.mean(y, axis=-1, keepdims=True)
    d = y - mu
    var = jnp.mean(d * d, axis=-1, keepdims=True)
    return d * lax.rsqrt(var + LN_EPS) * g + b


def _dot(a, b):
    return jnp.dot(a, b, preferred_element_type=F32)


def _ffn_ln_kernel(x_ref, wup_ref, wdn_ref, g_ref, b_ref, o_ref, *, alpha, d_ff):
    chunks = list(range(0, d_ff, FF_CHUNK))
    for r0 in range(0, x_ref.shape[0], SUB_TILE):
        x = x_ref[r0:r0 + SUB_TILE, :]
        xb = x.astype(BF16)

        def gate_up(c0):
            return (_dot(xb, wup_ref[:, c0:c0 + FF_CHUNK]),
                    _dot(xb, wup_ref[:, d_ff + c0:d_ff + c0 + FF_CHUNK]))

        pending = gate_up(chunks[0])
        acc = None
        for j, c0 in enumerate(chunks):
            gate, up = pending
            if j + 1 < len(chunks):
                pending = gate_up(chunks[j + 1])
            h = (gate * jax.nn.sigmoid(gate) * up).astype(BF16)
            down = _dot(h, wdn_ref[c0:c0 + FF_CHUNK, :])
            acc = down if acc is None else acc + down
        o_ref[r0:r0 + SUB_TILE, :] = _layer_norm(alpha * x + 0.5 * acc, g_ref[...], b_ref[...])


def _ffn_ln(x, w_up, w_down, g, b, layer, alpha):
    t, d = x.shape
    d_ff = w_down.shape[1]
    assert t % FFN_TILE == 0 and FFN_TILE % SUB_TILE == 0 and d_ff % FF_CHUNK == 0
    return pl.pallas_call(
        functools.partial(_ffn_ln_kernel, alpha=alpha, d_ff=d_ff),
        grid=(t // FFN_TILE,),
        in_specs=[
            pl.BlockSpec((FFN_TILE, d), lambda i: (i, 0)),
            _resident((None, d, 2 * d_ff), (layer, 0, 0)),
            _resident((None, d_ff, d), (layer, 0, 0)),
            _resident((None, 1, d), (layer, 0, 0)),
            _resident((None, 1, d), (layer, 0, 0)),
        ],
        out_specs=pl.BlockSpec((FFN_TILE, d), lambda i: (i, 0)),
        out_shape=jax.ShapeDtypeStruct((t, d), F32),
        compiler_params=pltpu.CompilerParams(
            dimension_semantics=("parallel",), vmem_limit_bytes=VMEM_LIMIT),
        name="ffn_ln",
    )(x, w_up, w_down, g, b)


def _qkv_kernel(x_ref, wq_ref, wk_ref, wv_ref, k_ref, qvt_ref, kmean_ref, wqvt_scr, *, q_scale):
    a = k_ref.shape[1]

    @pl.when(pl.program_id(0) == 0)
    def _():
        for c0 in range(0, wq_ref.shape[0], COL_CHUNK):
            wqvt_scr[0:a, c0:c0 + COL_CHUNK] = wq_ref[c0:c0 + COL_CHUNK, :].T.astype(BF16)
            wqvt_scr[a:, c0:c0 + COL_CHUNK] = wv_ref[c0:c0 + COL_CHUNK, :].T.astype(BF16)

    for r0 in range(0, x_ref.shape[0], SUB_TILE):
        rows = slice(r0, r0 + SUB_TILE)
        xb = x_ref[rows, :].astype(BF16)
        k = _dot(xb, wk_ref[...])
        k_ref[rows, :] = k.astype(BF16)
        for j in range(SUB_TILE // MOBA_BLOCK):
            blk = k[j * MOBA_BLOCK:(j + 1) * MOBA_BLOCK]
            row = r0 // MOBA_BLOCK + j
            kmean_ref[0, row:row + 1, :] = jnp.sum(blk, axis=0, keepdims=True) * (1.0 / MOBA_BLOCK)
        qvt = lax.dot_general(wqvt_scr[...], xb, (((1,), (1,)), ((), ())),
                              preferred_element_type=F32)
        qvt_ref[0:a, rows] = (qvt[0:a] * q_scale).astype(BF16)
        qvt_ref[a:, rows] = qvt[a:].astype(BF16)


def _qkv_proj(x, w_in, a, layer):
    t, d = x.shape
    q_scale = (HEAD_DIM ** -0.5) * math.log2(math.e)
    assert t % TOKEN_TILE == 0 and TOKEN_TILE % SUB_TILE == 0 and SUB_TILE % MOBA_BLOCK == 0
    assert d % COL_CHUNK == 0
    n_tiles, blocks_per_tile = t // TOKEN_TILE, TOKEN_TILE // MOBA_BLOCK
    return pl.pallas_call(
        functools.partial(_qkv_kernel, q_scale=q_scale),
        grid=(n_tiles,),
        in_specs=[
            pl.BlockSpec((TOKEN_TILE, d), lambda i: (i, 0)),
            _resident((None, d, a), (layer, 0, 0)),
            _resident((None, d, a), (layer, 0, 1)),
            _resident((None, d, a), (layer, 0, 2)),
        ],
        out_specs=[
            pl.BlockSpec((TOKEN_TILE, a), lambda i: (i, 0)),
            pl.BlockSpec((2 * a, TOKEN_TILE), lambda i: (0, i)),
            pl.BlockSpec((1, blocks_per_tile, a), lambda i: (i, 0, 0)),
        ],
        out_shape=[
            jax.ShapeDtypeStruct((t, a), BF16),
            jax.ShapeDtypeStruct((2 * a, t), BF16),
            jax.ShapeDtypeStruct((n_tiles, blocks_per_tile, a), F32),
        ],
        scratch_shapes=[pltpu.VMEM((2 * a, d), BF16)],
        compiler_params=pltpu.CompilerParams(
            dimension_semantics=("arbitrary",), vmem_limit_bytes=VMEM_LIMIT),
        name="qkv_proj",
    )(x, w_in, w_in, w_in)


def _moba_kernel(qt_ref, k_ref, vt_ref, km_ref, o_ref, s_scr, vt1_scr, *, n_blocks, hd):
    blk = MOBA_BLOCK
    n_heads = qt_ref.shape[0] // hd
    blk_id = lax.broadcasted_iota(jnp.int32, (n_blocks, blk), 0)
    key_pos = lax.broadcasted_iota(jnp.int32, (blk, blk), 0)
    qry_pos = lax.broadcasted_iota(jnp.int32, (blk, blk), 1)
    causal = key_pos <= qry_pos

    def q_tile(h, qb):
        return qt_ref[h * hd:(h + 1) * hd, qb * blk:(qb + 1) * blk]

    def choice_bias(h, qb):
        km = km_ref[:, h * hd:(h + 1) * hd]
        km_hi = km.astype(BF16).astype(F32)
        km_split = jnp.concatenate([km_hi, km - km_hi], axis=0).astype(BF16)
        g2 = _dot(km_split, q_tile(h, qb))
        gate = g2[:n_blocks] + g2[n_blocks:]
        past = blk_id < qb
        gate = jnp.where(past, gate, -jnp.inf)
        beaten = jnp.zeros(gate.shape, jnp.int32)
        for m in range(qb):
            gm = gate[m:m + 1, :]
            wins = (gm > gate) | ((gm == gate) & (blk_id > m))
            beaten = beaten + wins.astype(jnp.int32)
        return jnp.where(past & (beaten < MOBA_TOPK), 0.0, -jnp.inf)

    biases = {(h, qb): choice_bias(h, qb)
              for h in range(n_heads) for qb in range(MOBA_TOPK + 1, n_blocks)}

    ones_rows = vt1_scr.shape[1] - hd
    for h in range(n_heads):
        vt1_scr[h, 0:hd, :] = vt_ref[h * hd:(h + 1) * hd, :]
        vt1_scr[h, hd:, :] = jnp.ones((ones_rows, vt1_scr.shape[2]), BF16)

    def slots(group):
        base, out = 0, []
        for qb in group:
            out.append(base)
            base += qb + 1
        assert base <= s_scr.shape[2]
        return out

    def items(group):
        per_head = [(qb, base + n, n) for qb, base in zip(group, slots(group))
                    for n in range(qb + 1)]
        return [(h,) + it for it in per_head for h in range(n_heads)]

    col_max8 = {}
    acc = {}

    def score_block(half, h, qb, slot, n):
        s = _dot(k_ref[n * blk:(n + 1) * blk, h * hd:(h + 1) * hd], q_tile(h, qb))
        if n == qb:
            s = jnp.where(causal, s, -jnp.inf)
        elif (h, qb) in biases:
            s = s + biases[h, qb][n:n + 1, :]
        s_scr[h, half, slot] = s
        mx = jnp.max(s.reshape(blk // 8, 8, blk), axis=0)
        col_max8[h, qb] = mx if n == 0 else jnp.maximum(col_max8[h, qb], mx)

    def weights_block(half, h, qb, slot, n):
        if n == 0:
            m_run = jnp.max(col_max8[h, qb], axis=0, keepdims=True)
            col_max8[h, qb] = jnp.broadcast_to(m_run, (8, blk))
        p = jnp.exp2(s_scr[h, half, slot].reshape(blk // 8, 8, blk) - col_max8[h, qb])
        return p.reshape(blk, blk).astype(BF16)

    def pv_block(p, h, qb, n):
        pv = _dot(vt1_scr[h, :, n * blk:(n + 1) * blk], p)
        acc[h, qb] = pv if n == 0 else acc[h, qb] + pv
        if n == qb:
            a = acc.pop((h, qb))
            out = (a[:hd] * (1.0 / a[hd:hd + 1])).T
            o_ref[qb * blk:(qb + 1) * blk, h * hd:(h + 1) * hd] = out.astype(o_ref.dtype)

    first = MOBA_TOPK
    rest = [qb for qb in range(n_blocks - 1, -1, -1) if qb != first]
    groups = [[first]]
    while rest:
        big = rest.pop(0)
        small = [qb for qb in rest if big + qb + 2 <= n_blocks][:1]
        for qb in small:
            rest.remove(qb)
        groups.append([big] + small)
    for item in items(groups[0]):
        score_block(0, *item)
    ready = []
    for j, group in enumerate(groups):
        cur = items(group)
        nxt = items(groups[j + 1]) if j + 1 < len(groups) else []
        for t in range(max(len(cur), len(nxt))):
            if t < len(nxt):
                score_block((j + 1) % 2, *nxt[t])
            if len(ready) > PV_LAG or (ready and t >= len(cur)):
                pv_block(*ready.pop(0))
            if t < len(cur):
                h, qb, slot, n = cur[t]
                ready.append((weights_block(j % 2, h, qb, slot, n), h, qb, n))
    while ready:
        pv_block(*ready.pop(0))


def _moba_attention(k, qvt, kmean, batch, seq):
    t, a = k.shape
    hd = HEAD_DIM
    n_heads = a // hd
    n_blocks = seq // MOBA_BLOCK
    assert seq % MOBA_BLOCK == 0 and kmean.shape == (batch, n_blocks, a)
    assert n_heads % ATTN_HEADS == 0
    hw = ATTN_HEADS * hd
    steps = n_heads // ATTN_HEADS
    return pl.pallas_call(
        functools.partial(_moba_kernel, n_blocks=n_blocks, hd=hd),
        grid=(batch, steps),
        in_specs=[
            pl.BlockSpec((hw, seq), lambda b, h: (h, b)),
            pl.BlockSpec((seq, hw), lambda b, h: (b, h)),
            pl.BlockSpec((hw, seq), lambda b, h: (steps + h, b)),
            pl.BlockSpec((None, n_blocks, hw), lambda b, h: (b, 0, h)),
        ],
        out_specs=pl.BlockSpec((seq, hw), lambda b, h: (b, h)),
        out_shape=jax.ShapeDtypeStruct((t, a), BF16),
        scratch_shapes=[
            pltpu.VMEM((ATTN_HEADS, 2, n_blocks, MOBA_BLOCK, MOBA_BLOCK), F32),
            pltpu.VMEM((ATTN_HEADS, hd + BF16_SUBLANES, seq), BF16),
        ],
        compiler_params=pltpu.CompilerParams(
            dimension_semantics=("parallel", "parallel"), vmem_limit_bytes=VMEM_LIMIT),
        name="moba_attn",
    )(qvt, k, qvt, kmean)


def _tail_kernel(x_ref, a_ref, wc_ref, wg_ref, bg_ref, cw_ref, cb_ref, wpa_ref, wpc_ref,
                 wo_ref, g_ref, b_ref, o_ref, u_scr, c_scr, mix_scr, *, alpha, tiles_per_seq):
    tm, d = x_ref.shape
    i = pl.program_id(0)

    @pl.when(i % tiles_per_seq == 0)
    def _():
        u_scr[0:CONV_HALO, :] = jnp.zeros((CONV_HALO, d), F32)

    @pl.when(i % tiles_per_seq != 0)
    def _():
        u_scr[0:CONV_HALO, :] = u_scr[tm:tm + CONV_HALO, :]

    for r0 in range(0, tm, SUB_TILE):
        rows = slice(r0, r0 + SUB_TILE)
        x = x_ref[rows, :]
        xb = x.astype(BF16)
        for c0 in range(0, d, COL_CHUNK):
            cs = slice(c0, c0 + COL_CHUNK)
            h = _dot(xb, wc_ref[:, c0:c0 + COL_CHUNK])
            g_b = _dot(xb, wc_ref[:, d + c0:d + c0 + COL_CHUNK])
            g_c = _dot(xb, wc_ref[:, 2 * d + c0:2 * d + c0 + COL_CHUNK])
            u = g_c * h
            top = CONV_HALO + r0
            u_scr[top:top + SUB_TILE, cs] = u
            u1 = u_scr[top - 1:top - 1 + SUB_TILE, cs]
            u2 = u_scr[top - 2:top - 2 + SUB_TILE, cs]
            y = (cw_ref[0:1, cs] * u2 + cw_ref[1:2, cs] * u1 + cw_ref[2:3, cs] * u
                 + cb_ref[:, cs])
            c_scr[rows, cs] = (g_b * y).astype(BF16)

        att = a_ref[rows, :]
        cc = c_scr[rows, :]
        for c0 in range(0, d, COL_CHUNK):
            cs = slice(c0, c0 + COL_CHUNK)
            y_attn = _dot(att, wpa_ref[:, cs])
            y_conv = _dot(cc, wpc_ref[:, cs])
            gate_a = jax.nn.sigmoid(_dot(xb, wg_ref[:, c0:c0 + COL_CHUNK]) + bg_ref[:, cs])
            gate_c = jax.nn.sigmoid(_dot(xb, wg_ref[:, d + c0:d + c0 + COL_CHUNK])
                                    + bg_ref[:, d + c0:d + c0 + COL_CHUNK])
            mix_scr[rows, cs] = (gate_a * y_attn + gate_c * y_conv).astype(BF16)

        mixed = _dot(mix_scr[rows, :], wo_ref[...])
        o_ref[rows, :] = _layer_norm(alpha * x + mixed, g_ref[...], b_ref[...])


def _mixer_tail(x, attn, w_in, b_gate, conv_w, conv_b, w_pa, w_pc, w_out, g, b, layer, alpha, seq):
    t, d = x.shape
    a = attn.shape[1]
    assert t % TAIL_TILE == 0 and seq % TAIL_TILE == 0 and d % COL_CHUNK == 0
    assert TAIL_TILE % SUB_TILE == 0
    assert w_pa.shape[1:] == (a, d) and a == d, "column-block indexing of w_in assumes A == D"
    assert w_pc.shape[1:] == (d, d) and conv_w.shape[1:] == (CONV_K, d)
    assert w_in.shape[2] == 3 * a + 5 * d
    tile = lambda i: (i, 0)
    lyr = (layer, 0, 0)
    return pl.pallas_call(
        functools.partial(_tail_kernel, alpha=alpha, tiles_per_seq=seq // TAIL_TILE),
        grid=(t // TAIL_TILE,),
        in_specs=[
            pl.BlockSpec((TAIL_TILE, d), tile),
            pl.BlockSpec((TAIL_TILE, a), tile),
            _resident((None, d, 3 * d), (layer, 0, 1)),
            _resident((None, d, 2 * d), (layer, 0, 3)),
            _resident((None, 1, 2 * d), lyr),
            _resident((None, CONV_K, d), lyr),
            _resident((None, 1, d), lyr),
            _resident((None, a, d), lyr),
            _resident((None, d, d), lyr),
            _resident((None, d, d), lyr),
            _resident((None, 1, d), lyr),
            _resident((None, 1, d), lyr),
        ],
        out_specs=pl.BlockSpec((TAIL_TILE, d), tile),
        out_shape=jax.ShapeDtypeStruct((t, d), F32),
        scratch_shapes=[
            pltpu.VMEM((TAIL_TILE + CONV_HALO, d), F32),
            pltpu.VMEM((TAIL_TILE, d), BF16),
            pltpu.VMEM((TAIL_TILE, d), BF16),
        ],
        compiler_params=pltpu.CompilerParams(
            dimension_semantics=("arbitrary",), vmem_limit_bytes=VMEM_LIMIT),
        name="mixer_tail",
    )(x, attn, w_in, w_in, b_gate, conv_w, conv_b, w_pa, w_pc, w_out, g, b)


def kernel(x, ffn1_w_up, ffn1_w_down, ln1_g, ln1_b, w_in, b_gate, conv_w, conv_b,
           w_proj_attn, w_proj_conv, w_out, ln2_g, ln2_b, ffn2_w_up, ffn2_w_down,
           ln3_g, ln3_b):
    batch, seq, d = x.shape
    depth = w_in.shape[0]
    a = N_HEADS * HEAD_DIM
    assert w_in.shape[2] == 3 * a + 3 * d + 2 * d and w_proj_attn.shape[1] == a
    alpha = (2.0 * depth) ** 0.25
    rows = lambda v: v.reshape(depth, 1, -1)

    h = x.reshape(batch * seq, d)
    for l in range(depth):
        h = _ffn_ln(h, ffn1_w_up, ffn1_w_down, rows(ln1_g), rows(ln1_b), l, alpha)
        k, qvt, kmean = _qkv_proj(h, w_in, a, l)
        attn = _moba_attention(k, qvt, kmean.reshape(batch, seq // MOBA_BLOCK, a), batch, seq)
        h = _mixer_tail(h, attn, w_in, rows(b_gate), conv_w, rows(conv_b), w_proj_attn,
                        w_proj_conv, w_out, rows(ln2_g), rows(ln2_b), l, alpha, seq)
        h = _ffn_ln(h, ffn2_w_up, ffn2_w_down, rows(ln3_g), rows(ln3_b), l, alpha)
    return h.reshape(batch, seq, d)
```

```python
import functools
import math

import jax
import jax.numpy as jnp
from jax import lax
from jax.experimental import pallas as pl
from jax.experimental.pallas import tpu as pltpu

N_HEADS = 8
HEAD_DIM = 128
MOBA_BLOCK = 256
MOBA_TOPK = 3
CONV_K = 3
LN_EPS = 1e-5

F32 = jnp.float32
BF16 = jnp.bfloat16

TOKEN_TILE = 512
FFN_TILE = 1024
TAIL_TILE = 512
SUB_TILE = 256
FF_CHUNK = 256
COL_CHUNK = 256
CONV_HALO = 8
BF16_SUBLANES = 16
WEIGHT_CHUNKS = 16
PV_LAG = 4
ATTN_HEADS = 2
VMEM_LIMIT = 56 * 1024 * 1024


def _resident(block, index):
    return pl.BlockSpec(block, lambda *_: index, pipeline_mode=pl.Buffered(1))


def _layer_norm(y, g, b):
    mu = jnp.mean(y, axis=-1, keepdims=True)
    d = y - mu
    var = jnp.mean(d * d, axis=-1, keepdims=True)
    return d * lax.rsqrt(var + LN_EPS) * g + b


def _dot(a, b):
    return jnp.dot(a, b, preferred_element_type=F32)


def _load_as_bf16(src_hbm, dst_scr, stage, sem):
    chunk_rows = stage.shape[1]
    n_chunks = src_hbm.shape[0] // chunk_rows
    assert n_chunks * chunk_rows == src_hbm.shape[0] and chunk_rows % BF16_SUBLANES == 0

    def copy(j):
        return pltpu.make_async_copy(
            src_hbm.at[pl.ds(j * chunk_rows, chunk_rows), :], stage.at[j % 2], sem.at[j % 2])

    copy(0).start()
    for j in range(n_chunks):
        if j + 1 < n_chunks:
            copy(j + 1).start()
        copy(j).wait()
        dst_scr[j * chunk_rows:(j + 1) * chunk_rows, :] = stage[j % 2].astype(BF16)


def _ffn_ln_kernel(x_ref, wup_hbm, wdn_hbm, g_ref, b_ref, o_ref, wup_ref, wdn_ref,
                   up_stage, dn_stage, sems, *, alpha, d_ff, layer):
    @pl.when(pl.program_id(0) == 0)
    def _():
        _load_as_bf16(wup_hbm.at[layer], wup_ref, up_stage, sems.at[0])
        _load_as_bf16(wdn_hbm.at[layer], wdn_ref, dn_stage, sems.at[1])

    chunks = list(range(0, d_ff, FF_CHUNK))
    for r0 in range(0, x_ref.shape[0], SUB_TILE):
        x = x_ref[r0:r0 + SUB_TILE, :]
        xb = x.astype(BF16)

        def gate_up(c0):
            return (_dot(xb, wup_ref[:, c0:c0 + FF_CHUNK]),
                    _dot(xb, wup_ref[:, d_ff + c0:d_ff + c0 + FF_CHUNK]))

        pending = gate_up(chunks[0])
        acc = None
        for j, c0 in enumerate(chunks):
            gate, up = pending
            if j + 1 < len(chunks):
                pending = gate_up(chunks[j + 1])
            h = (gate * jax.nn.sigmoid(gate) * up).astype(BF16)
            down = _dot(h, wdn_ref[c0:c0 + FF_CHUNK, :])
            acc = down if acc is None else acc + down
        o_ref[r0:r0 + SUB_TILE, :] = _layer_norm(alpha * x + 0.5 * acc, g_ref[...], b_ref[...])


def _ffn_ln(x, w_up, w_down, g, b, layer, alpha):
    t, d = x.shape
    d_ff = w_down.shape[1]
    assert t % FFN_TILE == 0 and FFN_TILE % SUB_TILE == 0 and d_ff % FF_CHUNK == 0
    assert d % WEIGHT_CHUNKS == 0 and d_ff % WEIGHT_CHUNKS == 0
    return pl.pallas_call(
        functools.partial(_ffn_ln_kernel, alpha=alpha, d_ff=d_ff, layer=layer),
        grid=(t // FFN_TILE,),
        in_specs=[
            pl.BlockSpec((FFN_TILE, d), lambda i: (i, 0)),
            pl.BlockSpec(memory_space=pl.ANY),
            pl.BlockSpec(memory_space=pl.ANY),
            _resident((None, 1, d), (layer, 0, 0)),
            _resident((None, 1, d), (layer, 0, 0)),
        ],
        out_specs=pl.BlockSpec((FFN_TILE, d), lambda i: (i, 0)),
        out_shape=jax.ShapeDtypeStruct((t, d), F32),
        scratch_shapes=[
            pltpu.VMEM((d, 2 * d_ff), BF16),
            pltpu.VMEM((d_ff, d), BF16),
            pltpu.VMEM((2, d // WEIGHT_CHUNKS, 2 * d_ff), F32),
            pltpu.VMEM((2, d_ff // WEIGHT_CHUNKS, d), F32),
            pltpu.SemaphoreType.DMA((2, 2)),
        ],
        compiler_params=pltpu.CompilerParams(
            dimension_semantics=("arbitrary",), vmem_limit_bytes=VMEM_LIMIT),
        name="ffn_ln",
    )(x, w_up, w_down, g, b)


def _qkv_kernel(x_ref, wq_ref, wk_ref, wv_ref, k_ref, qvt_ref, kmean_ref, wqvt_scr, *, q_scale):
    a = k_ref.shape[1]

    @pl.when(pl.program_id(0) == 0)
    def _():
        for c0 in range(0, wq_ref.shape[0], COL_CHUNK):
            wqvt_scr[0:a, c0:c0 + COL_CHUNK] = wq_ref[c0:c0 + COL_CHUNK, :].T.astype(BF16)
            wqvt_scr[a:, c0:c0 + COL_CHUNK] = wv_ref[c0:c0 + COL_CHUNK, :].T.astype(BF16)

    for r0 in range(0, x_ref.shape[0], SUB_TILE):
        rows = slice(r0, r0 + SUB_TILE)
        xb = x_ref[rows, :].astype(BF16)
        k = _dot(xb, wk_ref[...])
        k_ref[rows, :] = k.astype(BF16)
        for j in range(SUB_TILE // MOBA_BLOCK):
            blk = k[j * MOBA_BLOCK:(j + 1) * MOBA_BLOCK]
            row = r0 // MOBA_BLOCK + j
            kmean_ref[0, row:row + 1, :] = jnp.sum(blk, axis=0, keepdims=True) * (1.0 / MOBA_BLOCK)
        qvt = lax.dot_general(wqvt_scr[...], xb, (((1,), (1,)), ((), ())),
                              preferred_element_type=F32)
        qvt_ref[0:a, rows] = (qvt[0:a] * q_scale).astype(BF16)
        qvt_ref[a:, rows] = qvt[a:].astype(BF16)


def _qkv_proj(x, w_in, a, layer):
    t, d = x.shape
    q_scale = (HEAD_DIM ** -0.5) * math.log2(math.e)
    assert t % TOKEN_TILE == 0 and TOKEN_TILE % SUB_TILE == 0 and SUB_TILE % MOBA_BLOCK == 0
    assert d % COL_CHUNK == 0
    n_tiles, blocks_per_tile = t // TOKEN_TILE, TOKEN_TILE // MOBA_BLOCK
    return pl.pallas_call(
        functools.partial(_qkv_kernel, q_scale=q_scale),
        grid=(n_tiles,),
        in_specs=[
            pl.BlockSpec((TOKEN_TILE, d), lambda i: (i, 0)),
            _resident((None, d, a), (layer, 0, 0)),
            _resident((None, d, a), (layer, 0, 1)),
            _resident((None, d, a), (layer, 0, 2)),
        ],
        out_specs=[
            pl.BlockSpec((TOKEN_TILE, a), lambda i: (i, 0)),
            pl.BlockSpec((2 * a, TOKEN_TILE), lambda i: (0, i)),
            pl.BlockSpec((1, blocks_per_tile, a), lambda i: (i, 0, 0)),
        ],
        out_shape=[
            jax.ShapeDtypeStruct((t, a), BF16),
            jax.ShapeDtypeStruct((2 * a, t), BF16),
            jax.ShapeDtypeStruct((n_tiles, blocks_per_tile, a), F32),
        ],
        scratch_shapes=[pltpu.VMEM((2 * a, d), BF16)],
        compiler_params=pltpu.CompilerParams(
            dimension_semantics=("arbitrary",), vmem_limit_bytes=VMEM_LIMIT),
        name="qkv_proj",
    )(x, w_in, w_in, w_in)


def _moba_kernel(qt_ref, k_ref, vt_ref, km_ref, o_ref, s_scr, vt1_scr, *, n_blocks, hd):
    blk = MOBA_BLOCK
    n_heads = qt_ref.shape[0] // hd
    blk_id = lax.broadcasted_iota(jnp.int32, (n_blocks, blk), 0)
    key_pos = lax.broadcasted_iota(jnp.int32, (blk, blk), 0)
    qry_pos = lax.broadcasted_iota(jnp.int32, (blk, blk), 1)
    causal = key_pos <= qry_pos

    def q_tile(h, qb):
        return qt_ref[h * hd:(h + 1) * hd, qb * blk:(qb + 1) * blk]

    def choice_bias(h, qb):
        km = km_ref[:, h * hd:(h + 1) * hd]
        km_hi = km.astype(BF16).astype(F32)
        km_split = jnp.concatenate([km_hi, km - km_hi], axis=0).astype(BF16)
        g2 = _dot(km_split, q_tile(h, qb))
        gate = g2[:n_blocks] + g2[n_blocks:]
        past = blk_id < qb
        gate = jnp.where(past, gate, -jnp.inf)
        beaten = jnp.zeros(gate.shape, jnp.int32)
        for m in range(qb):
            gm = gate[m:m + 1, :]
            wins = (gm > gate) | ((gm == gate) & (blk_id > m))
            beaten = beaten + wins.astype(jnp.int32)
        return jnp.where(past & (beaten < MOBA_TOPK), 0.0, -jnp.inf)

    biases = {(h, qb): choice_bias(h, qb)
              for h in range(n_heads) for qb in range(MOBA_TOPK + 1, n_blocks)}

    ones_rows = vt1_scr.shape[1] - hd
    for h in range(n_heads):
        vt1_scr[h, 0:hd, :] = vt_ref[h * hd:(h + 1) * hd, :]
        vt1_scr[h, hd:, :] = jnp.ones((ones_rows, vt1_scr.shape[2]), BF16)

    def slots(group):
        base, out = 0, []
        for qb in group:
            out.append(base)
            base += qb + 1
        assert base <= s_scr.shape[2]
        return out

    def items(group):
        per_head = [(qb, base + n, n) for qb, base in zip(group, slots(group))
                    for n in range(qb + 1)]
        return [(h,) + it for it in per_head for h in range(n_heads)]

    col_max8 = {}
    acc = {}

    def score_block(half, h, qb, slot, n):
        s = _dot(k_ref[n * blk:(n + 1) * blk, h * hd:(h + 1) * hd], q_tile(h, qb))
        if n == qb:
            s = jnp.where(causal, s, -jnp.inf)
        elif (h, qb) in biases:
            s = s + biases[h, qb][n:n + 1, :]
        s_scr[h, half, slot] = s
        mx = jnp.max(s.reshape(blk // 8, 8, blk), axis=0)
        col_max8[h, qb] = mx if n == 0 else jnp.maximum(col_max8[h, qb], mx)

    def weights_block(half, h, qb, slot, n):
        if n == 0:
            m_run = jnp.max(col_max8[h, qb], axis=0, keepdims=True)
            col_max8[h, qb] = jnp.broadcast_to(m_run, (8, blk))
        p = jnp.exp2(s_scr[h, half, slot].reshape(blk // 8, 8, blk) - col_max8[h, qb])
        return p.reshape(blk, blk).astype(BF16)

    def pv_block(p, h, qb, n):
        pv = _dot(vt1_scr[h, :, n * blk:(n + 1) * blk], p)
        acc[h, qb] = pv if n == 0 else acc[h, qb] + pv
        if n == qb:
            a = acc.pop((h, qb))
            out = (a[:hd] * (1.0 / a[hd:hd + 1])).T
            o_ref[qb * blk:(qb + 1) * blk, h * hd:(h + 1) * hd] = out.astype(o_ref.dtype)

    first = MOBA_TOPK
    rest = [qb for qb in range(n_blocks - 1, -1, -1) if qb != first]
    groups = [[first]]
    while rest:
        big = rest.pop(0)
        small = [qb for qb in rest if big + qb + 2 <= n_blocks][:1]
        for qb in small:
            rest.remove(qb)
        groups.append([big] + small)
    for item in items(groups[0]):
        score_block(0, *item)
    ready = []
    for j, group in enumerate(groups):
        cur = items(group)
        nxt = items(groups[j + 1]) if j + 1 < len(groups) else []
        for t in range(max(len(cur), len(nxt))):
            if t < len(nxt):
                score_block((j + 1) % 2, *nxt[t])
            if len(ready) > PV_LAG or (ready and t >= len(cur)):
                pv_block(*ready.pop(0))
            if t < len(cur):
                h, qb, slot, n = cur[t]
                ready.append((weights_block(j % 2, h, qb, slot, n), h, qb, n))
    while ready:
        pv_block(*ready.pop(0))


def _moba_attention(k, qvt, kmean, batch, seq):
    t, a = k.shape
    hd = HEAD_DIM
    n_heads = a // hd
    n_blocks = seq // MOBA_BLOCK
    assert seq % MOBA_BLOCK == 0 and kmean.shape == (batch, n_blocks, a)
    assert n_heads % ATTN_HEADS == 0
    hw = ATTN_HEADS * hd
    steps = n_heads // ATTN_HEADS
    return pl.pallas_call(
        functools.partial(_moba_kernel, n_blocks=n_blocks, hd=hd),
        grid=(batch, steps),
        in_specs=[
            pl.BlockSpec((hw, seq), lambda b, h: (h, b)),
            pl.BlockSpec((seq, hw), lambda b, h: (b, h)),
            pl.BlockSpec((hw, seq), lambda b, h: (steps + h, b)),
            pl.BlockSpec((None, n_blocks, hw), lambda b, h: (b, 0, h)),
        ],
        out_specs=pl.BlockSpec((seq, hw), lambda b, h: (b, h)),
        out_shape=jax.ShapeDtypeStruct((t, a), BF16),
        scratch_shapes=[
            pltpu.VMEM((ATTN_HEADS, 2, n_blocks, MOBA_BLOCK, MOBA_BLOCK), F32),
            pltpu.VMEM((ATTN_HEADS, hd + BF16_SUBLANES, seq), BF16),
        ],
        compiler_params=pltpu.CompilerParams(
            dimension_semantics=("parallel", "parallel"), vmem_limit_bytes=VMEM_LIMIT),
        name="moba_attn",
    )(qvt, k, qvt, kmean)


def _tail_kernel(x_ref, a_ref, wc_ref, wg_ref, bg_ref, cw_ref, cb_ref, wpa_ref, wpc_ref,
                 wo_ref, g_ref, b_ref, o_ref, u_scr, c_scr, mix_scr, *, alpha, tiles_per_seq):
    tm, d = x_ref.shape
    i = pl.program_id(0)

    @pl.when(i % tiles_per_seq == 0)
    def _():
        u_scr[0:CONV_HALO, :] = jnp.zeros((CONV_HALO, d), F32)

    @pl.when(i % tiles_per_seq != 0)
    def _():
        u_scr[0:CONV_HALO, :] = u_scr[tm:tm + CONV_HALO, :]

    for r0 in range(0, tm, SUB_TILE):
        rows = slice(r0, r0 + SUB_TILE)
        x = x_ref[rows, :]
        xb = x.astype(BF16)
        for c0 in range(0, d, COL_CHUNK):
            cs = slice(c0, c0 + COL_CHUNK)
            h = _dot(xb, wc_ref[:, c0:c0 + COL_CHUNK])
            g_b = _dot(xb, wc_ref[:, d + c0:d + c0 + COL_CHUNK])
            g_c = _dot(xb, wc_ref[:, 2 * d + c0:2 * d + c0 + COL_CHUNK])
            u = g_c * h
            top = CONV_HALO + r0
            u_scr[top:top + SUB_TILE, cs] = u
            u1 = u_scr[top - 1:top - 1 + SUB_TILE, cs]
            u2 = u_scr[top - 2:top - 2 + SUB_TILE, cs]
            y = (cw_ref[0:1, cs] * u2 + cw_ref[1:2, cs] * u1 + cw_ref[2:3, cs] * u
                 + cb_ref[:, cs])
            c_scr[rows, cs] = (g_b * y).astype(BF16)

        att = a_ref[rows, :]
        cc = c_scr[rows, :]
        for c0 in range(0, d, COL_CHUNK):
            cs = slice(c0, c0 + COL_CHUNK)
            y_attn = _dot(att, wpa_ref[:, cs])
            y_conv = _dot(cc, wpc_ref[:, cs])
            gate_a = jax.nn.sigmoid(_dot(xb, wg_ref[:, c0:c0 + COL_CHUNK]) + bg_ref[:, cs])
            gate_c = jax.nn.sigmoid(_dot(xb, wg_ref[:, d + c0:d + c0 + COL_CHUNK])
                                    + bg_ref[:, d + c0:d + c0 + COL_CHUNK])
            mix_scr[rows, cs] = (gate_a * y_attn + gate_c * y_conv).astype(BF16)

        mixed = _dot(mix_scr[rows, :], wo_ref[...])
        o_ref[rows, :] = _layer_norm(alpha * x + mixed, g_ref[...], b_ref[...])


def _mixer_tail(x, attn, w_in, b_gate, conv_w, conv_b, w_pa, w_pc, w_out, g, b, layer, alpha, seq):
    t, d = x.shape
    a = attn.shape[1]
    assert t % TAIL_TILE == 0 and seq % TAIL_TILE == 0 and d % COL_CHUNK == 0
    assert TAIL_TILE % SUB_TILE == 0
    assert w_pa.shape[1:] == (a, d) and a == d, "column-block indexing of w_in assumes A == D"
    assert w_pc.shape[1:] == (d, d) and conv_w.shape[1:] == (CONV_K, d)
    assert w_in.shape[2] == 3 * a + 5 * d
    tile = lambda i: (i, 0)
    lyr = (layer, 0, 0)
    return pl.pallas_call(
        functools.partial(_tail_kernel, alpha=alpha, tiles_per_seq=seq // TAIL_TILE),
        grid=(t // TAIL_TILE,),
        in_specs=[
            pl.BlockSpec((TAIL_TILE, d), tile),
            pl.BlockSpec((TAIL_TILE, a), tile),
            _resident((None, d, 3 * d), (layer, 0, 1)),
            _resident((None, d, 2 * d), (layer, 0, 3)),
            _resident((None, 1, 2 * d), lyr),
            _resident((None, CONV_K, d), lyr),
            _resident((None, 1, d), lyr),
            _resident((None, a, d), lyr),
            _resident((None, d, d), lyr),
            _resident((None, d, d), lyr),
            _resident((None, 1, d), lyr),
            _resident((None, 1, d), lyr),
        ],
        out_specs=pl.BlockSpec((TAIL_TILE, d), tile),
        out_shape=jax.ShapeDtypeStruct((t, d), F32),
        scratch_shapes=[
            pltpu.VMEM((TAIL_TILE + CONV_HALO, d), F32),
            pltpu.VMEM((TAIL_TILE, d), BF16),
            pltpu.VMEM((TAIL_TILE, d), BF16),
        ],
        compiler_params=pltpu.CompilerParams(
            dimension_semantics=("arbitrary",), vmem_limit_bytes=VMEM_LIMIT),
        name="mixer_tail",
    )(x, attn, w_in, w_in, b_gate, conv_w, conv_b, w_pa, w_pc, w_out, g, b)


def kernel(x, ffn1_w_up, ffn1_w_down, ln1_g, ln1_b, w_in, b_gate, conv_w, conv_b,
           w_proj_attn, w_proj_conv, w_out, ln2_g, ln2_b, ffn2_w_up, ffn2_w_down,
           ln3_g, ln3_b):
    batch, seq, d = x.shape
    depth = w_in.shape[0]
    a = N_HEADS * HEAD_DIM
    assert w_in.shape[2] == 3 * a + 3 * d + 2 * d and w_proj_attn.shape[1] == a
    alpha = (2.0 * depth) ** 0.25
    rows = lambda v: v.reshape(depth, 1, -1)

    h = x.reshape(batch * seq, d)
    for l in range(depth):
        h = _ffn_ln(h, ffn1_w_up, ffn1_w_down, rows(ln1_g), rows(ln1_b), l, alpha)
        k, qvt, kmean = _qkv_proj(h, w_in, a, l)
        attn = _moba_attention(k, qvt, kmean.reshape(batch, seq // MOBA_BLOCK, a), batch, seq)
        h = _mixer_tail(h, attn, w_in, rows(b_gate), conv_w, rows(conv_b), w_proj_attn,
                        w_proj_conv, w_out, rows(ln2_g), rows(ln2_b), l, alpha, seq)
        h = _ffn_ln(h, ffn2_w_up, ffn2_w_down, rows(ln3_g), rows(ln3_b), l, alpha)
    return h.reshape(batch, seq, d)
```

```python
import functools
import math

import jax
import jax.numpy as jnp
from jax import lax
from jax.experimental import pallas as pl
from jax.experimental.pallas import tpu as pltpu

N_HEADS = 8
HEAD_DIM = 128
MOBA_BLOCK = 256
MOBA_TOPK = 3
CONV_K = 3
LN_EPS = 1e-5

F32 = jnp.float32
BF16 = jnp.bfloat16

TOKEN_TILE = 512
FFN_TILE = 1024
TAIL_TILE = 512
SUB_TILE = 256
FF_CHUNK = 256
COL_CHUNK = 256
CONV_HALO = 8
BF16_SUBLANES = 16
UP_CHUNKS = 8
PV_LAG = 4
ATTN_HEADS = 2
VMEM_LIMIT = 56 * 1024 * 1024


def _resident(block, index):
    return pl.BlockSpec(block, lambda *_: index, pipeline_mode=pl.Buffered(1))


def _layer_norm(y, g, b):
    mu = jnp.mean(y, axis=-1, keepdims=True)
    d = y - mu
    var = jnp.mean(d * d, axis=-1, keepdims=True)
    return d * lax.rsqrt(var + LN_EPS) * g + b


def _dot(a, b):
    return jnp.dot(a, b, preferred_element_type=F32)


def _load_as_bf16(jobs):
    def n_chunks(job):
        src, _, stage, _ = job
        assert src.shape[0] % stage.shape[1] == 0 and stage.shape[1] % BF16_SUBLANES == 0
        return src.shape[0] // stage.shape[1]

    def copy(job, j):
        src, _, stage, sem = job
        rows = stage.shape[1]
        return pltpu.make_async_copy(
            src.at[pl.ds(j * rows, rows), :], stage.at[j % 2], sem.at[j % 2])

    for job in jobs:
        for j in range(min(2, n_chunks(job))):
            copy(job, j).start()
    for j in range(max(n_chunks(job) for job in jobs)):
        for job in jobs:
            if j < n_chunks(job):
                _, dst, stage, _ = job
                rows = stage.shape[1]
                copy(job, j).wait()
                dst[j * rows:(j + 1) * rows, :] = stage[j % 2].astype(BF16)
                if j + 2 < n_chunks(job):
                    copy(job, j + 2).start()


def _ffn_ln_kernel(x_ref, wup_hbm, wdn_hbm, g_ref, b_ref, o_ref, wup_ref, wdn_ref,
                   up_stage, dn_stage, sems, *, alpha, d_ff, layer):
    @pl.when(pl.program_id(0) == 0)
    def _():
        _load_as_bf16([(wup_hbm.at[layer], wup_ref, up_stage, sems.at[0]),
                       (wdn_hbm.at[layer], wdn_ref, dn_stage, sems.at[1])])

    chunks = list(range(0, d_ff, FF_CHUNK))
    for r0 in range(0, x_ref.shape[0], SUB_TILE):
        x = x_ref[r0:r0 + SUB_TILE, :]
        xb = x.astype(BF16)

        def gate_up(c0):
            return (_dot(xb, wup_ref[:, c0:c0 + FF_CHUNK]),
                    _dot(xb, wup_ref[:, d_ff + c0:d_ff + c0 + FF_CHUNK]))

        pending = gate_up(chunks[0])
        acc = None
        for j, c0 in enumerate(chunks):
            gate, up = pending
            if j + 1 < len(chunks):
                pending = gate_up(chunks[j + 1])
            h = (gate * jax.nn.sigmoid(gate) * up).astype(BF16)
            down = _dot(h, wdn_ref[c0:c0 + FF_CHUNK, :])
            acc = down if acc is None else acc + down
        o_ref[r0:r0 + SUB_TILE, :] = _layer_norm(alpha * x + 0.5 * acc, g_ref[...], b_ref[...])


def _ffn_ln(x, w_up, w_down, g, b, layer, alpha):
    t, d = x.shape
    d_ff = w_down.shape[1]
    assert t % FFN_TILE == 0 and FFN_TILE % SUB_TILE == 0 and d_ff % FF_CHUNK == 0
    up_rows, dn_rows = d // UP_CHUNKS, 2 * d_ff // UP_CHUNKS
    assert d % UP_CHUNKS == 0 and d_ff % dn_rows == 0
    return pl.pallas_call(
        functools.partial(_ffn_ln_kernel, alpha=alpha, d_ff=d_ff, layer=layer),
        grid=(t // FFN_TILE,),
        in_specs=[
            pl.BlockSpec((FFN_TILE, d), lambda i: (i, 0)),
            pl.BlockSpec(memory_space=pl.ANY),
            pl.BlockSpec(memory_space=pl.ANY),
            _resident((None, 1, d), (layer, 0, 0)),
            _resident((None, 1, d), (layer, 0, 0)),
        ],
        out_specs=pl.BlockSpec((FFN_TILE, d), lambda i: (i, 0)),
        out_shape=jax.ShapeDtypeStruct((t, d), F32),
        scratch_shapes=[
            pltpu.VMEM((d, 2 * d_ff), BF16),
            pltpu.VMEM((d_ff, d), BF16),
            pltpu.VMEM((2, up_rows, 2 * d_ff), F32),
            pltpu.VMEM((2, dn_rows, d), F32),
            pltpu.SemaphoreType.DMA((2, 2)),
        ],
        compiler_params=pltpu.CompilerParams(
            dimension_semantics=("arbitrary",), vmem_limit_bytes=VMEM_LIMIT),
        name="ffn_ln",
    )(x, w_up, w_down, g, b)


def _qkv_kernel(x_ref, wq_ref, wk_ref, wv_ref, k_ref, qvt_ref, kmean_ref, wqvt_scr, *, q_scale):
    a = k_ref.shape[1]

    @pl.when(pl.program_id(0) == 0)
    def _():
        for c0 in range(0, wq_ref.shape[0], COL_CHUNK):
            wqvt_scr[0:a, c0:c0 + COL_CHUNK] = wq_ref[c0:c0 + COL_CHUNK, :].T.astype(BF16)
            wqvt_scr[a:, c0:c0 + COL_CHUNK] = wv_ref[c0:c0 + COL_CHUNK, :].T.astype(BF16)

    for r0 in range(0, x_ref.shape[0], SUB_TILE):
        rows = slice(r0, r0 + SUB_TILE)
        xb = x_ref[rows, :].astype(BF16)
        k = _dot(xb, wk_ref[...])
        k_ref[rows, :] = k.astype(BF16)
        for j in range(SUB_TILE // MOBA_BLOCK):
            blk = k[j * MOBA_BLOCK:(j + 1) * MOBA_BLOCK]
            row = r0 // MOBA_BLOCK + j
            kmean_ref[0, row:row + 1, :] = jnp.sum(blk, axis=0, keepdims=True) * (1.0 / MOBA_BLOCK)
        qvt = lax.dot_general(wqvt_scr[...], xb, (((1,), (1,)), ((), ())),
                              preferred_element_type=F32)
        qvt_ref[0:a, rows] = (qvt[0:a] * q_scale).astype(BF16)
        qvt_ref[a:, rows] = qvt[a:].astype(BF16)


def _qkv_proj(x, w_in, a, layer):
    t, d = x.shape
    q_scale = (HEAD_DIM ** -0.5) * math.log2(math.e)
    assert t % TOKEN_TILE == 0 and TOKEN_TILE % SUB_TILE == 0 and SUB_TILE % MOBA_BLOCK == 0
    assert d % COL_CHUNK == 0
    n_tiles, blocks_per_tile = t // TOKEN_TILE, TOKEN_TILE // MOBA_BLOCK
    return pl.pallas_call(
        functools.partial(_qkv_kernel, q_scale=q_scale),
        grid=(n_tiles,),
        in_specs=[
            pl.BlockSpec((TOKEN_TILE, d), lambda i: (i, 0)),
            _resident((None, d, a), (layer, 0, 0)),
            _resident((None, d, a), (layer, 0, 1)),
            _resident((None, d, a), (layer, 0, 2)),
        ],
        out_specs=[
            pl.BlockSpec((TOKEN_TILE, a), lambda i: (i, 0)),
            pl.BlockSpec((2 * a, TOKEN_TILE), lambda i: (0, i)),
            pl.BlockSpec((1, blocks_per_tile, a), lambda i: (i, 0, 0)),
        ],
        out_shape=[
            jax.ShapeDtypeStruct((t, a), BF16),
            jax.ShapeDtypeStruct((2 * a, t), BF16),
            jax.ShapeDtypeStruct((n_tiles, blocks_per_tile, a), F32),
        ],
        scratch_shapes=[pltpu.VMEM((2 * a, d), BF16)],
        compiler_params=pltpu.CompilerParams(
            dimension_semantics=("arbitrary",), vmem_limit_bytes=VMEM_LIMIT),
        name="qkv_proj",
    )(x, w_in, w_in, w_in)


def _moba_kernel(qt_ref, k_ref, vt_ref, km_ref, o_ref, s_scr, vt1_scr, *, n_blocks, hd):
    blk = MOBA_BLOCK
    n_heads = qt_ref.shape[0] // hd
    blk_id = lax.broadcasted_iota(jnp.int32, (n_blocks, blk), 0)
    key_pos = lax.broadcasted_iota(jnp.int32, (blk, blk), 0)
    qry_pos = lax.broadcasted_iota(jnp.int32, (blk, blk), 1)
    causal = key_pos <= qry_pos

    def q_tile(h, qb):
        return qt_ref[h * hd:(h + 1) * hd, qb * blk:(qb + 1) * blk]

    def choice_bias(h, qb):
        km = km_ref[:, h * hd:(h + 1) * hd]
        km_hi = km.astype(BF16).astype(F32)
        km_split = jnp.concatenate([km_hi, km - km_hi], axis=0).astype(BF16)
        g2 = _dot(km_split, q_tile(h, qb))
        gate = g2[:n_blocks] + g2[n_blocks:]
        past = blk_id < qb
        gate = jnp.where(past, gate, -jnp.inf)
        beaten = jnp.zeros(gate.shape, jnp.int32)
        for m in range(qb):
            gm = gate[m:m + 1, :]
            wins = (gm > gate) | ((gm == gate) & (blk_id > m))
            beaten = beaten + wins.astype(jnp.int32)
        return jnp.where(past & (beaten < MOBA_TOPK), 0.0, -jnp.inf)

    biases = {(h, qb): choice_bias(h, qb)
              for h in range(n_heads) for qb in range(MOBA_TOPK + 1, n_blocks)}

    ones_rows = vt1_scr.shape[1] - hd
    for h in range(n_heads):
        vt1_scr[h, 0:hd, :] = vt_ref[h * hd:(h + 1) * hd, :]
        vt1_scr[h, hd:, :] = jnp.ones((ones_rows, vt1_scr.shape[2]), BF16)

    def slots(group):
        base, out = 0, []
        for qb in group:
            out.append(base)
            base += qb + 1
        assert base <= s_scr.shape[2]
        return out

    def items(group):
        per_head = [(qb, base + n, n) for qb, base in zip(group, slots(group))
                    for n in range(qb + 1)]
        return [(h,) + it for it in per_head for h in range(n_heads)]

    col_max8 = {}
    acc = {}

    def score_block(half, h, qb, slot, n):
        s = _dot(k_ref[n * blk:(n + 1) * blk, h * hd:(h + 1) * hd], q_tile(h, qb))
        if n == qb:
            s = jnp.where(causal, s, -jnp.inf)
        elif (h, qb) in biases:
            s = s + biases[h, qb][n:n + 1, :]
        s_scr[h, half, slot] = s
        mx = jnp.max(s.reshape(blk // 8, 8, blk), axis=0)
        col_max8[h, qb] = mx if n == 0 else jnp.maximum(col_max8[h, qb], mx)

    def weights_block(half, h, qb, slot, n):
        if n == 0:
            m_run = jnp.max(col_max8[h, qb], axis=0, keepdims=True)
            col_max8[h, qb] = jnp.broadcast_to(m_run, (8, blk))
        p = jnp.exp2(s_scr[h, half, slot].reshape(blk // 8, 8, blk) - col_max8[h, qb])
        return p.reshape(blk, blk).astype(BF16)

    def pv_block(p, h, qb, n):
        pv = _dot(vt1_scr[h, :, n * blk:(n + 1) * blk], p)
        acc[h, qb] = pv if n == 0 else acc[h, qb] + pv
        if n == qb:
            a = acc.pop((h, qb))
            out = (a[:hd] * (1.0 / a[hd:hd + 1])).T
            o_ref[qb * blk:(qb + 1) * blk, h * hd:(h + 1) * hd] = out.astype(o_ref.dtype)

    first = MOBA_TOPK
    rest = [qb for qb in range(n_blocks - 1, -1, -1) if qb != first]
    groups = [[first]]
    while rest:
        big = rest.pop(0)
        small = [qb for qb in rest if big + qb + 2 <= n_blocks][:1]
        for qb in small:
            rest.remove(qb)
        groups.append([big] + small)
    for item in items(groups[0]):
        score_block(0, *item)
    ready = []
    for j, group in enumerate(groups):
        cur = items(group)
        nxt = items(groups[j + 1]) if j + 1 < len(groups) else []
        for t in range(max(len(cur), len(nxt))):
            if t < len(nxt):
                score_block((j + 1) % 2, *nxt[t])
            if len(ready) > PV_LAG or (ready and t >= len(cur)):
                pv_block(*ready.pop(0))
            if t < len(cur):
                h, qb, slot, n = cur[t]
                ready.append((weights_block(j % 2, h, qb, slot, n), h, qb, n))
    while ready:
        pv_block(*ready.pop(0))


def _moba_attention(k, qvt, kmean, batch, seq):
    t, a = k.shape
    hd = HEAD_DIM
    n_heads = a // hd
    n_blocks = seq // MOBA_BLOCK
    assert seq % MOBA_BLOCK == 0 and kmean.shape == (batch, n_blocks, a)
    assert n_heads % ATTN_HEADS == 0
    hw = ATTN_HEADS * hd
    steps = n_heads // ATTN_HEADS
    return pl.pallas_call(
        functools.partial(_moba_kernel, n_blocks=n_blocks, hd=hd),
        grid=(batch, steps),
        in_specs=[
            pl.BlockSpec((hw, seq), lambda b, h: (h, b)),
            pl.BlockSpec((seq, hw), lambda b, h: (b, h)),
            pl.BlockSpec((hw, seq), lambda b, h: (steps + h, b)),
            pl.BlockSpec((None, n_blocks, hw), lambda b, h: (b, 0, h)),
        ],
        out_specs=pl.BlockSpec((seq, hw), lambda b, h: (b, h)),
        out_shape=jax.ShapeDtypeStruct((t, a), BF16),
        scratch_shapes=[
            pltpu.VMEM((ATTN_HEADS, 2, n_blocks, MOBA_BLOCK, MOBA_BLOCK), F32),
            pltpu.VMEM((ATTN_HEADS, hd + BF16_SUBLANES, seq), BF16),
        ],
        compiler_params=pltpu.CompilerParams(
            dimension_semantics=("parallel", "parallel"), vmem_limit_bytes=VMEM_LIMIT),
        name="moba_attn",
    )(qvt, k, qvt, kmean)


def _tail_kernel(x_ref, a_ref, wc_ref, wg_ref, bg_ref, cw_ref, cb_ref, wpa_ref, wpc_ref,
                 wo_ref, g_ref, b_ref, o_ref, u_scr, c_scr, mix_scr, *, alpha, tiles_per_seq):
    tm, d = x_ref.shape
    i = pl.program_id(0)

    @pl.when(i % tiles_per_seq == 0)
    def _():
        u_scr[0:CONV_HALO, :] = jnp.zeros((CONV_HALO, d), F32)

    @pl.when(i % tiles_per_seq != 0)
    def _():
        u_scr[0:CONV_HALO, :] = u_scr[tm:tm + CONV_HALO, :]

    for r0 in range(0, tm, SUB_TILE):
        rows = slice(r0, r0 + SUB_TILE)
        x = x_ref[rows, :]
        xb = x.astype(BF16)
        for c0 in range(0, d, COL_CHUNK):
            cs = slice(c0, c0 + COL_CHUNK)
            h = _dot(xb, wc_ref[:, c0:c0 + COL_CHUNK])
            g_b = _dot(xb, wc_ref[:, d + c0:d + c0 + COL_CHUNK])
            g_c = _dot(xb, wc_ref[:, 2 * d + c0:2 * d + c0 + COL_CHUNK])
            u = g_c * h
            top = CONV_HALO + r0
            u_scr[top:top + SUB_TILE, cs] = u
            u1 = u_scr[top - 1:top - 1 + SUB_TILE, cs]
            u2 = u_scr[top - 2:top - 2 + SUB_TILE, cs]
            y = (cw_ref[0:1, cs] * u2 + cw_ref[1:2, cs] * u1 + cw_ref[2:3, cs] * u
                 + cb_ref[:, cs])
            c_scr[rows, cs] = (g_b * y).astype(BF16)

        att = a_ref[rows, :]
        cc = c_scr[rows, :]
        for c0 in range(0, d, COL_CHUNK):
            cs = slice(c0, c0 + COL_CHUNK)
            y_attn = _dot(att, wpa_ref[:, cs])
            y_conv = _dot(cc, wpc_ref[:, cs])
            gate_a = jax.nn.sigmoid(_dot(xb, wg_ref[:, c0:c0 + COL_CHUNK]) + bg_ref[:, cs])
            gate_c = jax.nn.sigmoid(_dot(xb, wg_ref[:, d + c0:d + c0 + COL_CHUNK])
                                    + bg_ref[:, d + c0:d + c0 + COL_CHUNK])
            mix_scr[rows, cs] = (gate_a * y_attn + gate_c * y_conv).astype(BF16)

        mixed = _dot(mix_scr[rows, :], wo_ref[...])
        o_ref[rows, :] = _layer_norm(alpha * x + mixed, g_ref[...], b_ref[...])


def _mixer_tail(x, attn, w_in, b_gate, conv_w, conv_b, w_pa, w_pc, w_out, g, b, layer, alpha, seq):
    t, d = x.shape
    a = attn.shape[1]
    assert t % TAIL_TILE == 0 and seq % TAIL_TILE == 0 and d % COL_CHUNK == 0
    assert TAIL_TILE % SUB_TILE == 0
    assert w_pa.shape[1:] == (a, d) and a == d, "column-block indexing of w_in assumes A == D"
    assert w_pc.shape[1:] == (d, d) and conv_w.shape[1:] == (CONV_K, d)
    assert w_in.shape[2] == 3 * a + 5 * d
    tile = lambda i: (i, 0)
    lyr = (layer, 0, 0)
    return pl.pallas_call(
        functools.partial(_tail_kernel, alpha=alpha, tiles_per_seq=seq // TAIL_TILE),
        grid=(t // TAIL_TILE,),
        in_specs=[
            pl.BlockSpec((TAIL_TILE, d), tile),
            pl.BlockSpec((TAIL_TILE, a), tile),
            _resident((None, d, 3 * d), (layer, 0, 1)),
            _resident((None, d, 2 * d), (layer, 0, 3)),
            _resident((None, 1, 2 * d), lyr),
            _resident((None, CONV_K, d), lyr),
            _resident((None, 1, d), lyr),
            _resident((None, a, d), lyr),
            _resident((None, d, d), lyr),
            _resident((None, d, d), lyr),
            _resident((None, 1, d), lyr),
            _resident((None, 1, d), lyr),
        ],
        out_specs=pl.BlockSpec((TAIL_TILE, d), tile),
        out_shape=jax.ShapeDtypeStruct((t, d), F32),
        scratch_shapes=[
            pltpu.VMEM((TAIL_TILE + CONV_HALO, d), F32),
            pltpu.VMEM((TAIL_TILE, d), BF16),
            pltpu.VMEM((TAIL_TILE, d), BF16),
        ],
        compiler_params=pltpu.CompilerParams(
            dimension_semantics=("arbitrary",), vmem_limit_bytes=VMEM_LIMIT),
        name="mixer_tail",
    )(x, attn, w_in, w_in, b_gate, conv_w, conv_b, w_pa, w_pc, w_out, g, b)


def kernel(x, ffn1_w_up, ffn1_w_down, ln1_g, ln1_b, w_in, b_gate, conv_w, conv_b,
           w_proj_attn, w_proj_conv, w_out, ln2_g, ln2_b, ffn2_w_up, ffn2_w_down,
           ln3_g, ln3_b):
    batch, seq, d = x.shape
    depth = w_in.shape[0]
    a = N_HEADS * HEAD_DIM
    assert w_in.shape[2] == 3 * a + 3 * d + 2 * d and w_proj_attn.shape[1] == a
    alpha = (2.0 * depth) ** 0.25
    rows = lambda v: v.reshape(depth, 1, -1)

    h = x.reshape(batch * seq, d)
    for l in range(depth):
        h = _ffn_ln(h, ffn1_w_up, ffn1_w_down, rows(ln1_g), rows(ln1_b), l, alpha)
        k, qvt, kmean = _qkv_proj(h, w_in, a, l)
        attn = _moba_attention(k, qvt, kmean.reshape(batch, seq // MOBA_BLOCK, a), batch, seq)
        h = _mixer_tail(h, attn, w_in, rows(b_gate), conv_w, rows(conv_b), w_proj_attn,
                        w_proj_conv, w_out, rows(ln2_g), rows(ln2_b), l, alpha, seq)
        h = _ffn_ln(h, ffn2_w_up, ffn2_w_down, rows(ln3_g), rows(ln3_b), l, alpha)
    return h.reshape(batch, seq, d)
```

```python
import functools
import math

import jax
import jax.numpy as jnp
from jax import lax
from jax.experimental import pallas as pl
from jax.experimental.pallas import tpu as pltpu

N_HEADS = 8
HEAD_DIM = 128
MOBA_BLOCK = 256
MOBA_TOPK = 3
CONV_K = 3
LN_EPS = 1e-5

F32 = jnp.float32
BF16 = jnp.bfloat16

TOKEN_TILE = 1024
FFN_TILE = 1024
TAIL_TILE = 1024
SUB_TILE = 256
FF_CHUNK = 256
COL_CHUNK = 256
CONV_HALO = 8
BF16_SUBLANES = 16
UP_CHUNKS = 8
TAIL_WEIGHT_CHUNKS = 8
PV_LAG = 4
ATTN_HEADS = 2
VMEM_LIMIT = 56 * 1024 * 1024


def _resident(block, index):
    return pl.BlockSpec(block, lambda *_: index, pipeline_mode=pl.Buffered(1))


def _layer_norm(y, g, b):
    mu = jnp.mean(y, axis=-1, keepdims=True)
    d = y - mu
    var = jnp.mean(d * d, axis=-1, keepdims=True)
    return d * lax.rsqrt(var + LN_EPS) * g + b


def _dot(a, b):
    return jnp.dot(a, b, preferred_element_type=F32)


def _row_chunks(src_hbm, dst_scr, rows):
    assert src_hbm.shape == dst_scr.shape and src_hbm.shape[0] % rows == 0
    return [(src_hbm.at[pl.ds(r0, rows), :], dst_scr.at[pl.ds(r0, rows), :])
            for r0 in range(0, src_hbm.shape[0], rows)]


def _load_as_bf16(jobs):
    def copy(job, j):
        chunks, stage, sem = job
        assert chunks[j][0].shape == stage.shape[1:] and stage.shape[1] % BF16_SUBLANES == 0
        return pltpu.make_async_copy(chunks[j][0], stage.at[j % 2], sem.at[j % 2])

    for job in jobs:
        for j in range(min(2, len(job[0]))):
            copy(job, j).start()
    for j in range(max(len(job[0]) for job in jobs)):
        for job in jobs:
            chunks, stage, _ = job
            if j < len(chunks):
                copy(job, j).wait()
                chunks[j][1][...] = stage[j % 2].astype(BF16)
                if j + 2 < len(chunks):
                    copy(job, j + 2).start()


def _ffn_ln_kernel(x_ref, wup_hbm, wdn_hbm, g_ref, b_ref, o_ref, wup_ref, wdn_ref,
                   up_stage, dn_stage, sems, *, alpha, d_ff, layer):
    @pl.when(pl.program_id(0) == 0)
    def _():
        _load_as_bf16([
            (_row_chunks(wup_hbm.at[layer], wup_ref, up_stage.shape[1]), up_stage, sems.at[0]),
            (_row_chunks(wdn_hbm.at[layer], wdn_ref, dn_stage.shape[1]), dn_stage, sems.at[1])])

    chunks = list(range(0, d_ff, FF_CHUNK))
    for r0 in range(0, x_ref.shape[0], SUB_TILE):
        x = x_ref[r0:r0 + SUB_TILE, :]
        xb = x.astype(BF16)

        def gate_up(c0):
            return (_dot(xb, wup_ref[:, c0:c0 + FF_CHUNK]),
                    _dot(xb, wup_ref[:, d_ff + c0:d_ff + c0 + FF_CHUNK]))

        pending = gate_up(chunks[0])
        acc = None
        for j, c0 in enumerate(chunks):
            gate, up = pending
            if j + 1 < len(chunks):
                pending = gate_up(chunks[j + 1])
            h = (gate * jax.nn.sigmoid(gate) * up).astype(BF16)
            down = _dot(h, wdn_ref[c0:c0 + FF_CHUNK, :])
            acc = down if acc is None else acc + down
        o_ref[r0:r0 + SUB_TILE, :] = _layer_norm(alpha * x + 0.5 * acc, g_ref[...], b_ref[...])


def _ffn_ln(x, w_up, w_down, g, b, layer, alpha):
    t, d = x.shape
    d_ff = w_down.shape[1]
    assert t % FFN_TILE == 0 and FFN_TILE % SUB_TILE == 0 and d_ff % FF_CHUNK == 0
    up_rows, dn_rows = d // UP_CHUNKS, 2 * d_ff // UP_CHUNKS
    assert d % UP_CHUNKS == 0 and d_ff % dn_rows == 0
    return pl.pallas_call(
        functools.partial(_ffn_ln_kernel, alpha=alpha, d_ff=d_ff, layer=layer),
        grid=(t // FFN_TILE,),
        in_specs=[
            pl.BlockSpec((FFN_TILE, d), lambda i: (i, 0)),
            pl.BlockSpec(memory_space=pl.ANY),
            pl.BlockSpec(memory_space=pl.ANY),
            _resident((None, 1, d), (layer, 0, 0)),
            _resident((None, 1, d), (layer, 0, 0)),
        ],
        out_specs=pl.BlockSpec((FFN_TILE, d), lambda i: (i, 0)),
        out_shape=jax.ShapeDtypeStruct((t, d), F32),
        scratch_shapes=[
            pltpu.VMEM((d, 2 * d_ff), BF16),
            pltpu.VMEM((d_ff, d), BF16),
            pltpu.VMEM((2, up_rows, 2 * d_ff), F32),
            pltpu.VMEM((2, dn_rows, d), F32),
            pltpu.SemaphoreType.DMA((2, 2)),
        ],
        compiler_params=pltpu.CompilerParams(
            dimension_semantics=("arbitrary",), vmem_limit_bytes=VMEM_LIMIT),
        name="ffn_ln",
    )(x, w_up, w_down, g, b)


def _qkv_kernel(x_ref, wq_ref, wk_ref, wv_ref, k_ref, qvt_ref, kmean_ref, wqvt_scr, *, q_scale):
    a = k_ref.shape[1]

    @pl.when(pl.program_id(0) == 0)
    def _():
        for c0 in range(0, wq_ref.shape[0], COL_CHUNK):
            wqvt_scr[0:a, c0:c0 + COL_CHUNK] = wq_ref[c0:c0 + COL_CHUNK, :].T.astype(BF16)
            wqvt_scr[a:, c0:c0 + COL_CHUNK] = wv_ref[c0:c0 + COL_CHUNK, :].T.astype(BF16)

    for r0 in range(0, x_ref.shape[0], SUB_TILE):
        rows = slice(r0, r0 + SUB_TILE)
        xb = x_ref[rows, :].astype(BF16)
        k = _dot(xb, wk_ref[...])
        k_ref[rows, :] = k.astype(BF16)
        for j in range(SUB_TILE // MOBA_BLOCK):
            blk = k[j * MOBA_BLOCK:(j + 1) * MOBA_BLOCK]
            row = r0 // MOBA_BLOCK + j
            kmean_ref[0, row:row + 1, :] = jnp.sum(blk, axis=0, keepdims=True) * (1.0 / MOBA_BLOCK)
        qvt = lax.dot_general(wqvt_scr[...], xb, (((1,), (1,)), ((), ())),
                              preferred_element_type=F32)
        qvt_ref[0:a, rows] = (qvt[0:a] * q_scale).astype(BF16)
        qvt_ref[a:, rows] = qvt[a:].astype(BF16)


def _qkv_proj(x, w_in, a, layer):
    t, d = x.shape
    q_scale = (HEAD_DIM ** -0.5) * math.log2(math.e)
    assert t % TOKEN_TILE == 0 and TOKEN_TILE % SUB_TILE == 0 and SUB_TILE % MOBA_BLOCK == 0
    assert d % COL_CHUNK == 0
    n_tiles, blocks_per_tile = t // TOKEN_TILE, TOKEN_TILE // MOBA_BLOCK
    return pl.pallas_call(
        functools.partial(_qkv_kernel, q_scale=q_scale),
        grid=(n_tiles,),
        in_specs=[
            pl.BlockSpec((TOKEN_TILE, d), lambda i: (i, 0)),
            _resident((None, d, a), (layer, 0, 0)),
            _resident((None, d, a), (layer, 0, 1)),
            _resident((None, d, a), (layer, 0, 2)),
        ],
        out_specs=[
            pl.BlockSpec((TOKEN_TILE, a), lambda i: (i, 0)),
            pl.BlockSpec((2 * a, TOKEN_TILE), lambda i: (0, i)),
            pl.BlockSpec((1, blocks_per_tile, a), lambda i: (i, 0, 0)),
        ],
        out_shape=[
            jax.ShapeDtypeStruct((t, a), BF16),
            jax.ShapeDtypeStruct((2 * a, t), BF16),
            jax.ShapeDtypeStruct((n_tiles, blocks_per_tile, a), F32),
        ],
        scratch_shapes=[pltpu.VMEM((2 * a, d), BF16)],
        compiler_params=pltpu.CompilerParams(
            dimension_semantics=("arbitrary",), vmem_limit_bytes=VMEM_LIMIT),
        name="qkv_proj",
    )(x, w_in, w_in, w_in)


def _moba_kernel(qt_ref, k_ref, vt_ref, km_ref, o_ref, s_scr, vt1_scr, *, n_blocks, hd):
    blk = MOBA_BLOCK
    n_heads = qt_ref.shape[0] // hd
    blk_id = lax.broadcasted_iota(jnp.int32, (n_blocks, blk), 0)
    key_pos = lax.broadcasted_iota(jnp.int32, (blk, blk), 0)
    qry_pos = lax.broadcasted_iota(jnp.int32, (blk, blk), 1)
    causal = key_pos <= qry_pos

    def q_tile(h, qb):
        return qt_ref[h * hd:(h + 1) * hd, qb * blk:(qb + 1) * blk]

    def choice_bias(h, qb):
        km = km_ref[:, h * hd:(h + 1) * hd]
        km_hi = km.astype(BF16).astype(F32)
        km_split = jnp.concatenate([km_hi, km - km_hi], axis=0).astype(BF16)
        g2 = _dot(km_split, q_tile(h, qb))
        gate = g2[:n_blocks] + g2[n_blocks:]
        past = blk_id < qb
        gate = jnp.where(past, gate, -jnp.inf)
        beaten = jnp.zeros(gate.shape, jnp.int32)
        for m in range(qb):
            gm = gate[m:m + 1, :]
            wins = (gm > gate) | ((gm == gate) & (blk_id > m))
            beaten = beaten + wins.astype(jnp.int32)
        return jnp.where(past & (beaten < MOBA_TOPK), 0.0, -jnp.inf)

    biases = {(h, qb): choice_bias(h, qb)
              for h in range(n_heads) for qb in range(MOBA_TOPK + 1, n_blocks)}

    ones_rows = vt1_scr.shape[1] - hd
    for h in range(n_heads):
        vt1_scr[h, 0:hd, :] = vt_ref[h * hd:(h + 1) * hd, :]
        vt1_scr[h, hd:, :] = jnp.ones((ones_rows, vt1_scr.shape[2]), BF16)

    def slots(group):
        base, out = 0, []
        for qb in group:
            out.append(base)
            base += qb + 1
        assert base <= s_scr.shape[2]
        return out

    def items(group):
        per_head = [(qb, base + n, n) for qb, base in zip(group, slots(group))
                    for n in range(qb + 1)]
        return [(h,) + it for it in per_head for h in range(n_heads)]

    col_max8 = {}
    acc = {}

    def score_block(half, h, qb, slot, n):
        s = _dot(k_ref[n * blk:(n + 1) * blk, h * hd:(h + 1) * hd], q_tile(h, qb))
        if n == qb:
            s = jnp.where(causal, s, -jnp.inf)
        elif (h, qb) in biases:
            s = s + biases[h, qb][n:n + 1, :]
        s_scr[h, half, slot] = s
        mx = jnp.max(s.reshape(blk // 8, 8, blk), axis=0)
        col_max8[h, qb] = mx if n == 0 else jnp.maximum(col_max8[h, qb], mx)

    def weights_block(half, h, qb, slot, n):
        if n == 0:
            m_run = jnp.max(col_max8[h, qb], axis=0, keepdims=True)
            col_max8[h, qb] = jnp.broadcast_to(m_run, (8, blk))
        p = jnp.exp2(s_scr[h, half, slot].reshape(blk // 8, 8, blk) - col_max8[h, qb])
        return p.reshape(blk, blk).astype(BF16)

    def pv_block(p, h, qb, n):
        pv = _dot(vt1_scr[h, :, n * blk:(n + 1) * blk], p)
        acc[h, qb] = pv if n == 0 else acc[h, qb] + pv
        if n == qb:
            a = acc.pop((h, qb))
            out = (a[:hd] * (1.0 / a[hd:hd + 1])).T
            o_ref[qb * blk:(qb + 1) * blk, h * hd:(h + 1) * hd] = out.astype(o_ref.dtype)

    first = MOBA_TOPK
    rest = [qb for qb in range(n_blocks - 1, -1, -1) if qb != first]
    groups = [[first]]
    while rest:
        big = rest.pop(0)
        small = [qb for qb in rest if big + qb + 2 <= n_blocks][:1]
        for qb in small:
            rest.remove(qb)
        groups.append([big] + small)
    for item in items(groups[0]):
        score_block(0, *item)
    ready = []
    for j, group in enumerate(groups):
        cur = items(group)
        nxt = items(groups[j + 1]) if j + 1 < len(groups) else []
        for t in range(max(len(cur), len(nxt))):
            if t < len(nxt):
                score_block((j + 1) % 2, *nxt[t])
            if len(ready) > PV_LAG or (ready and t >= len(cur)):
                pv_block(*ready.pop(0))
            if t < len(cur):
                h, qb, slot, n = cur[t]
                ready.append((weights_block(j % 2, h, qb, slot, n), h, qb, n))
    while ready:
        pv_block(*ready.pop(0))


def _moba_attention(k, qvt, kmean, batch, seq):
    t, a = k.shape
    hd = HEAD_DIM
    n_heads = a // hd
    n_blocks = seq // MOBA_BLOCK
    assert seq % MOBA_BLOCK == 0 and kmean.shape == (batch, n_blocks, a)
    assert n_heads % ATTN_HEADS == 0
    hw = ATTN_HEADS * hd
    steps = n_heads // ATTN_HEADS
    return pl.pallas_call(
        functools.partial(_moba_kernel, n_blocks=n_blocks, hd=hd),
        grid=(batch, steps),
        in_specs=[
            pl.BlockSpec((hw, seq), lambda b, h: (h, b)),
            pl.BlockSpec((seq, hw), lambda b, h: (b, h)),
            pl.BlockSpec((hw, seq), lambda b, h: (steps + h, b)),
            pl.BlockSpec((None, n_blocks, hw), lambda b, h: (b, 0, h)),
        ],
        out_specs=pl.BlockSpec((seq, hw), lambda b, h: (b, h)),
        out_shape=jax.ShapeDtypeStruct((t, a), BF16),
        scratch_shapes=[
            pltpu.VMEM((ATTN_HEADS, 2, n_blocks, MOBA_BLOCK, MOBA_BLOCK), F32),
            pltpu.VMEM((ATTN_HEADS, hd + BF16_SUBLANES, seq), BF16),
        ],
        compiler_params=pltpu.CompilerParams(
            dimension_semantics=("parallel", "parallel"), vmem_limit_bytes=VMEM_LIMIT),
        name="moba_attn",
    )(qvt, k, qvt, kmean)


def _tail_kernel(x_ref, a_ref, win_hbm, bg_ref, cw_ref, cb_ref, wpa_hbm, wpc_hbm, wo_hbm,
                 g_ref, b_ref, o_ref, u_scr, c_scr, mix_scr, wc_ref, wg_ref, wpa_ref, wpc_ref,
                 wo_ref, c_stage, g_stage, p_stage, sems, *, alpha, tiles_per_seq, layer):
    tm, d = x_ref.shape
    a = a_ref.shape[1]
    i = pl.program_id(0)

    @pl.when(i == 0)
    def _():
        n_conv, n_gate = wc_ref.shape[1], wg_ref.shape[1]
        conv_cols = win_hbm.at[layer, pl.ds(0, d), pl.ds(3 * a, n_conv)]
        gate_cols = win_hbm.at[layer, pl.ds(0, d), pl.ds(3 * a + n_conv, n_gate)]
        square = [pair for src, dst in ((wpa_hbm, wpa_ref), (wpc_hbm, wpc_ref), (wo_hbm, wo_ref))
                  for pair in _row_chunks(src.at[layer], dst, p_stage.shape[1])]
        _load_as_bf16([
            (_row_chunks(conv_cols, wc_ref, c_stage.shape[1]), c_stage, sems.at[0]),
            (_row_chunks(gate_cols, wg_ref, g_stage.shape[1]), g_stage, sems.at[1]),
            (square, p_stage, sems.at[2])])

    @pl.when(i % tiles_per_seq == 0)
    def _():
        u_scr[0:CONV_HALO, :] = jnp.zeros((CONV_HALO, d), F32)

    @pl.when(i % tiles_per_seq != 0)
    def _():
        u_scr[0:CONV_HALO, :] = u_scr[tm:tm + CONV_HALO, :]

    for r0 in range(0, tm, SUB_TILE):
        rows = slice(r0, r0 + SUB_TILE)
        x = x_ref[rows, :]
        xb = x.astype(BF16)
        for c0 in range(0, d, COL_CHUNK):
            cs = slice(c0, c0 + COL_CHUNK)
            h = _dot(xb, wc_ref[:, c0:c0 + COL_CHUNK])
            g_b = _dot(xb, wc_ref[:, d + c0:d + c0 + COL_CHUNK])
            g_c = _dot(xb, wc_ref[:, 2 * d + c0:2 * d + c0 + COL_CHUNK])
            u = g_c * h
            top = CONV_HALO + r0
            u_scr[top:top + SUB_TILE, cs] = u
            u1 = u_scr[top - 1:top - 1 + SUB_TILE, cs]
            u2 = u_scr[top - 2:top - 2 + SUB_TILE, cs]
            y = (cw_ref[0:1, cs] * u2 + cw_ref[1:2, cs] * u1 + cw_ref[2:3, cs] * u
                 + cb_ref[:, cs])
            c_scr[rows, cs] = (g_b * y).astype(BF16)

        att = a_ref[rows, :]
        cc = c_scr[rows, :]
        for c0 in range(0, d, COL_CHUNK):
            cs = slice(c0, c0 + COL_CHUNK)
            y_attn = _dot(att, wpa_ref[:, cs])
            y_conv = _dot(cc, wpc_ref[:, cs])
            gate_a = jax.nn.sigmoid(_dot(xb, wg_ref[:, c0:c0 + COL_CHUNK]) + bg_ref[:, cs])
            gate_c = jax.nn.sigmoid(_dot(xb, wg_ref[:, d + c0:d + c0 + COL_CHUNK])
                                    + bg_ref[:, d + c0:d + c0 + COL_CHUNK])
            mix_scr[rows, cs] = (gate_a * y_attn + gate_c * y_conv).astype(BF16)

        mixed = _dot(mix_scr[rows, :], wo_ref[...])
        o_ref[rows, :] = _layer_norm(alpha * x + mixed, g_ref[...], b_ref[...])


def _mixer_tail(x, attn, w_in, b_gate, conv_w, conv_b, w_pa, w_pc, w_out, g, b, layer, alpha, seq):
    t, d = x.shape
    a = attn.shape[1]
    assert t % TAIL_TILE == 0 and seq % TAIL_TILE == 0 and d % COL_CHUNK == 0
    assert TAIL_TILE % SUB_TILE == 0 and d % TAIL_WEIGHT_CHUNKS == 0
    assert w_pa.shape[1:] == (a, d) and a == d
    assert w_pc.shape[1:] == (d, d) and w_out.shape[1:] == (d, d)
    assert conv_w.shape[1:] == (CONV_K, d) and w_in.shape[2] == 3 * a + 5 * d
    tile = lambda i: (i, 0)
    lyr = (layer, 0, 0)
    hbm = pl.BlockSpec(memory_space=pl.ANY)
    return pl.pallas_call(
        functools.partial(_tail_kernel, alpha=alpha, tiles_per_seq=seq // TAIL_TILE, layer=layer),
        grid=(t // TAIL_TILE,),
        in_specs=[
            pl.BlockSpec((TAIL_TILE, d), tile),
            pl.BlockSpec((TAIL_TILE, a), tile),
            hbm,
            _resident((None, 1, 2 * d), lyr),
            _resident((None, CONV_K, d), lyr),
            _resident((None, 1, d), lyr),
            hbm, hbm, hbm,
            _resident((None, 1, d), lyr),
            _resident((None, 1, d), lyr),
        ],
        out_specs=pl.BlockSpec((TAIL_TILE, d), tile),
        out_shape=jax.ShapeDtypeStruct((t, d), F32),
        scratch_shapes=[
            pltpu.VMEM((TAIL_TILE + CONV_HALO, d), F32),
            pltpu.VMEM((TAIL_TILE, d), BF16),
            pltpu.VMEM((TAIL_TILE, d), BF16),
            pltpu.VMEM((d, 3 * d), BF16),
            pltpu.VMEM((d, 2 * d), BF16),
            pltpu.VMEM((a, d), BF16),
            pltpu.VMEM((d, d), BF16),
            pltpu.VMEM((d, d), BF16),
            pltpu.VMEM((2, d // TAIL_WEIGHT_CHUNKS, 3 * d), F32),
            pltpu.VMEM((2, d // TAIL_WEIGHT_CHUNKS, 2 * d), F32),
            pltpu.VMEM((2, 2 * d // TAIL_WEIGHT_CHUNKS, d), F32),
            pltpu.SemaphoreType.DMA((3, 2)),
        ],
        compiler_params=pltpu.CompilerParams(
            dimension_semantics=("arbitrary",), vmem_limit_bytes=VMEM_LIMIT),
        name="mixer_tail",
    )(x, attn, w_in, b_gate, conv_w, conv_b, w_pa, w_pc, w_out, g, b)


def kernel(x, ffn1_w_up, ffn1_w_down, ln1_g, ln1_b, w_in, b_gate, conv_w, conv_b,
           w_proj_attn, w_proj_conv, w_out, ln2_g, ln2_b, ffn2_w_up, ffn2_w_down,
           ln3_g, ln3_b):
    batch, seq, d = x.shape
    depth = w_in.shape[0]
    a = N_HEADS * HEAD_DIM
    assert w_in.shape[2] == 3 * a + 3 * d + 2 * d and w_proj_attn.shape[1] == a
    alpha = (2.0 * depth) ** 0.25
    rows = lambda v: v.reshape(depth, 1, -1)

    h = x.reshape(batch * seq, d)
    for l in range(depth):
        h = _ffn_ln(h, ffn1_w_up, ffn1_w_down, rows(ln1_g), rows(ln1_b), l, alpha)
        k, qvt, kmean = _qkv_proj(h, w_in, a, l)
        attn = _moba_attention(k, qvt, kmean.reshape(batch, seq // MOBA_BLOCK, a), batch, seq)
        h = _mixer_tail(h, attn, w_in, rows(b_gate), conv_w, rows(conv_b), w_proj_attn,
                        w_proj_conv, w_out, rows(ln2_g), rows(ln2_b), l, alpha, seq)
        h = _ffn_ln(h, ffn2_w_up, ffn2_w_down, rows(ln3_g), rows(ln3_b), l, alpha)
    return h.reshape(batch, seq, d)
```

```python
import functools
import math

import jax
import jax.numpy as jnp
from jax import lax
from jax.experimental import pallas as pl
from jax.experimental.pallas import tpu as pltpu

N_HEADS = 8
HEAD_DIM = 128
MOBA_BLOCK = 256
MOBA_TOPK = 3
CONV_K = 3
LN_EPS = 1e-5

F32 = jnp.float32
BF16 = jnp.bfloat16

TOKEN_TILE = 1024
FFN_TILE = 1024
TAIL_TILE = 1024
SUB_TILE = 256
FF_CHUNK = 256
COL_CHUNK = 256
CONV_HALO = 8
BF16_SUBLANES = 16
UP_CHUNKS = 8
TAIL_WEIGHT_CHUNKS = 8
WEIGHT_SLOTS = 3
PV_LAG = 4
ATTN_HEADS = 2
VMEM_LIMIT = 56 * 1024 * 1024


def _resident(block, index):
    return pl.BlockSpec(block, lambda *_: index, pipeline_mode=pl.Buffered(1))


def _layer_norm(y, g, b):
    mu = jnp.mean(y, axis=-1, keepdims=True)
    d = y - mu
    var = jnp.mean(d * d, axis=-1, keepdims=True)
    return d * lax.rsqrt(var + LN_EPS) * g + b


def _dot(a, b):
    return jnp.dot(a, b, preferred_element_type=F32)


def _row_chunks(src_hbm, dst_scr, rows):
    assert src_hbm.shape == dst_scr.shape and src_hbm.shape[0] % rows == 0
    return [(src_hbm.at[pl.ds(r0, rows), :], dst_scr.at[pl.ds(r0, rows), :])
            for r0 in range(0, src_hbm.shape[0], rows)]


def _load_as_bf16(jobs):
    def copy(job, j):
        chunks, stage, sem = job
        slot = j % stage.shape[0]
        assert chunks[j][0].shape == stage.shape[1:] and stage.shape[1] % BF16_SUBLANES == 0
        return pltpu.make_async_copy(chunks[j][0], stage.at[slot], sem.at[slot])

    for job in jobs:
        for j in range(min(job[1].shape[0], len(job[0]))):
            copy(job, j).start()
    for j in range(max(len(job[0]) for job in jobs)):
        for job in jobs:
            chunks, stage, _ = job
            if j < len(chunks):
                copy(job, j).wait()
                chunks[j][1][...] = stage[j % stage.shape[0]].astype(BF16)
                if j + stage.shape[0] < len(chunks):
                    copy(job, j + stage.shape[0]).start()


def _ffn_ln_kernel(x_ref, wup_hbm, wdn_hbm, g_ref, b_ref, o_ref, wup_ref, wdn_ref,
                   up_stage, dn_stage, sems, *, alpha, d_ff, layer):
    @pl.when(pl.program_id(0) == 0)
    def _():
        _load_as_bf16([
            (_row_chunks(wup_hbm.at[layer], wup_ref, up_stage.shape[1]), up_stage, sems.at[0]),
            (_row_chunks(wdn_hbm.at[layer], wdn_ref, dn_stage.shape[1]), dn_stage, sems.at[1])])

    chunks = list(range(0, d_ff, FF_CHUNK))
    for r0 in range(0, x_ref.shape[0], SUB_TILE):
        x = x_ref[r0:r0 + SUB_TILE, :]
        xb = x.astype(BF16)

        def gate_up(c0):
            return (_dot(xb, wup_ref[:, c0:c0 + FF_CHUNK]),
                    _dot(xb, wup_ref[:, d_ff + c0:d_ff + c0 + FF_CHUNK]))

        pending = gate_up(chunks[0])
        acc = None
        for j, c0 in enumerate(chunks):
            gate, up = pending
            if j + 1 < len(chunks):
                pending = gate_up(chunks[j + 1])
            h = (gate * jax.nn.sigmoid(gate) * up).astype(BF16)
            down = _dot(h, wdn_ref[c0:c0 + FF_CHUNK, :])
            acc = down if acc is None else acc + down
        o_ref[r0:r0 + SUB_TILE, :] = _layer_norm(alpha * x + 0.5 * acc, g_ref[...], b_ref[...])


def _ffn_ln(x, w_up, w_down, g, b, layer, alpha):
    t, d = x.shape
    d_ff = w_down.shape[1]
    assert t % FFN_TILE == 0 and FFN_TILE % SUB_TILE == 0 and d_ff % FF_CHUNK == 0
    up_rows, dn_rows = d // UP_CHUNKS, 2 * d_ff // UP_CHUNKS
    assert d % UP_CHUNKS == 0 and d_ff % dn_rows == 0
    return pl.pallas_call(
        functools.partial(_ffn_ln_kernel, alpha=alpha, d_ff=d_ff, layer=layer),
        grid=(t // FFN_TILE,),
        in_specs=[
            pl.BlockSpec((FFN_TILE, d), lambda i: (i, 0)),
            pl.BlockSpec(memory_space=pl.ANY),
            pl.BlockSpec(memory_space=pl.ANY),
            _resident((None, 1, d), (layer, 0, 0)),
            _resident((None, 1, d), (layer, 0, 0)),
        ],
        out_specs=pl.BlockSpec((FFN_TILE, d), lambda i: (i, 0)),
        out_shape=jax.ShapeDtypeStruct((t, d), F32),
        scratch_shapes=[
            pltpu.VMEM((d, 2 * d_ff), BF16),
            pltpu.VMEM((d_ff, d), BF16),
            pltpu.VMEM((WEIGHT_SLOTS, up_rows, 2 * d_ff), F32),
            pltpu.VMEM((WEIGHT_SLOTS, dn_rows, d), F32),
            pltpu.SemaphoreType.DMA((2, WEIGHT_SLOTS)),
        ],
        compiler_params=pltpu.CompilerParams(
            dimension_semantics=("arbitrary",), vmem_limit_bytes=VMEM_LIMIT),
        name="ffn_ln",
    )(x, w_up, w_down, g, b)


def _qkv_kernel(x_ref, wq_ref, wk_ref, wv_ref, k_ref, qvt_ref, kmean_ref, wqvt_scr, *, q_scale):
    a = k_ref.shape[1]

    @pl.when(pl.program_id(0) == 0)
    def _():
        for c0 in range(0, wq_ref.shape[0], COL_CHUNK):
            wqvt_scr[0:a, c0:c0 + COL_CHUNK] = wq_ref[c0:c0 + COL_CHUNK, :].T.astype(BF16)
            wqvt_scr[a:, c0:c0 + COL_CHUNK] = wv_ref[c0:c0 + COL_CHUNK, :].T.astype(BF16)

    for r0 in range(0, x_ref.shape[0], SUB_TILE):
        rows = slice(r0, r0 + SUB_TILE)
        xb = x_ref[rows, :].astype(BF16)
        k = _dot(xb, wk_ref[...])
        k_ref[rows, :] = k.astype(BF16)
        for j in range(SUB_TILE // MOBA_BLOCK):
            blk = k[j * MOBA_BLOCK:(j + 1) * MOBA_BLOCK]
            row = r0 // MOBA_BLOCK + j
            kmean_ref[0, row:row + 1, :] = jnp.sum(blk, axis=0, keepdims=True) * (1.0 / MOBA_BLOCK)
        qvt = lax.dot_general(wqvt_scr[...], xb, (((1,), (1,)), ((), ())),
                              preferred_element_type=F32)
        qvt_ref[0:a, rows] = (qvt[0:a] * q_scale).astype(BF16)
        qvt_ref[a:, rows] = qvt[a:].astype(BF16)


def _qkv_proj(x, w_in, a, layer):
    t, d = x.shape
    q_scale = (HEAD_DIM ** -0.5) * math.log2(math.e)
    assert t % TOKEN_TILE == 0 and TOKEN_TILE % SUB_TILE == 0 and SUB_TILE % MOBA_BLOCK == 0
    assert d % COL_CHUNK == 0
    n_tiles, blocks_per_tile = t // TOKEN_TILE, TOKEN_TILE // MOBA_BLOCK
    return pl.pallas_call(
        functools.partial(_qkv_kernel, q_scale=q_scale),
        grid=(n_tiles,),
        in_specs=[
            pl.BlockSpec((TOKEN_TILE, d), lambda i: (i, 0)),
            _resident((None, d, a), (layer, 0, 0)),
            _resident((None, d, a), (layer, 0, 1)),
            _resident((None, d, a), (layer, 0, 2)),
        ],
        out_specs=[
            pl.BlockSpec((TOKEN_TILE, a), lambda i: (i, 0)),
            pl.BlockSpec((2 * a, TOKEN_TILE), lambda i: (0, i)),
            pl.BlockSpec((1, blocks_per_tile, a), lambda i: (i, 0, 0)),
        ],
        out_shape=[
            jax.ShapeDtypeStruct((t, a), BF16),
            jax.ShapeDtypeStruct((2 * a, t), BF16),
            jax.ShapeDtypeStruct((n_tiles, blocks_per_tile, a), F32),
        ],
        scratch_shapes=[pltpu.VMEM((2 * a, d), BF16)],
        compiler_params=pltpu.CompilerParams(
            dimension_semantics=("arbitrary",), vmem_limit_bytes=VMEM_LIMIT),
        name="qkv_proj",
    )(x, w_in, w_in, w_in)


def _moba_kernel(qt_ref, k_ref, vt_ref, km_ref, o_ref, s_scr, vt1_scr, *, n_blocks, hd):
    blk = MOBA_BLOCK
    n_heads = qt_ref.shape[0] // hd
    blk_id = lax.broadcasted_iota(jnp.int32, (n_blocks, blk), 0)
    key_pos = lax.broadcasted_iota(jnp.int32, (blk, blk), 0)
    qry_pos = lax.broadcasted_iota(jnp.int32, (blk, blk), 1)
    causal = key_pos <= qry_pos

    def q_tile(h, qb):
        return qt_ref[h * hd:(h + 1) * hd, qb * blk:(qb + 1) * blk]

    def choice_bias(h, qb):
        km = km_ref[:, h * hd:(h + 1) * hd]
        km_hi = km.astype(BF16).astype(F32)
        km_split = jnp.concatenate([km_hi, km - km_hi], axis=0).astype(BF16)
        g2 = _dot(km_split, q_tile(h, qb))
        gate = g2[:n_blocks] + g2[n_blocks:]
        past = blk_id < qb
        gate = jnp.where(past, gate, -jnp.inf)
        beaten = jnp.zeros(gate.shape, jnp.int32)
        for m in range(qb):
            gm = gate[m:m + 1, :]
            wins = (gm > gate) | ((gm == gate) & (blk_id > m))
            beaten = beaten + wins.astype(jnp.int32)
        return jnp.where(past & (beaten < MOBA_TOPK), 0.0, -jnp.inf)

    biases = {(h, qb): choice_bias(h, qb)
              for h in range(n_heads) for qb in range(MOBA_TOPK + 1, n_blocks)}

    ones_rows = vt1_scr.shape[1] - hd
    for h in range(n_heads):
        vt1_scr[h, 0:hd, :] = vt_ref[h * hd:(h + 1) * hd, :]
        vt1_scr[h, hd:, :] = jnp.ones((ones_rows, vt1_scr.shape[2]), BF16)

    def slots(group):
        base, out = 0, []
        for qb in group:
            out.append(base)
            base += qb + 1
        assert base <= s_scr.shape[2]
        return out

    def items(group):
        per_head = [(qb, base + n, n) for qb, base in zip(group, slots(group))
                    for n in range(qb + 1)]
        return [(h,) + it for it in per_head for h in range(n_heads)]

    col_max8 = {}
    acc = {}

    def score_block(half, h, qb, slot, n):
        s = _dot(k_ref[n * blk:(n + 1) * blk, h * hd:(h + 1) * hd], q_tile(h, qb))
        if n == qb:
            s = jnp.where(causal, s, -jnp.inf)
        elif (h, qb) in biases:
            s = s + biases[h, qb][n:n + 1, :]
        s_scr[h, half, slot] = s
        mx = jnp.max(s.reshape(blk // 8, 8, blk), axis=0)
        col_max8[h, qb] = mx if n == 0 else jnp.maximum(col_max8[h, qb], mx)

    def weights_block(half, h, qb, slot, n):
        if n == 0:
            m_run = jnp.max(col_max8[h, qb], axis=0, keepdims=True)
            col_max8[h, qb] = jnp.broadcast_to(m_run, (8, blk))
        p = jnp.exp2(s_scr[h, half, slot].reshape(blk // 8, 8, blk) - col_max8[h, qb])
        return p.reshape(blk, blk).astype(BF16)

    def pv_block(p, h, qb, n):
        pv = _dot(vt1_scr[h, :, n * blk:(n + 1) * blk], p)
        acc[h, qb] = pv if n == 0 else acc[h, qb] + pv
        if n == qb:
            a = acc.pop((h, qb))
            out = (a[:hd] * (1.0 / a[hd:hd + 1])).T
            o_ref[qb * blk:(qb + 1) * blk, h * hd:(h + 1) * hd] = out.astype(o_ref.dtype)

    first = MOBA_TOPK
    rest = [qb for qb in range(n_blocks - 1, -1, -1) if qb != first]
    groups = [[first]]
    while rest:
        big = rest.pop(0)
        small = [qb for qb in rest if big + qb + 2 <= n_blocks][:1]
        for qb in small:
            rest.remove(qb)
        groups.append([big] + small)
    for item in items(groups[0]):
        score_block(0, *item)
    ready = []
    for j, group in enumerate(groups):
        cur = items(group)
        nxt = items(groups[j + 1]) if j + 1 < len(groups) else []
        for t in range(max(len(cur), len(nxt))):
            if t < len(nxt):
                score_block((j + 1) % 2, *nxt[t])
            if len(ready) > PV_LAG or (ready and t >= len(cur)):
                pv_block(*ready.pop(0))
            if t < len(cur):
                h, qb, slot, n = cur[t]
                ready.append((weights_block(j % 2, h, qb, slot, n), h, qb, n))
    while ready:
        pv_block(*ready.pop(0))


def _moba_attention(k, qvt, kmean, batch, seq):
    t, a = k.shape
    hd = HEAD_DIM
    n_heads = a // hd
    n_blocks = seq // MOBA_BLOCK
    assert seq % MOBA_BLOCK == 0 and kmean.shape == (batch, n_blocks, a)
    assert n_heads % ATTN_HEADS == 0
    hw = ATTN_HEADS * hd
    steps = n_heads // ATTN_HEADS
    return pl.pallas_call(
        functools.partial(_moba_kernel, n_blocks=n_blocks, hd=hd),
        grid=(batch, steps),
        in_specs=[
            pl.BlockSpec((hw, seq), lambda b, h: (h, b)),
            pl.BlockSpec((seq, hw), lambda b, h: (b, h)),
            pl.BlockSpec((hw, seq), lambda b, h: (steps + h, b)),
            pl.BlockSpec((None, n_blocks, hw), lambda b, h: (b, 0, h)),
        ],
        out_specs=pl.BlockSpec((seq, hw), lambda b, h: (b, h)),
        out_shape=jax.ShapeDtypeStruct((t, a), BF16),
        scratch_shapes=[
            pltpu.VMEM((ATTN_HEADS, 2, n_blocks, MOBA_BLOCK, MOBA_BLOCK), F32),
            pltpu.VMEM((ATTN_HEADS, hd + BF16_SUBLANES, seq), BF16),
        ],
        compiler_params=pltpu.CompilerParams(
            dimension_semantics=("parallel", "parallel"), vmem_limit_bytes=VMEM_LIMIT),
        name="moba_attn",
    )(qvt, k, qvt, kmean)


def _tail_kernel(x_ref, a_ref, win_hbm, bg_ref, cw_ref, cb_ref, wpa_hbm, wpc_hbm, wo_hbm,
                 g_ref, b_ref, o_ref, u_scr, c_scr, mix_scr, wc_ref, wg_ref, wpa_ref, wpc_ref,
                 wo_ref, c_stage, g_stage, p_stage, sems, *, alpha, tiles_per_seq, layer):
    tm, d = x_ref.shape
    a = a_ref.shape[1]
    i = pl.program_id(0)

    @pl.when(i == 0)
    def _():
        n_conv, n_gate = wc_ref.shape[1], wg_ref.shape[1]
        conv_cols = win_hbm.at[layer, pl.ds(0, d), pl.ds(3 * a, n_conv)]
        gate_cols = win_hbm.at[layer, pl.ds(0, d), pl.ds(3 * a + n_conv, n_gate)]
        square = [pair for src, dst in ((wpa_hbm, wpa_ref), (wpc_hbm, wpc_ref), (wo_hbm, wo_ref))
                  for pair in _row_chunks(src.at[layer], dst, p_stage.shape[1])]
        _load_as_bf16([
            (_row_chunks(conv_cols, wc_ref, c_stage.shape[1]), c_stage, sems.at[0]),
            (_row_chunks(gate_cols, wg_ref, g_stage.shape[1]), g_stage, sems.at[1]),
            (square, p_stage, sems.at[2])])

    @pl.when(i % tiles_per_seq == 0)
    def _():
        u_scr[0:CONV_HALO, :] = jnp.zeros((CONV_HALO, d), F32)

    @pl.when(i % tiles_per_seq != 0)
    def _():
        u_scr[0:CONV_HALO, :] = u_scr[tm:tm + CONV_HALO, :]

    for r0 in range(0, tm, SUB_TILE):
        rows = slice(r0, r0 + SUB_TILE)
        x = x_ref[rows, :]
        xb = x.astype(BF16)
        for c0 in range(0, d, COL_CHUNK):
            cs = slice(c0, c0 + COL_CHUNK)
            h = _dot(xb, wc_ref[:, c0:c0 + COL_CHUNK])
            g_b = _dot(xb, wc_ref[:, d + c0:d + c0 + COL_CHUNK])
            g_c = _dot(xb, wc_ref[:, 2 * d + c0:2 * d + c0 + COL_CHUNK])
            u = g_c * h
            top = CONV_HALO + r0
            u_scr[top:top + SUB_TILE, cs] = u
            u1 = u_scr[top - 1:top - 1 + SUB_TILE, cs]
            u2 = u_scr[top - 2:top - 2 + SUB_TILE, cs]
            y = (cw_ref[0:1, cs] * u2 + cw_ref[1:2, cs] * u1 + cw_ref[2:3, cs] * u
                 + cb_ref[:, cs])
            c_scr[rows, cs] = (g_b * y).astype(BF16)

        att = a_ref[rows, :]
        cc = c_scr[rows, :]
        for c0 in range(0, d, COL_CHUNK):
            cs = slice(c0, c0 + COL_CHUNK)
            y_attn = _dot(att, wpa_ref[:, cs])
            y_conv = _dot(cc, wpc_ref[:, cs])
            gate_a = jax.nn.sigmoid(_dot(xb, wg_ref[:, c0:c0 + COL_CHUNK]) + bg_ref[:, cs])
            gate_c = jax.nn.sigmoid(_dot(xb, wg_ref[:, d + c0:d + c0 + COL_CHUNK])
                                    + bg_ref[:, d + c0:d + c0 + COL_CHUNK])
            mix_scr[rows, cs] = (gate_a * y_attn + gate_c * y_conv).astype(BF16)

        mixed = _dot(mix_scr[rows, :], wo_ref[...])
        o_ref[rows, :] = _layer_norm(alpha * x + mixed, g_ref[...], b_ref[...])


def _mixer_tail(x, attn, w_in, b_gate, conv_w, conv_b, w_pa, w_pc, w_out, g, b, layer, alpha, seq):
    t, d = x.shape
    a = attn.shape[1]
    assert t % TAIL_TILE == 0 and seq % TAIL_TILE == 0 and d % COL_CHUNK == 0
    assert TAIL_TILE % SUB_TILE == 0 and d % TAIL_WEIGHT_CHUNKS == 0
    assert w_pa.shape[1:] == (a, d) and a == d
    assert w_pc.shape[1:] == (d, d) and w_out.shape[1:] == (d, d)
    assert conv_w.shape[1:] == (CONV_K, d) and w_in.shape[2] == 3 * a + 5 * d
    tile = lambda i: (i, 0)
    lyr = (layer, 0, 0)
    hbm = pl.BlockSpec(memory_space=pl.ANY)
    return pl.pallas_call(
        functools.partial(_tail_kernel, alpha=alpha, tiles_per_seq=seq // TAIL_TILE, layer=layer),
        grid=(t // TAIL_TILE,),
        in_specs=[
            pl.BlockSpec((TAIL_TILE, d), tile),
            pl.BlockSpec((TAIL_TILE, a), tile),
            hbm,
            _resident((None, 1, 2 * d), lyr),
            _resident((None, CONV_K, d), lyr),
            _resident((None, 1, d), lyr),
            hbm, hbm, hbm,
            _resident((None, 1, d), lyr),
            _resident((None, 1, d), lyr),
        ],
        out_specs=pl.BlockSpec((TAIL_TILE, d), tile),
        out_shape=jax.ShapeDtypeStruct((t, d), F32),
        scratch_shapes=[
            pltpu.VMEM((TAIL_TILE + CONV_HALO, d), F32),
            pltpu.VMEM((TAIL_TILE, d), BF16),
            pltpu.VMEM((TAIL_TILE, d), BF16),
            pltpu.VMEM((d, 3 * d), BF16),
            pltpu.VMEM((d, 2 * d), BF16),
            pltpu.VMEM((a, d), BF16),
            pltpu.VMEM((d, d), BF16),
            pltpu.VMEM((d, d), BF16),
            pltpu.VMEM((WEIGHT_SLOTS, d // TAIL_WEIGHT_CHUNKS, 3 * d), F32),
            pltpu.VMEM((WEIGHT_SLOTS, d // TAIL_WEIGHT_CHUNKS, 2 * d), F32),
            pltpu.VMEM((WEIGHT_SLOTS, 2 * d // TAIL_WEIGHT_CHUNKS, d), F32),
            pltpu.SemaphoreType.DMA((3, WEIGHT_SLOTS)),
        ],
        compiler_params=pltpu.CompilerParams(
            dimension_semantics=("arbitrary",), vmem_limit_bytes=VMEM_LIMIT),
        name="mixer_tail",
    )(x, attn, w_in, b_gate, conv_w, conv_b, w_pa, w_pc, w_out, g, b)


def kernel(x, ffn1_w_up, ffn1_w_down, ln1_g, ln1_b, w_in, b_gate, conv_w, conv_b,
           w_proj_attn, w_proj_conv, w_out, ln2_g, ln2_b, ffn2_w_up, ffn2_w_down,
           ln3_g, ln3_b):
    batch, seq, d = x.shape
    depth = w_in.shape[0]
    a = N_HEADS * HEAD_DIM
    assert w_in.shape[2] == 3 * a + 3 * d + 2 * d and w_proj_attn.shape[1] == a
    alpha = (2.0 * depth) ** 0.25
    rows = lambda v: v.reshape(depth, 1, -1)

    h = x.reshape(batch * seq, d)
    for l in range(depth):
        h = _ffn_ln(h, ffn1_w_up, ffn1_w_down, rows(ln1_g), rows(ln1_b), l, alpha)
        k, qvt, kmean = _qkv_proj(h, w_in, a, l)
        attn = _moba_attention(k, qvt, kmean.reshape(batch, seq // MOBA_BLOCK, a), batch, seq)
        h = _mixer_tail(h, attn, w_in, rows(b_gate), conv_w, rows(conv_b), w_proj_attn,
                        w_proj_conv, w_out, rows(ln2_g), rows(ln2_b), l, alpha, seq)
        h = _ffn_ln(h, ffn2_w_up, ffn2_w_down, rows(ln3_g), rows(ln3_b), l, alpha)
    return h.reshape(batch, seq, d)
```

```python
import functools
import math

import jax
import jax.numpy as jnp
from jax import lax
from jax.experimental import pallas as pl
from jax.experimental.pallas import tpu as pltpu

N_HEADS = 8
HEAD_DIM = 128
MOBA_BLOCK = 256
MOBA_TOPK = 3
CONV_K = 3
LN_EPS = 1e-5

F32 = jnp.float32
BF16 = jnp.bfloat16

TOKEN_TILE = 1024
FFN_TILE = 1024
TAIL_TILE = 512
SUB_TILE = 256
FF_CHUNK = 256
COL_CHUNK = 256
CONV_HALO = 8
BF16_SUBLANES = 16
UP_CHUNKS = 8
TAIL_WEIGHT_CHUNKS = 8
WEIGHT_SLOTS = 3
PV_LAG = 4
ATTN_HEADS = 2
VMEM_LIMIT = 56 * 1024 * 1024


def _resident(block, index):
    return pl.BlockSpec(block, lambda *_: index, pipeline_mode=pl.Buffered(1))


def _layer_norm(y, g, b):
    mu = jnp.mean(y, axis=-1, keepdims=True)
    d = y - mu
    var = jnp.mean(d * d, axis=-1, keepdims=True)
    return d * lax.rsqrt(var + LN_EPS) * g + b


def _dot(a, b):
    return jnp.dot(a, b, preferred_element_type=F32)


def _row_chunks(src_hbm, dst_scr, rows):
    assert src_hbm.shape == dst_scr.shape and src_hbm.shape[0] % rows == 0
    return [(src_hbm.at[pl.ds(r0, rows), :], dst_scr.at[pl.ds(r0, rows), :])
            for r0 in range(0, src_hbm.shape[0], rows)]


def _load_as_bf16(jobs):
    def copy(job, j):
        chunks, stage, sem = job
        slot = j % stage.shape[0]
        assert chunks[j][0].shape == stage.shape[1:] and stage.shape[1] % BF16_SUBLANES == 0
        return pltpu.make_async_copy(chunks[j][0], stage.at[slot], sem.at[slot])

    for job in jobs:
        for j in range(min(job[1].shape[0], len(job[0]))):
            copy(job, j).start(priority=j % 2)
    for j in range(max(len(job[0]) for job in jobs)):
        for job in jobs:
            chunks, stage, _ = job
            if j < len(chunks):
                copy(job, j).wait()
                chunks[j][1][...] = stage[j % stage.shape[0]].astype(BF16)
                nxt = j + stage.shape[0]
                if nxt < len(chunks):
                    copy(job, nxt).start(priority=nxt % 2)


def _ffn_ln_kernel(x_ref, wup_hbm, wdn_hbm, g_ref, b_ref, o_ref, wup_ref, wdn_ref,
                   up_stage, dn_stage, sems, *, alpha, d_ff, layer):
    @pl.when(pl.program_id(0) == 0)
    def _():
        _load_as_bf16([
            (_row_chunks(wup_hbm.at[layer], wup_ref, up_stage.shape[1]), up_stage, sems.at[0]),
            (_row_chunks(wdn_hbm.at[layer], wdn_ref, dn_stage.shape[1]), dn_stage, sems.at[1])])

    chunks = list(range(0, d_ff, FF_CHUNK))
    for r0 in range(0, x_ref.shape[0], SUB_TILE):
        x = x_ref[r0:r0 + SUB_TILE, :]
        xb = x.astype(BF16)

        def gate_up(c0):
            return (_dot(xb, wup_ref[:, c0:c0 + FF_CHUNK]),
                    _dot(xb, wup_ref[:, d_ff + c0:d_ff + c0 + FF_CHUNK]))

        pending = gate_up(chunks[0])
        acc = None
        for j, c0 in enumerate(chunks):
            gate, up = pending
            if j + 1 < len(chunks):
                pending = gate_up(chunks[j + 1])
            h = (gate * jax.nn.sigmoid(gate) * up).astype(BF16)
            down = _dot(h, wdn_ref[c0:c0 + FF_CHUNK, :])
            acc = down if acc is None else acc + down
        o_ref[r0:r0 + SUB_TILE, :] = _layer_norm(alpha * x + 0.5 * acc, g_ref[...], b_ref[...])


def _ffn_ln(x, w_up, w_down, g, b, layer, alpha):
    t, d = x.shape
    d_ff = w_down.shape[1]
    assert t % FFN_TILE == 0 and FFN_TILE % SUB_TILE == 0 and d_ff % FF_CHUNK == 0
    up_rows, dn_rows = d // UP_CHUNKS, 2 * d_ff // UP_CHUNKS
    assert d % UP_CHUNKS == 0 and d_ff % dn_rows == 0
    return pl.pallas_call(
        functools.partial(_ffn_ln_kernel, alpha=alpha, d_ff=d_ff, layer=layer),
        grid=(t // FFN_TILE,),
        in_specs=[
            pl.BlockSpec((FFN_TILE, d), lambda i: (i, 0)),
            pl.BlockSpec(memory_space=pl.ANY),
            pl.BlockSpec(memory_space=pl.ANY),
            _resident((None, 1, d), (layer, 0, 0)),
            _resident((None, 1, d), (layer, 0, 0)),
        ],
        out_specs=pl.BlockSpec((FFN_TILE, d), lambda i: (i, 0)),
        out_shape=jax.ShapeDtypeStruct((t, d), F32),
        scratch_shapes=[
            pltpu.VMEM((d, 2 * d_ff), BF16),
            pltpu.VMEM((d_ff, d), BF16),
            pltpu.VMEM((WEIGHT_SLOTS, up_rows, 2 * d_ff), F32),
            pltpu.VMEM((WEIGHT_SLOTS, dn_rows, d), F32),
            pltpu.SemaphoreType.DMA((2, WEIGHT_SLOTS)),
        ],
        compiler_params=pltpu.CompilerParams(
            dimension_semantics=("arbitrary",), vmem_limit_bytes=VMEM_LIMIT),
        name="ffn_ln",
    )(x, w_up, w_down, g, b)


def _qkv_kernel(x_ref, wq_ref, wk_ref, wv_ref, k_ref, qvt_ref, kmean_ref, wqvt_scr, *, q_scale):
    a = k_ref.shape[1]

    @pl.when(pl.program_id(0) == 0)
    def _():
        for c0 in range(0, wq_ref.shape[0], COL_CHUNK):
            wqvt_scr[0:a, c0:c0 + COL_CHUNK] = wq_ref[c0:c0 + COL_CHUNK, :].T.astype(BF16)
            wqvt_scr[a:, c0:c0 + COL_CHUNK] = wv_ref[c0:c0 + COL_CHUNK, :].T.astype(BF16)

    for r0 in range(0, x_ref.shape[0], SUB_TILE):
        rows = slice(r0, r0 + SUB_TILE)
        xb = x_ref[rows, :].astype(BF16)
        k = _dot(xb, wk_ref[...])
        k_ref[rows, :] = k.astype(BF16)
        for j in range(SUB_TILE // MOBA_BLOCK):
            blk = k[j * MOBA_BLOCK:(j + 1) * MOBA_BLOCK]
            row = r0 // MOBA_BLOCK + j
            kmean_ref[0, row:row + 1, :] = jnp.sum(blk, axis=0, keepdims=True) * (1.0 / MOBA_BLOCK)
        qvt = lax.dot_general(wqvt_scr[...], xb, (((1,), (1,)), ((), ())),
                              preferred_element_type=F32)
        qvt_ref[0:a, rows] = (qvt[0:a] * q_scale).astype(BF16)
        qvt_ref[a:, rows] = qvt[a:].astype(BF16)


def _qkv_proj(x, w_in, a, layer):
    t, d = x.shape
    q_scale = (HEAD_DIM ** -0.5) * math.log2(math.e)
    assert t % TOKEN_TILE == 0 and TOKEN_TILE % SUB_TILE == 0 and SUB_TILE % MOBA_BLOCK == 0
    assert d % COL_CHUNK == 0
    n_tiles, blocks_per_tile = t // TOKEN_TILE, TOKEN_TILE // MOBA_BLOCK
    return pl.pallas_call(
        functools.partial(_qkv_kernel, q_scale=q_scale),
        grid=(n_tiles,),
        in_specs=[
            pl.BlockSpec((TOKEN_TILE, d), lambda i: (i, 0)),
            _resident((None, d, a), (layer, 0, 0)),
            _resident((None, d, a), (layer, 0, 1)),
            _resident((None, d, a), (layer, 0, 2)),
        ],
        out_specs=[
            pl.BlockSpec((TOKEN_TILE, a), lambda i: (i, 0)),
            pl.BlockSpec((2 * a, TOKEN_TILE), lambda i: (0, i)),
            pl.BlockSpec((1, blocks_per_tile, a), lambda i: (i, 0, 0)),
        ],
        out_shape=[
            jax.ShapeDtypeStruct((t, a), BF16),
            jax.ShapeDtypeStruct((2 * a, t), BF16),
            jax.ShapeDtypeStruct((n_tiles, blocks_per_tile, a), F32),
        ],
        scratch_shapes=[pltpu.VMEM((2 * a, d), BF16)],
        compiler_params=pltpu.CompilerParams(
            dimension_semantics=("arbitrary",), vmem_limit_bytes=VMEM_LIMIT),
        name="qkv_proj",
    )(x, w_in, w_in, w_in)


def _moba_kernel(qt_ref, k_ref, vt_ref, km_ref, o_ref, s_scr, vt1_scr, *, n_blocks, hd):
    blk = MOBA_BLOCK
    n_heads = qt_ref.shape[0] // hd
    blk_id = lax.broadcasted_iota(jnp.int32, (n_blocks, blk), 0)
    key_pos = lax.broadcasted_iota(jnp.int32, (blk, blk), 0)
    qry_pos = lax.broadcasted_iota(jnp.int32, (blk, blk), 1)
    causal = key_pos <= qry_pos

    def q_tile(h, qb):
        return qt_ref[h * hd:(h + 1) * hd, qb * blk:(qb + 1) * blk]

    def choice_bias(h, qb):
        km = km_ref[:, h * hd:(h + 1) * hd]
        km_hi = km.astype(BF16).astype(F32)
        km_split = jnp.concatenate([km_hi, km - km_hi], axis=0).astype(BF16)
        g2 = _dot(km_split, q_tile(h, qb))
        gate = g2[:n_blocks] + g2[n_blocks:]
        past = blk_id < qb
        gate = jnp.where(past, gate, -jnp.inf)
        beaten = jnp.zeros(gate.shape, jnp.int32)
        for m in range(qb):
            gm = gate[m:m + 1, :]
            wins = (gm > gate) | ((gm == gate) & (blk_id > m))
            beaten = beaten + wins.astype(jnp.int32)
        return jnp.where(past & (beaten < MOBA_TOPK), 0.0, -jnp.inf)

    biases = {(h, qb): choice_bias(h, qb)
              for h in range(n_heads) for qb in range(MOBA_TOPK + 1, n_blocks)}

    ones_rows = vt1_scr.shape[1] - hd
    for h in range(n_heads):
        vt1_scr[h, 0:hd, :] = vt_ref[h * hd:(h + 1) * hd, :]
        vt1_scr[h, hd:, :] = jnp.ones((ones_rows, vt1_scr.shape[2]), BF16)

    def slots(group):
        base, out = 0, []
        for qb in group:
            out.append(base)
            base += qb + 1
        assert base <= s_scr.shape[2]
        return out

    def items(group):
        per_head = [(qb, base + n, n) for qb, base in zip(group, slots(group))
                    for n in range(qb + 1)]
        return [(h,) + it for it in per_head for h in range(n_heads)]

    col_max8 = {}
    acc = {}

    def score_block(half, h, qb, slot, n):
        s = _dot(k_ref[n * blk:(n + 1) * blk, h * hd:(h + 1) * hd], q_tile(h, qb))
        if n == qb:
            s = jnp.where(causal, s, -jnp.inf)
        elif (h, qb) in biases:
            s = s + biases[h, qb][n:n + 1, :]
        s_scr[h, half, slot] = s
        mx = jnp.max(s.reshape(blk // 8, 8, blk), axis=0)
        col_max8[h, qb] = mx if n == 0 else jnp.maximum(col_max8[h, qb], mx)

    def weights_block(half, h, qb, slot, n):
        if n == 0:
            m_run = jnp.max(col_max8[h, qb], axis=0, keepdims=True)
            col_max8[h, qb] = jnp.broadcast_to(m_run, (8, blk))
        p = jnp.exp2(s_scr[h, half, slot].reshape(blk // 8, 8, blk) - col_max8[h, qb])
        return p.reshape(blk, blk).astype(BF16)

    def pv_block(p, h, qb, n):
        pv = _dot(vt1_scr[h, :, n * blk:(n + 1) * blk], p)
        acc[h, qb] = pv if n == 0 else acc[h, qb] + pv
        if n == qb:
            a = acc.pop((h, qb))
            out = (a[:hd] * (1.0 / a[hd:hd + 1])).T
            o_ref[qb * blk:(qb + 1) * blk, h * hd:(h + 1) * hd] = out.astype(o_ref.dtype)

    first = MOBA_TOPK
    rest = [qb for qb in range(n_blocks - 1, -1, -1) if qb != first]
    groups = [[first]]
    while rest:
        big = rest.pop(0)
        small = [qb for qb in rest if big + qb + 2 <= n_blocks][:1]
        for qb in small:
            rest.remove(qb)
        groups.append([big] + small)
    for item in items(groups[0]):
        score_block(0, *item)
    ready = []
    for j, group in enumerate(groups):
        cur = items(group)
        nxt = items(groups[j + 1]) if j + 1 < len(groups) else []
        for t in range(max(len(cur), len(nxt))):
            if t < len(nxt):
                score_block((j + 1) % 2, *nxt[t])
            if len(ready) > PV_LAG or (ready and t >= len(cur)):
                pv_block(*ready.pop(0))
            if t < len(cur):
                h, qb, slot, n = cur[t]
                ready.append((weights_block(j % 2, h, qb, slot, n), h, qb, n))
    while ready:
        pv_block(*ready.pop(0))


def _moba_attention(k, qvt, kmean, batch, seq):
    t, a = k.shape
    hd = HEAD_DIM
    n_heads = a // hd
    n_blocks = seq // MOBA_BLOCK
    assert seq % MOBA_BLOCK == 0 and kmean.shape == (batch, n_blocks, a)
    assert n_heads % ATTN_HEADS == 0
    hw = ATTN_HEADS * hd
    steps = n_heads // ATTN_HEADS
    return pl.pallas_call(
        functools.partial(_moba_kernel, n_blocks=n_blocks, hd=hd),
        grid=(batch, steps),
        in_specs=[
            pl.BlockSpec((hw, seq), lambda b, h: (h, b)),
            pl.BlockSpec((seq, hw), lambda b, h: (b, h)),
            pl.BlockSpec((hw, seq), lambda b, h: (steps + h, b)),
            pl.BlockSpec((None, n_blocks, hw), lambda b, h: (b, 0, h)),
        ],
        out_specs=pl.BlockSpec((seq, hw), lambda b, h: (b, h)),
        out_shape=jax.ShapeDtypeStruct((t, a), BF16),
        scratch_shapes=[
            pltpu.VMEM((ATTN_HEADS, 2, n_blocks, MOBA_BLOCK, MOBA_BLOCK), F32),
            pltpu.VMEM((ATTN_HEADS, hd + BF16_SUBLANES, seq), BF16),
        ],
        compiler_params=pltpu.CompilerParams(
            dimension_semantics=("parallel", "parallel"), vmem_limit_bytes=VMEM_LIMIT),
        name="moba_attn",
    )(qvt, k, qvt, kmean)


def _tail_kernel(x_ref, a_ref, win_hbm, bg_ref, cw_ref, cb_ref, wpa_hbm, wpc_hbm, wo_hbm,
                 g_ref, b_ref, o_ref, u_scr, c_scr, mix_scr, wc_ref, wg_ref, wpa_ref, wpc_ref,
                 wo_ref, c_stage, g_stage, p_stage, sems, *, alpha, tiles_per_seq, layer):
    tm, d = x_ref.shape
    a = a_ref.shape[1]
    i = pl.program_id(0)

    @pl.when(i == 0)
    def _():
        n_conv, n_gate = wc_ref.shape[1], wg_ref.shape[1]
        conv_cols = win_hbm.at[layer, pl.ds(0, d), pl.ds(3 * a, n_conv)]
        gate_cols = win_hbm.at[layer, pl.ds(0, d), pl.ds(3 * a + n_conv, n_gate)]
        square = [pair for src, dst in ((wpa_hbm, wpa_ref), (wpc_hbm, wpc_ref), (wo_hbm, wo_ref))
                  for pair in _row_chunks(src.at[layer], dst, p_stage.shape[1])]
        _load_as_bf16([
            (_row_chunks(conv_cols, wc_ref, c_stage.shape[1]), c_stage, sems.at[0]),
            (_row_chunks(gate_cols, wg_ref, g_stage.shape[1]), g_stage, sems.at[1]),
            (square, p_stage, sems.at[2])])

    @pl.when(i % tiles_per_seq == 0)
    def _():
        u_scr[0:CONV_HALO, :] = jnp.zeros((CONV_HALO, d), F32)

    @pl.when(i % tiles_per_seq != 0)
    def _():
        u_scr[0:CONV_HALO, :] = u_scr[tm:tm + CONV_HALO, :]

    for r0 in range(0, tm, SUB_TILE):
        rows = slice(r0, r0 + SUB_TILE)
        x = x_ref[rows, :]
        xb = x.astype(BF16)
        for c0 in range(0, d, COL_CHUNK):
            cs = slice(c0, c0 + COL_CHUNK)
            h = _dot(xb, wc_ref[:, c0:c0 + COL_CHUNK])
            g_b = _dot(xb, wc_ref[:, d + c0:d + c0 + COL_CHUNK])
            g_c = _dot(xb, wc_ref[:, 2 * d + c0:2 * d + c0 + COL_CHUNK])
            u = g_c * h
            top = CONV_HALO + r0
            u_scr[top:top + SUB_TILE, cs] = u
            u1 = u_scr[top - 1:top - 1 + SUB_TILE, cs]
            u2 = u_scr[top - 2:top - 2 + SUB_TILE, cs]
            y = (cw_ref[0:1, cs] * u2 + cw_ref[1:2, cs] * u1 + cw_ref[2:3, cs] * u
                 + cb_ref[:, cs])
            c_scr[rows, cs] = (g_b * y).astype(BF16)

        att = a_ref[rows, :]
        cc = c_scr[rows, :]
        for c0 in range(0, d, COL_CHUNK):
            cs = slice(c0, c0 + COL_CHUNK)
            y_attn = _dot(att, wpa_ref[:, cs])
            y_conv = _dot(cc, wpc_ref[:, cs])
            gate_a = jax.nn.sigmoid(_dot(xb, wg_ref[:, c0:c0 + COL_CHUNK]) + bg_ref[:, cs])
            gate_c = jax.nn.sigmoid(_dot(xb, wg_ref[:, d + c0:d + c0 + COL_CHUNK])
                                    + bg_ref[:, d + c0:d + c0 + COL_CHUNK])
            mix_scr[rows, cs] = (gate_a * y_attn + gate_c * y_conv).astype(BF16)

        mixed = _dot(mix_scr[rows, :], wo_ref[...])
        o_ref[rows, :] = _layer_norm(alpha * x + mixed, g_ref[...], b_ref[...])


def _mixer_tail(x, attn, w_in, b_gate, conv_w, conv_b, w_pa, w_pc, w_out, g, b, layer, alpha, seq):
    t, d = x.shape
    a = attn.shape[1]
    assert t % TAIL_TILE == 0 and seq % TAIL_TILE == 0 and d % COL_CHUNK == 0
    assert TAIL_TILE % SUB_TILE == 0 and d % TAIL_WEIGHT_CHUNKS == 0
    assert w_pa.shape[1:] == (a, d) and a == d
    assert w_pc.shape[1:] == (d, d) and w_out.shape[1:] == (d, d)
    assert conv_w.shape[1:] == (CONV_K, d) and w_in.shape[2] == 3 * a + 5 * d
    tile = lambda i: (i, 0)
    lyr = (layer, 0, 0)
    hbm = pl.BlockSpec(memory_space=pl.ANY)
    return pl.pallas_call(
        functools.partial(_tail_kernel, alpha=alpha, tiles_per_seq=seq // TAIL_TILE, layer=layer),
        grid=(t // TAIL_TILE,),
        in_specs=[
            pl.BlockSpec((TAIL_TILE, d), tile),
            pl.BlockSpec((TAIL_TILE, a), tile),
            hbm,
            _resident((None, 1, 2 * d), lyr),
            _resident((None, CONV_K, d), lyr),
            _resident((None, 1, d), lyr),
            hbm, hbm, hbm,
            _resident((None, 1, d), lyr),
            _resident((None, 1, d), lyr),
        ],
        out_specs=pl.BlockSpec((TAIL_TILE, d), tile),
        out_shape=jax.ShapeDtypeStruct((t, d), F32),
        scratch_shapes=[
            pltpu.VMEM((TAIL_TILE + CONV_HALO, d), F32),
            pltpu.VMEM((TAIL_TILE, d), BF16),
            pltpu.VMEM((TAIL_TILE, d), BF16),
            pltpu.VMEM((d, 3 * d), BF16),
            pltpu.VMEM((d, 2 * d), BF16),
            pltpu.VMEM((a, d), BF16),
            pltpu.VMEM((d, d), BF16),
            pltpu.VMEM((d, d), BF16),
            pltpu.VMEM((WEIGHT_SLOTS, d // TAIL_WEIGHT_CHUNKS, 3 * d), F32),
            pltpu.VMEM((WEIGHT_SLOTS, d // TAIL_WEIGHT_CHUNKS, 2 * d), F32),
            pltpu.VMEM((WEIGHT_SLOTS, 2 * d // TAIL_WEIGHT_CHUNKS, d), F32),
            pltpu.SemaphoreType.DMA((3, WEIGHT_SLOTS)),
        ],
        compiler_params=pltpu.CompilerParams(
            dimension_semantics=("arbitrary",), vmem_limit_bytes=VMEM_LIMIT),
        name="mixer_tail",
    )(x, attn, w_in, b_gate, conv_w, conv_b, w_pa, w_pc, w_out, g, b)


def kernel(x, ffn1_w_up, ffn1_w_down, ln1_g, ln1_b, w_in, b_gate, conv_w, conv_b,
           w_proj_attn, w_proj_conv, w_out, ln2_g, ln2_b, ffn2_w_up, ffn2_w_down,
           ln3_g, ln3_b):
    batch, seq, d = x.shape
    depth = w_in.shape[0]
    a = N_HEADS * HEAD_DIM
    assert w_in.shape[2] == 3 * a + 3 * d + 2 * d and w_proj_attn.shape[1] == a
    alpha = (2.0 * depth) ** 0.25
    rows = lambda v: v.reshape(depth, 1, -1)

    h = x.reshape(batch * seq, d)
    for l in range(depth):
        h = _ffn_ln(h, ffn1_w_up, ffn1_w_down, rows(ln1_g), rows(ln1_b), l, alpha)
        k, qvt, kmean = _qkv_proj(h, w_in, a, l)
        attn = _moba_attention(k, qvt, kmean.reshape(batch, seq // MOBA_BLOCK, a), batch, seq)
        h = _mixer_tail(h, attn, w_in, rows(b_gate), conv_w, rows(conv_b), w_proj_attn,
                        w_proj_conv, w_out, rows(ln2_g), rows(ln2_b), l, alpha, seq)
        h = _ffn_ln(h, ffn2_w_up, ffn2_w_down, rows(ln3_g), rows(ln3_b), l, alpha)
    return h.reshape(batch, seq, d)
```

```python
import functools
import math

import jax
import jax.numpy as jnp
from jax import lax
from jax.experimental import pallas as pl
from jax.experimental.pallas import tpu as pltpu

N_HEADS = 8
HEAD_DIM = 128
MOBA_BLOCK = 256
MOBA_TOPK = 3
CONV_K = 3
LN_EPS = 1e-5

F32 = jnp.float32
BF16 = jnp.bfloat16

TOKEN_TILE = 1024
FFN_TILE = 1024
TAIL_TILE = 512
SUB_TILE = 256
FF_CHUNK = 256
COL_CHUNK = 256
CONV_HALO = 8
BF16_SUBLANES = 16
UP_CHUNKS = 8
TAIL_WEIGHT_CHUNKS = 8
WEIGHT_SLOTS = 3
PV_LAG = 4
ATTN_HEADS = 4
VMEM_LIMIT = 56 * 1024 * 1024


def _resident(block, index):
    return pl.BlockSpec(block, lambda *_: index, pipeline_mode=pl.Buffered(1))


def _layer_norm(y, g, b):
    mu = jnp.mean(y, axis=-1, keepdims=True)
    d = y - mu
    var = jnp.mean(d * d, axis=-1, keepdims=True)
    return d * lax.rsqrt(var + LN_EPS) * g + b


def _dot(a, b):
    return jnp.dot(a, b, preferred_element_type=F32)


def _row_chunks(src_hbm, dst_scr, rows):
    assert src_hbm.shape == dst_scr.shape and src_hbm.shape[0] % rows == 0
    return [(src_hbm.at[pl.ds(r0, rows), :], dst_scr.at[pl.ds(r0, rows), :])
            for r0 in range(0, src_hbm.shape[0], rows)]


def _load_as_bf16(jobs):
    def copy(job, j):
        chunks, stage, sem = job
        slot = j % stage.shape[0]
        assert chunks[j][0].shape == stage.shape[1:] and stage.shape[1] % BF16_SUBLANES == 0
        return pltpu.make_async_copy(chunks[j][0], stage.at[slot], sem.at[slot])

    for job in jobs:
        for j in range(min(job[1].shape[0], len(job[0]))):
            copy(job, j).start(priority=j % 2)
    for j in range(max(len(job[0]) for job in jobs)):
        for job in jobs:
            chunks, stage, _ = job
            if j < len(chunks):
                copy(job, j).wait()
                chunks[j][1][...] = stage[j % stage.shape[0]].astype(BF16)
                nxt = j + stage.shape[0]
                if nxt < len(chunks):
                    copy(job, nxt).start(priority=nxt % 2)


def _ffn_ln_kernel(x_ref, wup_hbm, wdn_hbm, g_ref, b_ref, o_ref, wup_ref, wdn_ref,
                   up_stage, dn_stage, sems, *, alpha, d_ff, layer):
    @pl.when(pl.program_id(0) == 0)
    def _():
        _load_as_bf16([
            (_row_chunks(wup_hbm.at[layer], wup_ref, up_stage.shape[1]), up_stage, sems.at[0]),
            (_row_chunks(wdn_hbm.at[layer], wdn_ref, dn_stage.shape[1]), dn_stage, sems.at[1])])

    chunks = list(range(0, d_ff, FF_CHUNK))
    for r0 in range(0, x_ref.shape[0], SUB_TILE):
        x = x_ref[r0:r0 + SUB_TILE, :]
        xb = x.astype(BF16)

        def gate_up(c0):
            return (_dot(xb, wup_ref[:, c0:c0 + FF_CHUNK]),
                    _dot(xb, wup_ref[:, d_ff + c0:d_ff + c0 + FF_CHUNK]))

        pending = gate_up(chunks[0])
        acc = None
        for j, c0 in enumerate(chunks):
            gate, up = pending
            if j + 1 < len(chunks):
                pending = gate_up(chunks[j + 1])
            h = (gate * jax.nn.sigmoid(gate) * up).astype(BF16)
            down = _dot(h, wdn_ref[c0:c0 + FF_CHUNK, :])
            acc = down if acc is None else acc + down
        o_ref[r0:r0 + SUB_TILE, :] = _layer_norm(alpha * x + 0.5 * acc, g_ref[...], b_ref[...])


def _ffn_ln(x, w_up, w_down, g, b, layer, alpha):
    t, d = x.shape
    d_ff = w_down.shape[1]
    assert t % FFN_TILE == 0 and FFN_TILE % SUB_TILE == 0 and d_ff % FF_CHUNK == 0
    up_rows, dn_rows = d // UP_CHUNKS, 2 * d_ff // UP_CHUNKS
    assert d % UP_CHUNKS == 0 and d_ff % dn_rows == 0
    return pl.pallas_call(
        functools.partial(_ffn_ln_kernel, alpha=alpha, d_ff=d_ff, layer=layer),
        grid=(t // FFN_TILE,),
        in_specs=[
            pl.BlockSpec((FFN_TILE, d), lambda i: (i, 0)),
            pl.BlockSpec(memory_space=pl.ANY),
            pl.BlockSpec(memory_space=pl.ANY),
            _resident((None, 1, d), (layer, 0, 0)),
            _resident((None, 1, d), (layer, 0, 0)),
        ],
        out_specs=pl.BlockSpec((FFN_TILE, d), lambda i: (i, 0)),
        out_shape=jax.ShapeDtypeStruct((t, d), F32),
        scratch_shapes=[
            pltpu.VMEM((d, 2 * d_ff), BF16),
            pltpu.VMEM((d_ff, d), BF16),
            pltpu.VMEM((WEIGHT_SLOTS, up_rows, 2 * d_ff), F32),
            pltpu.VMEM((WEIGHT_SLOTS, dn_rows, d), F32),
            pltpu.SemaphoreType.DMA((2, WEIGHT_SLOTS)),
        ],
        compiler_params=pltpu.CompilerParams(
            dimension_semantics=("arbitrary",), vmem_limit_bytes=VMEM_LIMIT),
        name="ffn_ln",
    )(x, w_up, w_down, g, b)


def _qkv_kernel(x_ref, wq_ref, wk_ref, wv_ref, k_ref, qvt_ref, kmean_ref, wqvt_scr, *, q_scale):
    a = k_ref.shape[1]

    @pl.when(pl.program_id(0) == 0)
    def _():
        for c0 in range(0, wq_ref.shape[0], COL_CHUNK):
            wqvt_scr[0:a, c0:c0 + COL_CHUNK] = wq_ref[c0:c0 + COL_CHUNK, :].T.astype(BF16)
            wqvt_scr[a:, c0:c0 + COL_CHUNK] = wv_ref[c0:c0 + COL_CHUNK, :].T.astype(BF16)

    for r0 in range(0, x_ref.shape[0], SUB_TILE):
        rows = slice(r0, r0 + SUB_TILE)
        xb = x_ref[rows, :].astype(BF16)
        k = _dot(xb, wk_ref[...])
        k_ref[rows, :] = k.astype(BF16)
        for j in range(SUB_TILE // MOBA_BLOCK):
            blk = k[j * MOBA_BLOCK:(j + 1) * MOBA_BLOCK]
            row = r0 // MOBA_BLOCK + j
            kmean_ref[0, row:row + 1, :] = jnp.sum(blk, axis=0, keepdims=True) * (1.0 / MOBA_BLOCK)
        qvt = lax.dot_general(wqvt_scr[...], xb, (((1,), (1,)), ((), ())),
                              preferred_element_type=F32)
        qvt_ref[0:a, rows] = (qvt[0:a] * q_scale).astype(BF16)
        qvt_ref[a:, rows] = qvt[a:].astype(BF16)


def _qkv_proj(x, w_in, a, layer):
    t, d = x.shape
    q_scale = (HEAD_DIM ** -0.5) * math.log2(math.e)
    assert t % TOKEN_TILE == 0 and TOKEN_TILE % SUB_TILE == 0 and SUB_TILE % MOBA_BLOCK == 0
    assert d % COL_CHUNK == 0
    n_tiles, blocks_per_tile = t // TOKEN_TILE, TOKEN_TILE // MOBA_BLOCK
    return pl.pallas_call(
        functools.partial(_qkv_kernel, q_scale=q_scale),
        grid=(n_tiles,),
        in_specs=[
            pl.BlockSpec((TOKEN_TILE, d), lambda i: (i, 0)),
            _resident((None, d, a), (layer, 0, 0)),
            _resident((None, d, a), (layer, 0, 1)),
            _resident((None, d, a), (layer, 0, 2)),
        ],
        out_specs=[
            pl.BlockSpec((TOKEN_TILE, a), lambda i: (i, 0)),
            pl.BlockSpec((2 * a, TOKEN_TILE), lambda i: (0, i)),
            pl.BlockSpec((1, blocks_per_tile, a), lambda i: (i, 0, 0)),
        ],
        out_shape=[
            jax.ShapeDtypeStruct((t, a), BF16),
            jax.ShapeDtypeStruct((2 * a, t), BF16),
            jax.ShapeDtypeStruct((n_tiles, blocks_per_tile, a), F32),
        ],
        scratch_shapes=[pltpu.VMEM((2 * a, d), BF16)],
        compiler_params=pltpu.CompilerParams(
            dimension_semantics=("arbitrary",), vmem_limit_bytes=VMEM_LIMIT),
        name="qkv_proj",
    )(x, w_in, w_in, w_in)


def _moba_kernel(qt_ref, k_ref, vt_ref, km_ref, o_ref, s_scr, vt1_scr, *, n_blocks, hd):
    blk = MOBA_BLOCK
    n_heads = qt_ref.shape[0] // hd
    blk_id = lax.broadcasted_iota(jnp.int32, (n_blocks, blk), 0)
    key_pos = lax.broadcasted_iota(jnp.int32, (blk, blk), 0)
    qry_pos = lax.broadcasted_iota(jnp.int32, (blk, blk), 1)
    causal = key_pos <= qry_pos

    def q_tile(h, qb):
        return qt_ref[h * hd:(h + 1) * hd, qb * blk:(qb + 1) * blk]

    def choice_bias(h, qb):
        km = km_ref[:, h * hd:(h + 1) * hd]
        km_hi = km.astype(BF16).astype(F32)
        km_split = jnp.concatenate([km_hi, km - km_hi], axis=0).astype(BF16)
        g2 = _dot(km_split, q_tile(h, qb))
        gate = g2[:n_blocks] + g2[n_blocks:]
        past = blk_id < qb
        gate = jnp.where(past, gate, -jnp.inf)
        beaten = jnp.zeros(gate.shape, jnp.int32)
        for m in range(qb):
            gm = gate[m:m + 1, :]
            wins = (gm > gate) | ((gm == gate) & (blk_id > m))
            beaten = beaten + wins.astype(jnp.int32)
        return jnp.where(past & (beaten < MOBA_TOPK), 0.0, -jnp.inf)

    biases = {(h, qb): choice_bias(h, qb)
              for h in range(n_heads) for qb in range(MOBA_TOPK + 1, n_blocks)}

    ones_rows = vt1_scr.shape[1] - hd
    for h in range(n_heads):
        vt1_scr[h, 0:hd, :] = vt_ref[h * hd:(h + 1) * hd, :]
        vt1_scr[h, hd:, :] = jnp.ones((ones_rows, vt1_scr.shape[2]), BF16)

    def slots(group):
        base, out = 0, []
        for qb in group:
            out.append(base)
            base += qb + 1
        assert base <= s_scr.shape[2]
        return out

    def items(group):
        per_head = [(qb, base + n, n) for qb, base in zip(group, slots(group))
                    for n in range(qb + 1)]
        return [(h,) + it for it in per_head for h in range(n_heads)]

    col_max8 = {}
    acc = {}

    def score_block(half, h, qb, slot, n):
        s = _dot(k_ref[n * blk:(n + 1) * blk, h * hd:(h + 1) * hd], q_tile(h, qb))
        if n == qb:
            s = jnp.where(causal, s, -jnp.inf)
        elif (h, qb) in biases:
            s = s + biases[h, qb][n:n + 1, :]
        s_scr[h, half, slot] = s
        mx = jnp.max(s.reshape(blk // 8, 8, blk), axis=0)
        col_max8[h, qb] = mx if n == 0 else jnp.maximum(col_max8[h, qb], mx)

    def weights_block(half, h, qb, slot, n):
        if n == 0:
            m_run = jnp.max(col_max8[h, qb], axis=0, keepdims=True)
            col_max8[h, qb] = jnp.broadcast_to(m_run, (8, blk))
        p = jnp.exp2(s_scr[h, half, slot].reshape(blk // 8, 8, blk) - col_max8[h, qb])
        return p.reshape(blk, blk).astype(BF16)

    def pv_block(p, h, qb, n):
        pv = _dot(vt1_scr[h, :, n * blk:(n + 1) * blk], p)
        acc[h, qb] = pv if n == 0 else acc[h, qb] + pv
        if n == qb:
            a = acc.pop((h, qb))
            out = (a[:hd] * (1.0 / a[hd:hd + 1])).T
            o_ref[qb * blk:(qb + 1) * blk, h * hd:(h + 1) * hd] = out.astype(o_ref.dtype)

    first = MOBA_TOPK
    rest = [qb for qb in range(n_blocks - 1, -1, -1) if qb != first]
    groups = [[first]]
    while rest:
        big = rest.pop(0)
        small = [qb for qb in rest if big + qb + 2 <= n_blocks][:1]
        for qb in small:
            rest.remove(qb)
        groups.append([big] + small)
    for item in items(groups[0]):
        score_block(0, *item)
    ready = []
    for j, group in enumerate(groups):
        cur = items(group)
        nxt = items(groups[j + 1]) if j + 1 < len(groups) else []
        for t in range(max(len(cur), len(nxt))):
            if t < len(nxt):
                score_block((j + 1) % 2, *nxt[t])
            if len(ready) > PV_LAG or (ready and t >= len(cur)):
                pv_block(*ready.pop(0))
            if t < len(cur):
                h, qb, slot, n = cur[t]
                ready.append((weights_block(j % 2, h, qb, slot, n), h, qb, n))
    while ready:
        pv_block(*ready.pop(0))


def _moba_attention(k, qvt, kmean, batch, seq):
    t, a = k.shape
    hd = HEAD_DIM
    n_heads = a // hd
    n_blocks = seq // MOBA_BLOCK
    assert seq % MOBA_BLOCK == 0 and kmean.shape == (batch, n_blocks, a)
    assert n_heads % ATTN_HEADS == 0
    hw = ATTN_HEADS * hd
    steps = n_heads // ATTN_HEADS
    return pl.pallas_call(
        functools.partial(_moba_kernel, n_blocks=n_blocks, hd=hd),
        grid=(batch, steps),
        in_specs=[
            pl.BlockSpec((hw, seq), lambda b, h: (h, b)),
            pl.BlockSpec((seq, hw), lambda b, h: (b, h)),
            pl.BlockSpec((hw, seq), lambda b, h: (steps + h, b)),
            pl.BlockSpec((None, n_blocks, hw), lambda b, h: (b, 0, h)),
        ],
        out_specs=pl.BlockSpec((seq, hw), lambda b, h: (b, h)),
        out_shape=jax.ShapeDtypeStruct((t, a), BF16),
        scratch_shapes=[
            pltpu.VMEM((ATTN_HEADS, 2, n_blocks, MOBA_BLOCK, MOBA_BLOCK), F32),
            pltpu.VMEM((ATTN_HEADS, hd + BF16_SUBLANES, seq), BF16),
        ],
        compiler_params=pltpu.CompilerParams(
            dimension_semantics=("parallel", "parallel"), vmem_limit_bytes=VMEM_LIMIT),
        name="moba_attn",
    )(qvt, k, qvt, kmean)


def _tail_kernel(x_ref, a_ref, win_hbm, bg_ref, cw_ref, cb_ref, wpa_hbm, wpc_hbm, wo_hbm,
                 g_ref, b_ref, o_ref, u_scr, c_scr, mix_scr, wc_ref, wg_ref, wpa_ref, wpc_ref,
                 wo_ref, c_stage, g_stage, p_stage, sems, *, alpha, tiles_per_seq, layer):
    tm, d = x_ref.shape
    a = a_ref.shape[1]
    i = pl.program_id(0)

    @pl.when(i == 0)
    def _():
        n_conv, n_gate = wc_ref.shape[1], wg_ref.shape[1]
        conv_cols = win_hbm.at[layer, pl.ds(0, d), pl.ds(3 * a, n_conv)]
        gate_cols = win_hbm.at[layer, pl.ds(0, d), pl.ds(3 * a + n_conv, n_gate)]
        square = [pair for src, dst in ((wpa_hbm, wpa_ref), (wpc_hbm, wpc_ref), (wo_hbm, wo_ref))
                  for pair in _row_chunks(src.at[layer], dst, p_stage.shape[1])]
        _load_as_bf16([
            (_row_chunks(conv_cols, wc_ref, c_stage.shape[1]), c_stage, sems.at[0]),
            (_row_chunks(gate_cols, wg_ref, g_stage.shape[1]), g_stage, sems.at[1]),
            (square, p_stage, sems.at[2])])

    @pl.when(i % tiles_per_seq == 0)
    def _():
        u_scr[0:CONV_HALO, :] = jnp.zeros((CONV_HALO, d), F32)

    @pl.when(i % tiles_per_seq != 0)
    def _():
        u_scr[0:CONV_HALO, :] = u_scr[tm:tm + CONV_HALO, :]

    for r0 in range(0, tm, SUB_TILE):
        rows = slice(r0, r0 + SUB_TILE)
        x = x_ref[rows, :]
        xb = x.astype(BF16)
        for c0 in range(0, d, COL_CHUNK):
            cs = slice(c0, c0 + COL_CHUNK)
            h = _dot(xb, wc_ref[:, c0:c0 + COL_CHUNK])
            g_b = _dot(xb, wc_ref[:, d + c0:d + c0 + COL_CHUNK])
            g_c = _dot(xb, wc_ref[:, 2 * d + c0:2 * d + c0 + COL_CHUNK])
            u = g_c * h
            top = CONV_HALO + r0
            u_scr[top:top + SUB_TILE, cs] = u
            u1 = u_scr[top - 1:top - 1 + SUB_TILE, cs]
            u2 = u_scr[top - 2:top - 2 + SUB_TILE, cs]
            y = (cw_ref[0:1, cs] * u2 + cw_ref[1:2, cs] * u1 + cw_ref[2:3, cs] * u
                 + cb_ref[:, cs])
            c_scr[rows, cs] = (g_b * y).astype(BF16)

        att = a_ref[rows, :]
        cc = c_scr[rows, :]
        for c0 in range(0, d, COL_CHUNK):
            cs = slice(c0, c0 + COL_CHUNK)
            y_attn = _dot(att, wpa_ref[:, cs])
            y_conv = _dot(cc, wpc_ref[:, cs])
            gate_a = jax.nn.sigmoid(_dot(xb, wg_ref[:, c0:c0 + COL_CHUNK]) + bg_ref[:, cs])
            gate_c = jax.nn.sigmoid(_dot(xb, wg_ref[:, d + c0:d + c0 + COL_CHUNK])
                                    + bg_ref[:, d + c0:d + c0 + COL_CHUNK])
            mix_scr[rows, cs] = (gate_a * y_attn + gate_c * y_conv).astype(BF16)

        mixed = _dot(mix_scr[rows, :], wo_ref[...])
        o_ref[rows, :] = _layer_norm(alpha * x + mixed, g_ref[...], b_ref[...])


def _mixer_tail(x, attn, w_in, b_gate, conv_w, conv_b, w_pa, w_pc, w_out, g, b, layer, alpha, seq):
    t, d = x.shape
    a = attn.shape[1]
    assert t % TAIL_TILE == 0 and seq % TAIL_TILE == 0 and d % COL_CHUNK == 0
    assert TAIL_TILE % SUB_TILE == 0 and d % TAIL_WEIGHT_CHUNKS == 0
    assert w_pa.shape[1:] == (a, d) and a == d
    assert w_pc.shape[1:] == (d, d) and w_out.shape[1:] == (d, d)
    assert conv_w.shape[1:] == (CONV_K, d) and w_in.shape[2] == 3 * a + 5 * d
    tile = lambda i: (i, 0)
    lyr = (layer, 0, 0)
    hbm = pl.BlockSpec(memory_space=pl.ANY)
    return pl.pallas_call(
        functools.partial(_tail_kernel, alpha=alpha, tiles_per_seq=seq // TAIL_TILE, layer=layer),
        grid=(t // TAIL_TILE,),
        in_specs=[
            pl.BlockSpec((TAIL_TILE, d), tile),
            pl.BlockSpec((TAIL_TILE, a), tile),
            hbm,
            _resident((None, 1, 2 * d), lyr),
            _resident((None, CONV_K, d), lyr),
            _resident((None, 1, d), lyr),
            hbm, hbm, hbm,
            _resident((None, 1, d), lyr),
            _resident((None, 1, d), lyr),
        ],
        out_specs=pl.BlockSpec((TAIL_TILE, d), tile),
        out_shape=jax.ShapeDtypeStruct((t, d), F32),
        scratch_shapes=[
            pltpu.VMEM((TAIL_TILE + CONV_HALO, d), F32),
            pltpu.VMEM((TAIL_TILE, d), BF16),
            pltpu.VMEM((TAIL_TILE, d), BF16),
            pltpu.VMEM((d, 3 * d), BF16),
            pltpu.VMEM((d, 2 * d), BF16),
            pltpu.VMEM((a, d), BF16),
            pltpu.VMEM((d, d), BF16),
            pltpu.VMEM((d, d), BF16),
            pltpu.VMEM((WEIGHT_SLOTS, d // TAIL_WEIGHT_CHUNKS, 3 * d), F32),
            pltpu.VMEM((WEIGHT_SLOTS, d // TAIL_WEIGHT_CHUNKS, 2 * d), F32),
            pltpu.VMEM((WEIGHT_SLOTS, 2 * d // TAIL_WEIGHT_CHUNKS, d), F32),
            pltpu.SemaphoreType.DMA((3, WEIGHT_SLOTS)),
        ],
        compiler_params=pltpu.CompilerParams(
            dimension_semantics=("arbitrary",), vmem_limit_bytes=VMEM_LIMIT),
        name="mixer_tail",
    )(x, attn, w_in, b_gate, conv_w, conv_b, w_pa, w_pc, w_out, g, b)


def kernel(x, ffn1_w_up, ffn1_w_down, ln1_g, ln1_b, w_in, b_gate, conv_w, conv_b,
           w_proj_attn, w_proj_conv, w_out, ln2_g, ln2_b, ffn2_w_up, ffn2_w_down,
           ln3_g, ln3_b):
    batch, seq, d = x.shape
    depth = w_in.shape[0]
    a = N_HEADS * HEAD_DIM
    assert w_in.shape[2] == 3 * a + 3 * d + 2 * d and w_proj_attn.shape[1] == a
    alpha = (2.0 * depth) ** 0.25
    rows = lambda v: v.reshape(depth, 1, -1)

    h = x.reshape(batch * seq, d)
    for l in range(depth):
        h = _ffn_ln(h, ffn1_w_up, ffn1_w_down, rows(ln1_g), rows(ln1_b), l, alpha)
        k, qvt, kmean = _qkv_proj(h, w_in, a, l)
        attn = _moba_attention(k, qvt, kmean.reshape(batch, seq // MOBA_BLOCK, a), batch, seq)
        h = _mixer_tail(h, attn, w_in, rows(b_gate), conv_w, rows(conv_b), w_proj_attn,
                        w_proj_conv, w_out, rows(ln2_g), rows(ln2_b), l, alpha, seq)
        h = _ffn_ln(h, ffn2_w_up, ffn2_w_down, rows(ln3_g), rows(ln3_b), l, alpha)
    return h.reshape(batch, seq, d)
```

```python
import functools
import math

import jax
import jax.numpy as jnp
from jax import lax
from jax.experimental import pallas as pl
from jax.experimental.pallas import tpu as pltpu

N_HEADS = 8
HEAD_DIM = 128
MOBA_BLOCK = 256
MOBA_TOPK = 3
CONV_K = 3
LN_EPS = 1e-5

F32 = jnp.float32
BF16 = jnp.bfloat16

TOKEN_TILE = 1024
FFN_TILE = 1024
TAIL_TILE = 512
SUB_TILE = 256
FF_CHUNK = 256
COL_CHUNK = 256
F32_SUBLANES = 8
BF16_SUBLANES = 16
CONV_HALO = F32_SUBLANES
UP_CHUNKS = 8
TAIL_WEIGHT_CHUNKS = 8
WEIGHT_SLOTS = 2
PV_LAG = 4
ATTN_HEADS = 2
VMEM_LIMIT = 56 * 1024 * 1024


def _resident(block, index):
    return pl.BlockSpec(block, lambda *_: index, pipeline_mode=pl.Buffered(1))


def _layer_norm(y, g, b):
    mu = jnp.mean(y, axis=-1, keepdims=True)
    d = y - mu
    var = jnp.mean(d * d, axis=-1, keepdims=True)
    return d * lax.rsqrt(var + LN_EPS) * g + b


def _dot(a, b):
    return jnp.dot(a, b, preferred_element_type=F32)


def _row_chunks(src_hbm, dst_scr, rows):
    assert src_hbm.shape == dst_scr.shape and src_hbm.shape[0] % rows == 0
    return [(src_hbm.at[pl.ds(r0, rows), :], dst_scr.at[pl.ds(r0, rows), :])
            for r0 in range(0, src_hbm.shape[0], rows)]


def _load_as_bf16(jobs):
    def copy(job, j):
        chunks, stage, sem = job
        slot = j % stage.shape[0]
        assert chunks[j][0].shape == stage.shape[1:] and stage.shape[1] % BF16_SUBLANES == 0
        return pltpu.make_async_copy(chunks[j][0], stage.at[slot], sem.at[slot])

    for job in jobs:
        for j in range(min(job[1].shape[0], len(job[0]))):
            copy(job, j).start()
    for j in range(max(len(job[0]) for job in jobs)):
        for job in jobs:
            chunks, stage, _ = job
            if j < len(chunks):
                copy(job, j).wait()
                chunks[j][1][...] = stage[j % stage.shape[0]].astype(BF16)
                if j + stage.shape[0] < len(chunks):
                    copy(job, j + stage.shape[0]).start()


def _ffn_ln_kernel(x_ref, wup_hbm, wdn_hbm, g_ref, b_ref, o_ref, wup_ref, wdn_ref,
                   up_stage, dn_stage, sems, *, alpha, d_ff, layer):
    @pl.when(pl.program_id(0) == 0)
    def _():
        _load_as_bf16([
            (_row_chunks(wup_hbm.at[layer], wup_ref, up_stage.shape[1]), up_stage, sems.at[0]),
            (_row_chunks(wdn_hbm.at[layer], wdn_ref, dn_stage.shape[1]), dn_stage, sems.at[1])])

    chunks = list(range(0, d_ff, FF_CHUNK))
    for r0 in range(0, x_ref.shape[0], SUB_TILE):
        x = x_ref[r0:r0 + SUB_TILE, :]
        xb = x.astype(BF16)

        def gate_up(c0):
            return (_dot(xb, wup_ref[:, c0:c0 + FF_CHUNK]),
                    _dot(xb, wup_ref[:, d_ff + c0:d_ff + c0 + FF_CHUNK]))

        pending = gate_up(chunks[0])
        acc = None
        for j, c0 in enumerate(chunks):
            gate, up = pending
            if j + 1 < len(chunks):
                pending = gate_up(chunks[j + 1])
            h = (gate * jax.nn.sigmoid(gate) * up).astype(BF16)
            down = _dot(h, wdn_ref[c0:c0 + FF_CHUNK, :])
            acc = down if acc is None else acc + down
        o_ref[r0:r0 + SUB_TILE, :] = _layer_norm(alpha * x + 0.5 * acc, g_ref[...], b_ref[...])


def _ffn_ln(x, w_up, w_down, g, b, layer, alpha):
    t, d = x.shape
    d_ff = w_down.shape[1]
    assert t % FFN_TILE == 0 and FFN_TILE % SUB_TILE == 0 and d_ff % FF_CHUNK == 0
    up_rows, dn_rows = d // UP_CHUNKS, 2 * d_ff // UP_CHUNKS
    assert d % UP_CHUNKS == 0 and d_ff % dn_rows == 0
    return pl.pallas_call(
        functools.partial(_ffn_ln_kernel, alpha=alpha, d_ff=d_ff, layer=layer),
        grid=(t // FFN_TILE,),
        in_specs=[
            pl.BlockSpec((FFN_TILE, d), lambda i: (i, 0)),
            pl.BlockSpec(memory_space=pl.ANY),
            pl.BlockSpec(memory_space=pl.ANY),
            _resident((None, 1, d), (layer, 0, 0)),
            _resident((None, 1, d), (layer, 0, 0)),
        ],
        out_specs=pl.BlockSpec((FFN_TILE, d), lambda i: (i, 0)),
        out_shape=jax.ShapeDtypeStruct((t, d), F32),
        scratch_shapes=[
            pltpu.VMEM((d, 2 * d_ff), BF16),
            pltpu.VMEM((d_ff, d), BF16),
            pltpu.VMEM((WEIGHT_SLOTS, up_rows, 2 * d_ff), F32),
            pltpu.VMEM((WEIGHT_SLOTS, dn_rows, d), F32),
            pltpu.SemaphoreType.DMA((2, WEIGHT_SLOTS)),
        ],
        compiler_params=pltpu.CompilerParams(
            dimension_semantics=("arbitrary",), vmem_limit_bytes=VMEM_LIMIT),
        name="ffn_ln",
    )(x, w_up, w_down, g, b)


def _qkv_kernel(x_ref, wq_ref, wk_ref, wv_ref, k_ref, qvt_ref, kmean_ref, wqvt_scr, *, q_scale):
    a = k_ref.shape[1]

    @pl.when(pl.program_id(0) == 0)
    def _():
        for c0 in range(0, wq_ref.shape[0], COL_CHUNK):
            wqvt_scr[0:a, c0:c0 + COL_CHUNK] = wq_ref[c0:c0 + COL_CHUNK, :].T.astype(BF16)
            wqvt_scr[a:, c0:c0 + COL_CHUNK] = wv_ref[c0:c0 + COL_CHUNK, :].T.astype(BF16)

    for r0 in range(0, x_ref.shape[0], SUB_TILE):
        rows = slice(r0, r0 + SUB_TILE)
        xb = x_ref[rows, :].astype(BF16)
        k = _dot(xb, wk_ref[...])
        k_ref[rows, :] = k.astype(BF16)
        for j in range(SUB_TILE // MOBA_BLOCK):
            blk = k[j * MOBA_BLOCK:(j + 1) * MOBA_BLOCK]
            row = r0 // MOBA_BLOCK + j
            kmean_ref[0, row:row + 1, :] = jnp.sum(blk, axis=0, keepdims=True) * (1.0 / MOBA_BLOCK)
        qvt = lax.dot_general(wqvt_scr[...], xb, (((1,), (1,)), ((), ())),
                              preferred_element_type=F32)
        qvt_ref[0:a, rows] = (qvt[0:a] * q_scale).astype(BF16)
        qvt_ref[a:, rows] = qvt[a:].astype(BF16)


def _qkv_proj(x, w_in, a, layer):
    t, d = x.shape
    q_scale = (HEAD_DIM ** -0.5) * math.log2(math.e)
    assert t % TOKEN_TILE == 0 and TOKEN_TILE % SUB_TILE == 0 and SUB_TILE % MOBA_BLOCK == 0
    assert d % COL_CHUNK == 0
    n_tiles, blocks_per_tile = t // TOKEN_TILE, TOKEN_TILE // MOBA_BLOCK
    return pl.pallas_call(
        functools.partial(_qkv_kernel, q_scale=q_scale),
        grid=(n_tiles,),
        in_specs=[
            pl.BlockSpec((TOKEN_TILE, d), lambda i: (i, 0)),
            _resident((None, d, a), (layer, 0, 0)),
            _resident((None, d, a), (layer, 0, 1)),
            _resident((None, d, a), (layer, 0, 2)),
        ],
        out_specs=[
            pl.BlockSpec((TOKEN_TILE, a), lambda i: (i, 0)),
            pl.BlockSpec((2 * a, TOKEN_TILE), lambda i: (0, i)),
            pl.BlockSpec((1, blocks_per_tile, a), lambda i: (i, 0, 0)),
        ],
        out_shape=[
            jax.ShapeDtypeStruct((t, a), BF16),
            jax.ShapeDtypeStruct((2 * a, t), BF16),
            jax.ShapeDtypeStruct((n_tiles, blocks_per_tile, a), F32),
        ],
        scratch_shapes=[pltpu.VMEM((2 * a, d), BF16)],
        compiler_params=pltpu.CompilerParams(
            dimension_semantics=("arbitrary",), vmem_limit_bytes=VMEM_LIMIT),
        name="qkv_proj",
    )(x, w_in, w_in, w_in)


def _moba_kernel(qt_ref, k_ref, vt_ref, km_ref, o_ref, s_scr, vt1_scr, *, n_blocks, hd):
    blk = MOBA_BLOCK
    n_heads = qt_ref.shape[0] // hd
    blk_id = lax.broadcasted_iota(jnp.int32, (n_blocks, blk), 0)
    key_pos = lax.broadcasted_iota(jnp.int32, (blk, blk), 0)
    qry_pos = lax.broadcasted_iota(jnp.int32, (blk, blk), 1)
    causal = key_pos <= qry_pos

    def q_tile(h, qb):
        return qt_ref[h * hd:(h + 1) * hd, qb * blk:(qb + 1) * blk]

    def choice_bias(h, qb):
        km = km_ref[:, h * hd:(h + 1) * hd]
        km_hi = km.astype(BF16).astype(F32)
        km_split = jnp.concatenate([km_hi, km - km_hi], axis=0).astype(BF16)
        g2 = _dot(km_split, q_tile(h, qb))
        gate = g2[:n_blocks] + g2[n_blocks:]
        past = blk_id < qb
        gate = jnp.where(past, gate, -jnp.inf)
        beaten = jnp.zeros(gate.shape, jnp.int32)
        for m in range(qb):
            gm = gate[m:m + 1, :]
            wins = (gm > gate) | ((gm == gate) & (blk_id > m))
            beaten = beaten + wins.astype(jnp.int32)
        return jnp.where(past & (beaten < MOBA_TOPK), 0.0, -jnp.inf)

    biases = {(h, qb): choice_bias(h, qb)
              for h in range(n_heads) for qb in range(MOBA_TOPK + 1, n_blocks)}

    ones_rows = vt1_scr.shape[1] - hd
    for h in range(n_heads):
        vt1_scr[h, 0:hd, :] = vt_ref[h * hd:(h + 1) * hd, :]
        vt1_scr[h, hd:, :] = jnp.ones((ones_rows, vt1_scr.shape[2]), BF16)

    def slots(group):
        base, out = 0, []
        for qb in group:
            out.append(base)
            base += qb + 1
        assert base <= s_scr.shape[2]
        return out

    def items(group):
        per_head = [(qb, base + n, n) for qb, base in zip(group, slots(group))
                    for n in range(qb + 1)]
        return [(h,) + it for it in per_head for h in range(n_heads)]

    col_max8 = {}
    acc = {}

    def score_block(half, h, qb, slot, n):
        s = _dot(k_ref[n * blk:(n + 1) * blk, h * hd:(h + 1) * hd], q_tile(h, qb))
        if n == qb:
            s = jnp.where(causal, s, -jnp.inf)
        elif (h, qb) in biases:
            s = s + biases[h, qb][n:n + 1, :]
        s_scr[h, half, slot] = s
        mx = jnp.max(s.reshape(blk // F32_SUBLANES, F32_SUBLANES, blk), axis=0)
        col_max8[h, qb] = mx if n == 0 else jnp.maximum(col_max8[h, qb], mx)

    def weights_block(half, h, qb, slot, n):
        if n == 0:
            m_run = jnp.max(col_max8[h, qb], axis=0, keepdims=True)
            col_max8[h, qb] = jnp.broadcast_to(m_run, (F32_SUBLANES, blk))
        s = s_scr[h, half, slot].reshape(blk // F32_SUBLANES, F32_SUBLANES, blk)
        return jnp.exp2(s - col_max8[h, qb]).reshape(blk, blk).astype(BF16)

    def pv_block(p, h, qb, n):
        pv = _dot(vt1_scr[h, :, n * blk:(n + 1) * blk], p)
        acc[h, qb] = pv if n == 0 else acc[h, qb] + pv
        if n == qb:
            a = acc.pop((h, qb))
            out = (a[:hd] * (1.0 / a[hd:hd + 1])).T
            o_ref[qb * blk:(qb + 1) * blk, h * hd:(h + 1) * hd] = out.astype(o_ref.dtype)

    first = MOBA_TOPK
    rest = [qb for qb in range(n_blocks - 1, -1, -1) if qb != first]
    groups = [[first]]
    while rest:
        big = rest.pop(0)
        small = [qb for qb in rest if big + qb + 2 <= n_blocks][:1]
        for qb in small:
            rest.remove(qb)
        groups.append([big] + small)
    for item in items(groups[0]):
        score_block(0, *item)
    ready = []
    for j, group in enumerate(groups):
        cur = items(group)
        nxt = items(groups[j + 1]) if j + 1 < len(groups) else []
        for t in range(max(len(cur), len(nxt))):
            if t < len(nxt):
                score_block((j + 1) % 2, *nxt[t])
            if len(ready) > PV_LAG or (ready and t >= len(cur)):
                pv_block(*ready.pop(0))
            if t < len(cur):
                h, qb, slot, n = cur[t]
                ready.append((weights_block(j % 2, h, qb, slot, n), h, qb, n))
    while ready:
        pv_block(*ready.pop(0))


def _moba_attention(k, qvt, kmean, batch, seq):
    t, a = k.shape
    hd = HEAD_DIM
    n_heads = a // hd
    n_blocks = seq // MOBA_BLOCK
    assert seq % MOBA_BLOCK == 0 and kmean.shape == (batch, n_blocks, a)
    assert n_heads % ATTN_HEADS == 0
    hw = ATTN_HEADS * hd
    steps = n_heads // ATTN_HEADS
    return pl.pallas_call(
        functools.partial(_moba_kernel, n_blocks=n_blocks, hd=hd),
        grid=(batch, steps),
        in_specs=[
            pl.BlockSpec((hw, seq), lambda b, h: (h, b)),
            pl.BlockSpec((seq, hw), lambda b, h: (b, h)),
            pl.BlockSpec((hw, seq), lambda b, h: (steps + h, b)),
            pl.BlockSpec((None, n_blocks, hw), lambda b, h: (b, 0, h)),
        ],
        out_specs=pl.BlockSpec((seq, hw), lambda b, h: (b, h)),
        out_shape=jax.ShapeDtypeStruct((t, a), BF16),
        scratch_shapes=[
            pltpu.VMEM((ATTN_HEADS, 2, n_blocks, MOBA_BLOCK, MOBA_BLOCK), F32),
            pltpu.VMEM((ATTN_HEADS, hd + BF16_SUBLANES, seq), BF16),
        ],
        compiler_params=pltpu.CompilerParams(
            dimension_semantics=("parallel", "parallel"), vmem_limit_bytes=VMEM_LIMIT),
        name="moba_attn",
    )(qvt, k, qvt, kmean)


def _tail_kernel(x_ref, a_ref, win_hbm, bg_ref, cw_ref, cb_ref, wpa_hbm, wpc_hbm, wo_hbm,
                 g_ref, b_ref, o_ref, u_scr, c_scr, mix_scr, wc_ref, wg_ref, wpa_ref, wpc_ref,
                 wo_ref, c_stage, g_stage, p_stage, sems, *, alpha, tiles_per_seq, layer):
    tm, d = x_ref.shape
    a = a_ref.shape[1]
    i = pl.program_id(0)

    @pl.when(i == 0)
    def _():
        n_conv, n_gate = wc_ref.shape[1], wg_ref.shape[1]
        conv_cols = win_hbm.at[layer, pl.ds(0, d), pl.ds(3 * a, n_conv)]
        gate_cols = win_hbm.at[layer, pl.ds(0, d), pl.ds(3 * a + n_conv, n_gate)]
        square = [pair for src, dst in ((wpa_hbm, wpa_ref), (wpc_hbm, wpc_ref), (wo_hbm, wo_ref))
                  for pair in _row_chunks(src.at[layer], dst, p_stage.shape[1])]
        _load_as_bf16([
            (_row_chunks(conv_cols, wc_ref, c_stage.shape[1]), c_stage, sems.at[0]),
            (_row_chunks(gate_cols, wg_ref, g_stage.shape[1]), g_stage, sems.at[1]),
            (square, p_stage, sems.at[2])])

    @pl.when(i % tiles_per_seq == 0)
    def _():
        u_scr[0:CONV_HALO, :] = jnp.zeros((CONV_HALO, d), F32)

    @pl.when(i % tiles_per_seq != 0)
    def _():
        u_scr[0:CONV_HALO, :] = u_scr[tm:tm + CONV_HALO, :]

    for r0 in range(0, tm, SUB_TILE):
        rows = slice(r0, r0 + SUB_TILE)
        x = x_ref[rows, :]
        xb = x.astype(BF16)
        for c0 in range(0, d, COL_CHUNK):
            cs = slice(c0, c0 + COL_CHUNK)
            h = _dot(xb, wc_ref[:, c0:c0 + COL_CHUNK])
            g_b = _dot(xb, wc_ref[:, d + c0:d + c0 + COL_CHUNK])
            g_c = _dot(xb, wc_ref[:, 2 * d + c0:2 * d + c0 + COL_CHUNK])
            u = g_c * h
            top = CONV_HALO + r0
            u_scr[top:top + SUB_TILE, cs] = u
            u1 = u_scr[top - 1:top - 1 + SUB_TILE, cs]
            u2 = u_scr[top - 2:top - 2 + SUB_TILE, cs]
            y = (cw_ref[0:1, cs] * u2 + cw_ref[1:2, cs] * u1 + cw_ref[2:3, cs] * u
                 + cb_ref[:, cs])
            c_scr[rows, cs] = (g_b * y).astype(BF16)

        att = a_ref[rows, :]
        cc = c_scr[rows, :]
        for c0 in range(0, d, COL_CHUNK):
            cs = slice(c0, c0 + COL_CHUNK)
            y_attn = _dot(att, wpa_ref[:, cs])
            y_conv = _dot(cc, wpc_ref[:, cs])
            gate_a = jax.nn.sigmoid(_dot(xb, wg_ref[:, c0:c0 + COL_CHUNK]) + bg_ref[:, cs])
            gate_c = jax.nn.sigmoid(_dot(xb, wg_ref[:, d + c0:d + c0 + COL_CHUNK])
                                    + bg_ref[:, d + c0:d + c0 + COL_CHUNK])
            mix_scr[rows, cs] = (gate_a * y_attn + gate_c * y_conv).astype(BF16)

        mixed = _dot(mix_scr[rows, :], wo_ref[...])
        o_ref[rows, :] = _layer_norm(alpha * x + mixed, g_ref[...], b_ref[...])


def _mixer_tail(x, attn, w_in, b_gate, conv_w, conv_b, w_pa, w_pc, w_out, g, b, layer, alpha, seq):
    t, d = x.shape
    a = attn.shape[1]
    assert t % TAIL_TILE == 0 and seq % TAIL_TILE == 0 and d % COL_CHUNK == 0
    assert TAIL_TILE % SUB_TILE == 0 and d % TAIL_WEIGHT_CHUNKS == 0
    assert w_pa.shape[1:] == (a, d) and a == d
    assert w_pc.shape[1:] == (d, d) and w_out.shape[1:] == (d, d)
    assert conv_w.shape[1:] == (CONV_K, d) and w_in.shape[2] == 3 * a + 5 * d
    tile = lambda i: (i, 0)
    lyr = (layer, 0, 0)
    hbm = pl.BlockSpec(memory_space=pl.ANY)
    return pl.pallas_call(
        functools.partial(_tail_kernel, alpha=alpha, tiles_per_seq=seq // TAIL_TILE, layer=layer),
        grid=(t // TAIL_TILE,),
        in_specs=[
            pl.BlockSpec((TAIL_TILE, d), tile),
            pl.BlockSpec((TAIL_TILE, a), tile),
            hbm,
            _resident((None, 1, 2 * d), lyr),
            _resident((None, CONV_K, d), lyr),
            _resident((None, 1, d), lyr),
            hbm, hbm, hbm,
            _resident((None, 1, d), lyr),
            _resident((None, 1, d), lyr),
        ],
        out_specs=pl.BlockSpec((TAIL_TILE, d), tile),
        out_shape=jax.ShapeDtypeStruct((t, d), F32),
        scratch_shapes=[
            pltpu.VMEM((TAIL_TILE + CONV_HALO, d), F32),
            pltpu.VMEM((TAIL_TILE, d), BF16),
            pltpu.VMEM((TAIL_TILE, d), BF16),
            pltpu.VMEM((d, 3 * d), BF16),
            pltpu.VMEM((d, 2 * d), BF16),
            pltpu.VMEM((a, d), BF16),
            pltpu.VMEM((d, d), BF16),
            pltpu.VMEM((d, d), BF16),
            pltpu.VMEM((WEIGHT_SLOTS, d // TAIL_WEIGHT_CHUNKS, 3 * d), F32),
            pltpu.VMEM((WEIGHT_SLOTS, d // TAIL_WEIGHT_CHUNKS, 2 * d), F32),
            pltpu.VMEM((WEIGHT_SLOTS, 2 * d // TAIL_WEIGHT_CHUNKS, d), F32),
            pltpu.SemaphoreType.DMA((3, WEIGHT_SLOTS)),
        ],
        compiler_params=pltpu.CompilerParams(
            dimension_semantics=("arbitrary",), vmem_limit_bytes=VMEM_LIMIT),
        name="mixer_tail",
    )(x, attn, w_in, b_gate, conv_w, conv_b, w_pa, w_pc, w_out, g, b)


def kernel(x, ffn1_w_up, ffn1_w_down, ln1_g, ln1_b, w_in, b_gate, conv_w, conv_b,
           w_proj_attn, w_proj_conv, w_out, ln2_g, ln2_b, ffn2_w_up, ffn2_w_down,
           ln3_g, ln3_b):
    batch, seq, d = x.shape
    depth = w_in.shape[0]
    a = N_HEADS * HEAD_DIM
    assert w_in.shape[2] == 3 * a + 3 * d + 2 * d and w_proj_attn.shape[1] == a
    alpha = (2.0 * depth) ** 0.25
    rows = lambda v: v.reshape(depth, 1, -1)

    h = x.reshape(batch * seq, d)
    for l in range(depth):
        h = _ffn_ln(h, ffn1_w_up, ffn1_w_down, rows(ln1_g), rows(ln1_b), l, alpha)
        k, qvt, kmean = _qkv_proj(h, w_in, a, l)
        attn = _moba_attention(k, qvt, kmean.reshape(batch, seq // MOBA_BLOCK, a), batch, seq)
        h = _mixer_tail(h, attn, w_in, rows(b_gate), conv_w, rows(conv_b), w_proj_attn,
                        w_proj_conv, w_out, rows(ln2_g), rows(ln2_b), l, alpha, seq)
        h = _ffn_ln(h, ffn2_w_up, ffn2_w_down, rows(ln3_g), rows(ln3_b), l, alpha)
    return h.reshape(batch, seq, d)
```

```python
import functools
import math

import jax
import jax.numpy as jnp
from jax import lax
from jax.experimental import pallas as pl
from jax.experimental.pallas import tpu as pltpu

N_HEADS = 8
HEAD_DIM = 128
MOBA_BLOCK = 256
MOBA_TOPK = 3
CONV_K = 3
LN_EPS = 1e-5

F32 = jnp.float32
BF16 = jnp.bfloat16

TOKEN_TILE = 1024
FFN_TILE = 1024
TAIL_TILE = 512
SUB_TILE = 256
FF_CHUNK = 256
COL_CHUNK = 256
F32_SUBLANES = 8
BF16_SUBLANES = 16
CONV_HALO = F32_SUBLANES
UP_CHUNKS = 8
TAIL_WEIGHT_CHUNKS = 8
WEIGHT_SLOTS = 2
PV_LAG = 4
ATTN_HEADS = 2
VMEM_LIMIT = 56 * 1024 * 1024


def _resident(block, index):
    return pl.BlockSpec(block, lambda *_: index, pipeline_mode=pl.Buffered(1))


def _layer_norm(y, g, b):
    mu = jnp.mean(y, axis=-1, keepdims=True)
    d = y - mu
    var = jnp.mean(d * d, axis=-1, keepdims=True)
    return d * lax.rsqrt(var + LN_EPS) * g + b


def _dot(a, b):
    return jnp.dot(a, b, preferred_element_type=F32)


def _row_chunks(src_hbm, dst_scr, rows):
    assert src_hbm.shape == dst_scr.shape and src_hbm.shape[0] % rows == 0
    return [(src_hbm.at[pl.ds(r0, rows), :], dst_scr.at[pl.ds(r0, rows), :])
            for r0 in range(0, src_hbm.shape[0], rows)]


def _load_as_bf16(jobs):
    def copy(job, j):
        chunks, stage, sem = job
        slot = j % stage.shape[0]
        assert chunks[j][0].shape == stage.shape[1:] and stage.shape[1] % BF16_SUBLANES == 0
        return pltpu.make_async_copy(chunks[j][0], stage.at[slot], sem.at[slot])

    for job in jobs:
        for j in range(min(job[1].shape[0], len(job[0]))):
            copy(job, j).start()
    for j in range(max(len(job[0]) for job in jobs)):
        for job in jobs:
            chunks, stage, _ = job
            if j < len(chunks):
                copy(job, j).wait()
                chunks[j][1][...] = stage[j % stage.shape[0]].astype(BF16)
                if j + stage.shape[0] < len(chunks):
                    copy(job, j + stage.shape[0]).start()


def _ffn_ln_kernel(x_ref, wup_hbm, wdn_hbm, g_ref, b_ref, o_ref, wup_ref, wdn_ref,
                   up_stage, dn_stage, sems, *, alpha, d_ff, layer):
    @pl.when(pl.program_id(0) == 0)
    def _():
        _load_as_bf16([
            (_row_chunks(wup_hbm.at[layer], wup_ref, up_stage.shape[1]), up_stage, sems.at[0]),
            (_row_chunks(wdn_hbm.at[layer], wdn_ref, dn_stage.shape[1]), dn_stage, sems.at[1])])

    items = [(r0, c0) for r0 in range(0, x_ref.shape[0], SUB_TILE)
             for c0 in range(0, d_ff, FF_CHUNK)]
    xb_of = {}

    def gate_up(r0, c0):
        if r0 not in xb_of:
            xb_of[r0] = x_ref[r0:r0 + SUB_TILE, :].astype(BF16)
        xb = xb_of[r0]
        return (_dot(xb, wup_ref[:, c0:c0 + FF_CHUNK]),
                _dot(xb, wup_ref[:, d_ff + c0:d_ff + c0 + FF_CHUNK]))

    pending = gate_up(*items[0])
    acc = None
    for j, (r0, c0) in enumerate(items):
        gate, up = pending
        if j + 1 < len(items):
            pending = gate_up(*items[j + 1])
        h = (gate * jax.nn.sigmoid(gate) * up).astype(BF16)
        down = _dot(h, wdn_ref[c0:c0 + FF_CHUNK, :])
        acc = down if c0 == 0 else acc + down
        if c0 + FF_CHUNK == d_ff:
            rows = slice(r0, r0 + SUB_TILE)
            o_ref[rows, :] = _layer_norm(alpha * x_ref[rows, :] + 0.5 * acc,
                                         g_ref[...], b_ref[...])


def _ffn_ln(x, w_up, w_down, g, b, layer, alpha):
    t, d = x.shape
    d_ff = w_down.shape[1]
    assert t % FFN_TILE == 0 and FFN_TILE % SUB_TILE == 0 and d_ff % FF_CHUNK == 0
    up_rows, dn_rows = d // UP_CHUNKS, 2 * d_ff // UP_CHUNKS
    assert d % UP_CHUNKS == 0 and d_ff % dn_rows == 0
    return pl.pallas_call(
        functools.partial(_ffn_ln_kernel, alpha=alpha, d_ff=d_ff, layer=layer),
        grid=(t // FFN_TILE,),
        in_specs=[
            pl.BlockSpec((FFN_TILE, d), lambda i: (i, 0)),
            pl.BlockSpec(memory_space=pl.ANY),
            pl.BlockSpec(memory_space=pl.ANY),
            _resident((None, 1, d), (layer, 0, 0)),
            _resident((None, 1, d), (layer, 0, 0)),
        ],
        out_specs=pl.BlockSpec((FFN_TILE, d), lambda i: (i, 0)),
        out_shape=jax.ShapeDtypeStruct((t, d), F32),
        scratch_shapes=[
            pltpu.VMEM((d, 2 * d_ff), BF16),
            pltpu.VMEM((d_ff, d), BF16),
            pltpu.VMEM((WEIGHT_SLOTS, up_rows, 2 * d_ff), F32),
            pltpu.VMEM((WEIGHT_SLOTS, dn_rows, d), F32),
            pltpu.SemaphoreType.DMA((2, WEIGHT_SLOTS)),
        ],
        compiler_params=pltpu.CompilerParams(
            dimension_semantics=("arbitrary",), vmem_limit_bytes=VMEM_LIMIT),
        name="ffn_ln",
    )(x, w_up, w_down, g, b)


def _qkv_kernel(x_ref, wq_ref, wk_ref, wv_ref, k_ref, qvt_ref, kmean_ref, wqvt_scr, *, q_scale):
    a = k_ref.shape[1]

    @pl.when(pl.program_id(0) == 0)
    def _():
        for c0 in range(0, wq_ref.shape[0], COL_CHUNK):
            wqvt_scr[0:a, c0:c0 + COL_CHUNK] = wq_ref[c0:c0 + COL_CHUNK, :].T.astype(BF16)
            wqvt_scr[a:, c0:c0 + COL_CHUNK] = wv_ref[c0:c0 + COL_CHUNK, :].T.astype(BF16)

    for r0 in range(0, x_ref.shape[0], SUB_TILE):
        rows = slice(r0, r0 + SUB_TILE)
        xb = x_ref[rows, :].astype(BF16)
        k = _dot(xb, wk_ref[...])
        k_ref[rows, :] = k.astype(BF16)
        for j in range(SUB_TILE // MOBA_BLOCK):
            blk = k[j * MOBA_BLOCK:(j + 1) * MOBA_BLOCK]
            row = r0 // MOBA_BLOCK + j
            kmean_ref[0, row:row + 1, :] = jnp.sum(blk, axis=0, keepdims=True) * (1.0 / MOBA_BLOCK)
        qvt = lax.dot_general(wqvt_scr[...], xb, (((1,), (1,)), ((), ())),
                              preferred_element_type=F32)
        qvt_ref[0:a, rows] = (qvt[0:a] * q_scale).astype(BF16)
        qvt_ref[a:, rows] = qvt[a:].astype(BF16)


def _qkv_proj(x, w_in, a, layer):
    t, d = x.shape
    q_scale = (HEAD_DIM ** -0.5) * math.log2(math.e)
    assert t % TOKEN_TILE == 0 and TOKEN_TILE % SUB_TILE == 0 and SUB_TILE % MOBA_BLOCK == 0
    assert d % COL_CHUNK == 0
    n_tiles, blocks_per_tile = t // TOKEN_TILE, TOKEN_TILE // MOBA_BLOCK
    return pl.pallas_call(
        functools.partial(_qkv_kernel, q_scale=q_scale),
        grid=(n_tiles,),
        in_specs=[
            pl.BlockSpec((TOKEN_TILE, d), lambda i: (i, 0)),
            _resident((None, d, a), (layer, 0, 0)),
            _resident((None, d, a), (layer, 0, 1)),
            _resident((None, d, a), (layer, 0, 2)),
        ],
        out_specs=[
            pl.BlockSpec((TOKEN_TILE, a), lambda i: (i, 0)),
            pl.BlockSpec((2 * a, TOKEN_TILE), lambda i: (0, i)),
            pl.BlockSpec((1, blocks_per_tile, a), lambda i: (i, 0, 0)),
        ],
        out_shape=[
            jax.ShapeDtypeStruct((t, a), BF16),
            jax.ShapeDtypeStruct((2 * a, t), BF16),
            jax.ShapeDtypeStruct((n_tiles, blocks_per_tile, a), F32),
        ],
        scratch_shapes=[pltpu.VMEM((2 * a, d), BF16)],
        compiler_params=pltpu.CompilerParams(
            dimension_semantics=("arbitrary",), vmem_limit_bytes=VMEM_LIMIT),
        name="qkv_proj",
    )(x, w_in, w_in, w_in)


def _moba_kernel(qt_ref, k_ref, vt_ref, km_ref, o_ref, s_scr, vt1_scr, *, n_blocks, hd):
    blk = MOBA_BLOCK
    n_heads = qt_ref.shape[0] // hd
    blk_id = lax.broadcasted_iota(jnp.int32, (n_blocks, blk), 0)
    key_pos = lax.broadcasted_iota(jnp.int32, (blk, blk), 0)
    qry_pos = lax.broadcasted_iota(jnp.int32, (blk, blk), 1)
    causal = key_pos <= qry_pos

    def q_tile(h, qb):
        return qt_ref[h * hd:(h + 1) * hd, qb * blk:(qb + 1) * blk]

    def choice_bias(h, qb):
        km = km_ref[:, h * hd:(h + 1) * hd]
        km_hi = km.astype(BF16).astype(F32)
        km_split = jnp.concatenate([km_hi, km - km_hi], axis=0).astype(BF16)
        g2 = _dot(km_split, q_tile(h, qb))
        gate = g2[:n_blocks] + g2[n_blocks:]
        past = blk_id < qb
        gate = jnp.where(past, gate, -jnp.inf)
        beaten = jnp.zeros(gate.shape, jnp.int32)
        for m in range(qb):
            gm = gate[m:m + 1, :]
            wins = (gm > gate) | ((gm == gate) & (blk_id > m))
            beaten = beaten + wins.astype(jnp.int32)
        return jnp.where(past & (beaten < MOBA_TOPK), 0.0, -jnp.inf)

    biases = {(h, qb): choice_bias(h, qb)
              for h in range(n_heads) for qb in range(MOBA_TOPK + 1, n_blocks)}

    ones_rows = vt1_scr.shape[1] - hd
    for h in range(n_heads):
        vt1_scr[h, 0:hd, :] = vt_ref[h * hd:(h + 1) * hd, :]
        vt1_scr[h, hd:, :] = jnp.ones((ones_rows, vt1_scr.shape[2]), BF16)

    def slots(group):
        base, out = 0, []
        for qb in group:
            out.append(base)
            base += qb + 1
        assert base <= s_scr.shape[2]
        return out

    def items(group):
        per_head = [(qb, base + n, n) for qb, base in zip(group, slots(group))
                    for n in range(qb + 1)]
        return [(h,) + it for it in per_head for h in range(n_heads)]

    col_max8 = {}
    acc = {}

    def score_block(half, h, qb, slot, n):
        s = _dot(k_ref[n * blk:(n + 1) * blk, h * hd:(h + 1) * hd], q_tile(h, qb))
        if n == qb:
            s = jnp.where(causal, s, -jnp.inf)
        elif (h, qb) in biases:
            s = s + biases[h, qb][n:n + 1, :]
        s_scr[h, half, slot] = s
        mx = jnp.max(s.reshape(blk // F32_SUBLANES, F32_SUBLANES, blk), axis=0)
        col_max8[h, qb] = mx if n == 0 else jnp.maximum(col_max8[h, qb], mx)

    def weights_block(half, h, qb, slot, n):
        if n == 0:
            m_run = jnp.max(col_max8[h, qb], axis=0, keepdims=True)
            col_max8[h, qb] = jnp.broadcast_to(m_run, (F32_SUBLANES, blk))
        s = s_scr[h, half, slot].reshape(blk // F32_SUBLANES, F32_SUBLANES, blk)
        return jnp.exp2(s - col_max8[h, qb]).reshape(blk, blk).astype(BF16)

    def pv_block(p, h, qb, n):
        pv = _dot(vt1_scr[h, :, n * blk:(n + 1) * blk], p)
        acc[h, qb] = pv if n == 0 else acc[h, qb] + pv
        if n == qb:
            a = acc.pop((h, qb))
            out = (a[:hd] * (1.0 / a[hd:hd + 1])).T
            o_ref[qb * blk:(qb + 1) * blk, h * hd:(h + 1) * hd] = out.astype(o_ref.dtype)

    first = MOBA_TOPK
    rest = [qb for qb in range(n_blocks - 1, -1, -1) if qb != first]
    groups = [[first]]
    while rest:
        big = rest.pop(0)
        small = [qb for qb in rest if big + qb + 2 <= n_blocks][:1]
        for qb in small:
            rest.remove(qb)
        groups.append([big] + small)
    for item in items(groups[0]):
        score_block(0, *item)
    ready = []
    for j, group in enumerate(groups):
        cur = items(group)
        nxt = items(groups[j + 1]) if j + 1 < len(groups) else []
        for t in range(max(len(cur), len(nxt))):
            if t < len(nxt):
                score_block((j + 1) % 2, *nxt[t])
            if len(ready) > PV_LAG or (ready and t >= len(cur)):
                pv_block(*ready.pop(0))
            if t < len(cur):
                h, qb, slot, n = cur[t]
                ready.append((weights_block(j % 2, h, qb, slot, n), h, qb, n))
    while ready:
        pv_block(*ready.pop(0))


def _moba_attention(k, qvt, kmean, batch, seq):
    t, a = k.shape
    hd = HEAD_DIM
    n_heads = a // hd
    n_blocks = seq // MOBA_BLOCK
    assert seq % MOBA_BLOCK == 0 and kmean.shape == (batch, n_blocks, a)
    assert n_heads % ATTN_HEADS == 0
    hw = ATTN_HEADS * hd
    steps = n_heads // ATTN_HEADS
    return pl.pallas_call(
        functools.partial(_moba_kernel, n_blocks=n_blocks, hd=hd),
        grid=(batch, steps),
        in_specs=[
            pl.BlockSpec((hw, seq), lambda b, h: (h, b)),
            pl.BlockSpec((seq, hw), lambda b, h: (b, h)),
            pl.BlockSpec((hw, seq), lambda b, h: (steps + h, b)),
            pl.BlockSpec((None, n_blocks, hw), lambda b, h: (b, 0, h)),
        ],
        out_specs=pl.BlockSpec((seq, hw), lambda b, h: (b, h)),
        out_shape=jax.ShapeDtypeStruct((t, a), BF16),
        scratch_shapes=[
            pltpu.VMEM((ATTN_HEADS, 2, n_blocks, MOBA_BLOCK, MOBA_BLOCK), F32),
            pltpu.VMEM((ATTN_HEADS, hd + BF16_SUBLANES, seq), BF16),
        ],
        compiler_params=pltpu.CompilerParams(
            dimension_semantics=("parallel", "parallel"), vmem_limit_bytes=VMEM_LIMIT),
        name="moba_attn",
    )(qvt, k, qvt, kmean)


def _tail_kernel(x_ref, a_ref, win_hbm, bg_ref, cw_ref, cb_ref, wpa_hbm, wpc_hbm, wo_hbm,
                 g_ref, b_ref, o_ref, u_scr, c_scr, mix_scr, wc_ref, wg_ref, wpa_ref, wpc_ref,
                 wo_ref, c_stage, g_stage, p_stage, sems, *, alpha, tiles_per_seq, layer):
    tm, d = x_ref.shape
    a = a_ref.shape[1]
    i = pl.program_id(0)

    @pl.when(i == 0)
    def _():
        n_conv, n_gate = wc_ref.shape[1], wg_ref.shape[1]
        conv_cols = win_hbm.at[layer, pl.ds(0, d), pl.ds(3 * a, n_conv)]
        gate_cols = win_hbm.at[layer, pl.ds(0, d), pl.ds(3 * a + n_conv, n_gate)]
        square = [pair for src, dst in ((wpa_hbm, wpa_ref), (wpc_hbm, wpc_ref), (wo_hbm, wo_ref))
                  for pair in _row_chunks(src.at[layer], dst, p_stage.shape[1])]
        _load_as_bf16([
            (_row_chunks(conv_cols, wc_ref, c_stage.shape[1]), c_stage, sems.at[0]),
            (_row_chunks(gate_cols, wg_ref, g_stage.shape[1]), g_stage, sems.at[1]),
            (square, p_stage, sems.at[2])])

    @pl.when(i % tiles_per_seq == 0)
    def _():
        u_scr[0:CONV_HALO, :] = jnp.zeros((CONV_HALO, d), F32)

    @pl.when(i % tiles_per_seq != 0)
    def _():
        u_scr[0:CONV_HALO, :] = u_scr[tm:tm + CONV_HALO, :]

    def conv_phase(r0):
        rows = slice(r0, r0 + SUB_TILE)
        xb = x_ref[rows, :].astype(BF16)
        for c0 in range(0, d, COL_CHUNK):
            cs = slice(c0, c0 + COL_CHUNK)
            h = _dot(xb, wc_ref[:, c0:c0 + COL_CHUNK])
            g_b = _dot(xb, wc_ref[:, d + c0:d + c0 + COL_CHUNK])
            g_c = _dot(xb, wc_ref[:, 2 * d + c0:2 * d + c0 + COL_CHUNK])
            u = g_c * h
            top = CONV_HALO + r0
            u_scr[top:top + SUB_TILE, cs] = u
            u1 = u_scr[top - 1:top - 1 + SUB_TILE, cs]
            u2 = u_scr[top - 2:top - 2 + SUB_TILE, cs]
            y = (cw_ref[0:1, cs] * u2 + cw_ref[1:2, cs] * u1 + cw_ref[2:3, cs] * u
                 + cb_ref[:, cs])
            c_scr[rows, cs] = (g_b * y).astype(BF16)

    def merge_phase(r0):
        rows = slice(r0, r0 + SUB_TILE)
        xb = x_ref[rows, :].astype(BF16)
        att = a_ref[rows, :]
        cc = c_scr[rows, :]
        for c0 in range(0, d, COL_CHUNK):
            cs = slice(c0, c0 + COL_CHUNK)
            y_attn = _dot(att, wpa_ref[:, cs])
            y_conv = _dot(cc, wpc_ref[:, cs])
            gate_a = jax.nn.sigmoid(_dot(xb, wg_ref[:, c0:c0 + COL_CHUNK]) + bg_ref[:, cs])
            gate_c = jax.nn.sigmoid(_dot(xb, wg_ref[:, d + c0:d + c0 + COL_CHUNK])
                                    + bg_ref[:, d + c0:d + c0 + COL_CHUNK])
            mix_scr[rows, cs] = (gate_a * y_attn + gate_c * y_conv).astype(BF16)

    def out_phase(r0):
        rows = slice(r0, r0 + SUB_TILE)
        mixed = _dot(mix_scr[rows, :], wo_ref[...])
        o_ref[rows, :] = _layer_norm(alpha * x_ref[rows, :] + mixed, g_ref[...], b_ref[...])

    conv_phase(0)
    for r0 in range(0, tm, SUB_TILE):
        merge_phase(r0)
        if r0 + SUB_TILE < tm:
            conv_phase(r0 + SUB_TILE)
        out_phase(r0)


def _mixer_tail(x, attn, w_in, b_gate, conv_w, conv_b, w_pa, w_pc, w_out, g, b, layer, alpha, seq):
    t, d = x.shape
    a = attn.shape[1]
    assert t % TAIL_TILE == 0 and seq % TAIL_TILE == 0 and d % COL_CHUNK == 0
    assert TAIL_TILE % SUB_TILE == 0 and d % TAIL_WEIGHT_CHUNKS == 0
    assert w_pa.shape[1:] == (a, d) and a == d
    assert w_pc.shape[1:] == (d, d) and w_out.shape[1:] == (d, d)
    assert conv_w.shape[1:] == (CONV_K, d) and w_in.shape[2] == 3 * a + 5 * d
    tile = lambda i: (i, 0)
    lyr = (layer, 0, 0)
    hbm = pl.BlockSpec(memory_space=pl.ANY)
    return pl.pallas_call(
        functools.partial(_tail_kernel, alpha=alpha, tiles_per_seq=seq // TAIL_TILE, layer=layer),
        grid=(t // TAIL_TILE,),
        in_specs=[
            pl.BlockSpec((TAIL_TILE, d), tile),
            pl.BlockSpec((TAIL_TILE, a), tile),
            hbm,
            _resident((None, 1, 2 * d), lyr),
            _resident((None, CONV_K, d), lyr),
            _resident((None, 1, d), lyr),
            hbm, hbm, hbm,
            _resident((None, 1, d), lyr),
            _resident((None, 1, d), lyr),
        ],
        out_specs=pl.BlockSpec((TAIL_TILE, d), tile),
        out_shape=jax.ShapeDtypeStruct((t, d), F32),
        scratch_shapes=[
            pltpu.VMEM((TAIL_TILE + CONV_HALO, d), F32),
            pltpu.VMEM((TAIL_TILE, d), BF16),
            pltpu.VMEM((TAIL_TILE, d), BF16),
            pltpu.VMEM((d, 3 * d), BF16),
            pltpu.VMEM((d, 2 * d), BF16),
            pltpu.VMEM((a, d), BF16),
            pltpu.VMEM((d, d), BF16),
            pltpu.VMEM((d, d), BF16),
            pltpu.VMEM((WEIGHT_SLOTS, d // TAIL_WEIGHT_CHUNKS, 3 * d), F32),
            pltpu.VMEM((WEIGHT_SLOTS, d // TAIL_WEIGHT_CHUNKS, 2 * d), F32),
            pltpu.VMEM((WEIGHT_SLOTS, 2 * d // TAIL_WEIGHT_CHUNKS, d), F32),
            pltpu.SemaphoreType.DMA((3, WEIGHT_SLOTS)),
        ],
        compiler_params=pltpu.CompilerParams(
            dimension_semantics=("arbitrary",), vmem_limit_bytes=VMEM_LIMIT),
        name="mixer_tail",
    )(x, attn, w_in, b_gate, conv_w, conv_b, w_pa, w_pc, w_out, g, b)


def kernel(x, ffn1_w_up, ffn1_w_down, ln1_g, ln1_b, w_in, b_gate, conv_w, conv_b,
           w_proj_attn, w_proj_conv, w_out, ln2_g, ln2_b, ffn2_w_up, ffn2_w_down,
           ln3_g, ln3_b):
    batch, seq, d = x.shape
    depth = w_in.shape[0]
    a = N_HEADS * HEAD_DIM
    assert w_in.shape[2] == 3 * a + 3 * d + 2 * d and w_proj_attn.shape[1] == a
    alpha = (2.0 * depth) ** 0.25
    rows = lambda v: v.reshape(depth, 1, -1)

    h = x.reshape(batch * seq, d)
    for l in range(depth):
        h = _ffn_ln(h, ffn1_w_up, ffn1_w_down, rows(ln1_g), rows(ln1_b), l, alpha)
        k, qvt, kmean = _qkv_proj(h, w_in, a, l)
        attn = _moba_attention(k, qvt, kmean.reshape(batch, seq // MOBA_BLOCK, a), batch, seq)
        h = _mixer_tail(h, attn, w_in, rows(b_gate), conv_w, rows(conv_b), w_proj_attn,
                        w_proj_conv, w_out, rows(ln2_g), rows(ln2_b), l, alpha, seq)
        h = _ffn_ln(h, ffn2_w_up, ffn2_w_down, rows(ln3_g), rows(ln3_b), l, alpha)
    return h.reshape(batch, seq, d)
```

```python
import functools
import math

import jax
import jax.numpy as jnp
from jax import lax
from jax.experimental import pallas as pl
from jax.experimental.pallas import tpu as pltpu

N_HEADS = 8
HEAD_DIM = 128
MOBA_BLOCK = 256
MOBA_TOPK = 3
CONV_K = 3
LN_EPS = 1e-5

F32 = jnp.float32
BF16 = jnp.bfloat16

TOKEN_TILE = 1024
FFN_TILE = 1024
TAIL_TILE = 1024
SUB_TILE = 256
FF_CHUNK = 256
COL_CHUNK = 256
F32_SUBLANES = 8
BF16_SUBLANES = 16
CONV_HALO = F32_SUBLANES
UP_CHUNKS = 8
TAIL_WEIGHT_CHUNKS = 16
WEIGHT_SLOTS = 2
PV_LAG = 4
ATTN_HEADS = 2
VMEM_LIMIT = 56 * 1024 * 1024


def _resident(block, index):
    return pl.BlockSpec(block, lambda *_: index, pipeline_mode=pl.Buffered(1))


def _layer_norm(y, g, b):
    mu = jnp.mean(y, axis=-1, keepdims=True)
    d = y - mu
    var = jnp.mean(d * d, axis=-1, keepdims=True)
    return d * lax.rsqrt(var + LN_EPS) * g + b


def _dot(a, b):
    return jnp.dot(a, b, preferred_element_type=F32)


def _row_chunks(src_hbm, dst_scr, rows):
    assert src_hbm.shape == dst_scr.shape and src_hbm.shape[0] % rows == 0
    return [(src_hbm.at[pl.ds(r0, rows), :], dst_scr.at[pl.ds(r0, rows), :])
            for r0 in range(0, src_hbm.shape[0], rows)]


def _load_as_bf16(jobs):
    def copy(job, j):
        chunks, stage, sem = job
        slot = j % stage.shape[0]
        assert chunks[j][0].shape == stage.shape[1:] and stage.shape[1] % BF16_SUBLANES == 0
        return pltpu.make_async_copy(chunks[j][0], stage.at[slot], sem.at[slot])

    for job in jobs:
        for j in range(min(job[1].shape[0], len(job[0]))):
            copy(job, j).start()
    for j in range(max(len(job[0]) for job in jobs)):
        for job in jobs:
            chunks, stage, _ = job
            if j < len(chunks):
                copy(job, j).wait()
                chunks[j][1][...] = stage[j % stage.shape[0]].astype(BF16)
                if j + stage.shape[0] < len(chunks):
                    copy(job, j + stage.shape[0]).start()


def _ffn_ln_kernel(x_ref, wup_hbm, wdn_hbm, g_ref, b_ref, o_ref, wup_ref, wdn_ref,
                   up_stage, dn_stage, sems, *, alpha, d_ff, layer):
    @pl.when(pl.program_id(0) == 0)
    def _():
        _load_as_bf16([
            (_row_chunks(wup_hbm.at[layer], wup_ref, up_stage.shape[1]), up_stage, sems.at[0]),
            (_row_chunks(wdn_hbm.at[layer], wdn_ref, dn_stage.shape[1]), dn_stage, sems.at[1])])

    items = [(r0, c0) for r0 in range(0, x_ref.shape[0], SUB_TILE)
             for c0 in range(0, d_ff, FF_CHUNK)]
    xb_of = {}

    def gate_up(r0, c0):
        if r0 not in xb_of:
            xb_of[r0] = x_ref[r0:r0 + SUB_TILE, :].astype(BF16)
        xb = xb_of[r0]
        return (_dot(xb, wup_ref[:, c0:c0 + FF_CHUNK]),
                _dot(xb, wup_ref[:, d_ff + c0:d_ff + c0 + FF_CHUNK]))

    pending = gate_up(*items[0])
    acc = None
    for j, (r0, c0) in enumerate(items):
        gate, up = pending
        if j + 1 < len(items):
            pending = gate_up(*items[j + 1])
        h = (gate * jax.nn.sigmoid(gate) * up).astype(BF16)
        down = _dot(h, wdn_ref[c0:c0 + FF_CHUNK, :])
        acc = down if c0 == 0 else acc + down
        if c0 + FF_CHUNK == d_ff:
            rows = slice(r0, r0 + SUB_TILE)
            o_ref[rows, :] = _layer_norm(alpha * x_ref[rows, :] + 0.5 * acc,
                                         g_ref[...], b_ref[...])


def _ffn_ln(x, w_up, w_down, g, b, layer, alpha):
    t, d = x.shape
    d_ff = w_down.shape[1]
    assert t % FFN_TILE == 0 and FFN_TILE % SUB_TILE == 0 and d_ff % FF_CHUNK == 0
    up_rows, dn_rows = d // UP_CHUNKS, 2 * d_ff // UP_CHUNKS
    assert d % UP_CHUNKS == 0 and d_ff % dn_rows == 0
    return pl.pallas_call(
        functools.partial(_ffn_ln_kernel, alpha=alpha, d_ff=d_ff, layer=layer),
        grid=(t // FFN_TILE,),
        in_specs=[
            pl.BlockSpec((FFN_TILE, d), lambda i: (i, 0)),
            pl.BlockSpec(memory_space=pl.ANY),
            pl.BlockSpec(memory_space=pl.ANY),
            _resident((None, 1, d), (layer, 0, 0)),
            _resident((None, 1, d), (layer, 0, 0)),
        ],
        out_specs=pl.BlockSpec((FFN_TILE, d), lambda i: (i, 0)),
        out_shape=jax.ShapeDtypeStruct((t, d), F32),
        scratch_shapes=[
            pltpu.VMEM((d, 2 * d_ff), BF16),
            pltpu.VMEM((d_ff, d), BF16),
            pltpu.VMEM((WEIGHT_SLOTS, up_rows, 2 * d_ff), F32),
            pltpu.VMEM((WEIGHT_SLOTS, dn_rows, d), F32),
            pltpu.SemaphoreType.DMA((2, WEIGHT_SLOTS)),
        ],
        compiler_params=pltpu.CompilerParams(
            dimension_semantics=("arbitrary",), vmem_limit_bytes=VMEM_LIMIT),
        name="ffn_ln",
    )(x, w_up, w_down, g, b)


def _qkv_kernel(x_ref, wq_ref, wk_ref, wv_ref, k_ref, qvt_ref, kmean_ref, wqvt_scr, *, q_scale):
    a = k_ref.shape[1]

    @pl.when(pl.program_id(0) == 0)
    def _():
        for c0 in range(0, wq_ref.shape[0], COL_CHUNK):
            wqvt_scr[0:a, c0:c0 + COL_CHUNK] = wq_ref[c0:c0 + COL_CHUNK, :].T.astype(BF16)
            wqvt_scr[a:, c0:c0 + COL_CHUNK] = wv_ref[c0:c0 + COL_CHUNK, :].T.astype(BF16)

    for r0 in range(0, x_ref.shape[0], SUB_TILE):
        rows = slice(r0, r0 + SUB_TILE)
        xb = x_ref[rows, :].astype(BF16)
        k = _dot(xb, wk_ref[...])
        k_ref[rows, :] = k.astype(BF16)
        for j in range(SUB_TILE // MOBA_BLOCK):
            blk = k[j * MOBA_BLOCK:(j + 1) * MOBA_BLOCK]
            row = r0 // MOBA_BLOCK + j
            kmean_ref[0, row:row + 1, :] = jnp.sum(blk, axis=0, keepdims=True) * (1.0 / MOBA_BLOCK)
        qvt = lax.dot_general(wqvt_scr[...], xb, (((1,), (1,)), ((), ())),
                              preferred_element_type=F32)
        qvt_ref[0:a, rows] = (qvt[0:a] * q_scale).astype(BF16)
        qvt_ref[a:, rows] = qvt[a:].astype(BF16)


def _qkv_proj(x, w_in, a, layer):
    t, d = x.shape
    q_scale = (HEAD_DIM ** -0.5) * math.log2(math.e)
    assert t % TOKEN_TILE == 0 and TOKEN_TILE % SUB_TILE == 0 and SUB_TILE % MOBA_BLOCK == 0
    assert d % COL_CHUNK == 0
    n_tiles, blocks_per_tile = t // TOKEN_TILE, TOKEN_TILE // MOBA_BLOCK
    return pl.pallas_call(
        functools.partial(_qkv_kernel, q_scale=q_scale),
        grid=(n_tiles,),
        in_specs=[
            pl.BlockSpec((TOKEN_TILE, d), lambda i: (i, 0)),
            _resident((None, d, a), (layer, 0, 0)),
            _resident((None, d, a), (layer, 0, 1)),
            _resident((None, d, a), (layer, 0, 2)),
        ],
        out_specs=[
            pl.BlockSpec((TOKEN_TILE, a), lambda i: (i, 0)),
            pl.BlockSpec((2 * a, TOKEN_TILE), lambda i: (0, i)),
            pl.BlockSpec((1, blocks_per_tile, a), lambda i: (i, 0, 0)),
        ],
        out_shape=[
            jax.ShapeDtypeStruct((t, a), BF16),
            jax.ShapeDtypeStruct((2 * a, t), BF16),
            jax.ShapeDtypeStruct((n_tiles, blocks_per_tile, a), F32),
        ],
        scratch_shapes=[pltpu.VMEM((2 * a, d), BF16)],
        compiler_params=pltpu.CompilerParams(
            dimension_semantics=("arbitrary",), vmem_limit_bytes=VMEM_LIMIT),
        name="qkv_proj",
    )(x, w_in, w_in, w_in)


def _moba_kernel(qt_ref, k_ref, vt_ref, km_ref, o_ref, s_scr, vt1_scr, *, n_blocks, hd):
    blk = MOBA_BLOCK
    n_heads = qt_ref.shape[0] // hd
    blk_id = lax.broadcasted_iota(jnp.int32, (n_blocks, blk), 0)
    key_pos = lax.broadcasted_iota(jnp.int32, (blk, blk), 0)
    qry_pos = lax.broadcasted_iota(jnp.int32, (blk, blk), 1)
    causal = key_pos <= qry_pos

    def q_tile(h, qb):
        return qt_ref[h * hd:(h + 1) * hd, qb * blk:(qb + 1) * blk]

    def choice_bias(h, qb):
        km = km_ref[:, h * hd:(h + 1) * hd]
        km_hi = km.astype(BF16).astype(F32)
        km_split = jnp.concatenate([km_hi, km - km_hi], axis=0).astype(BF16)
        g2 = _dot(km_split, q_tile(h, qb))
        gate = g2[:n_blocks] + g2[n_blocks:]
        past = blk_id < qb
        gate = jnp.where(past, gate, -jnp.inf)
        beaten = jnp.zeros(gate.shape, jnp.int32)
        for m in range(qb):
            gm = gate[m:m + 1, :]
            wins = (gm > gate) | ((gm == gate) & (blk_id > m))
            beaten = beaten + wins.astype(jnp.int32)
        return jnp.where(past & (beaten < MOBA_TOPK), 0.0, -jnp.inf)

    biases = {(h, qb): choice_bias(h, qb)
              for h in range(n_heads) for qb in range(MOBA_TOPK + 1, n_blocks)}

    ones_rows = vt1_scr.shape[1] - hd
    for h in range(n_heads):
        vt1_scr[h, 0:hd, :] = vt_ref[h * hd:(h + 1) * hd, :]
        vt1_scr[h, hd:, :] = jnp.ones((ones_rows, vt1_scr.shape[2]), BF16)

    def slots(group):
        base, out = 0, []
        for qb in group:
            out.append(base)
            base += qb + 1
        assert base <= s_scr.shape[2]
        return out

    def items(group):
        per_head = [(qb, base + n, n) for qb, base in zip(group, slots(group))
                    for n in range(qb + 1)]
        return [(h,) + it for it in per_head for h in range(n_heads)]

    col_max8 = {}
    acc = {}

    def score_block(half, h, qb, slot, n):
        s = _dot(k_ref[n * blk:(n + 1) * blk, h * hd:(h + 1) * hd], q_tile(h, qb))
        if n == qb:
            s = jnp.where(causal, s, -jnp.inf)
        elif (h, qb) in biases:
            s = s + biases[h, qb][n:n + 1, :]
        s_scr[h, half, slot] = s
        mx = jnp.max(s.reshape(blk // F32_SUBLANES, F32_SUBLANES, blk), axis=0)
        col_max8[h, qb] = mx if n == 0 else jnp.maximum(col_max8[h, qb], mx)

    def weights_block(half, h, qb, slot, n):
        if n == 0:
            m_run = jnp.max(col_max8[h, qb], axis=0, keepdims=True)
            col_max8[h, qb] = jnp.broadcast_to(m_run, (F32_SUBLANES, blk))
        s = s_scr[h, half, slot].reshape(blk // F32_SUBLANES, F32_SUBLANES, blk)
        return jnp.exp2(s - col_max8[h, qb]).reshape(blk, blk).astype(BF16)

    def pv_block(p, h, qb, n):
        pv = _dot(vt1_scr[h, :, n * blk:(n + 1) * blk], p)
        acc[h, qb] = pv if n == 0 else acc[h, qb] + pv
        if n == qb:
            a = acc.pop((h, qb))
            out = (a[:hd] * (1.0 / a[hd:hd + 1])).T
            o_ref[qb * blk:(qb + 1) * blk, h * hd:(h + 1) * hd] = out.astype(o_ref.dtype)

    first = MOBA_TOPK
    rest = [qb for qb in range(n_blocks - 1, -1, -1) if qb != first]
    groups = [[first]]
    while rest:
        big = rest.pop(0)
        small = [qb for qb in rest if big + qb + 2 <= n_blocks][:1]
        for qb in small:
            rest.remove(qb)
        groups.append([big] + small)
    for item in items(groups[0]):
        score_block(0, *item)
    ready = []
    for j, group in enumerate(groups):
        cur = items(group)
        nxt = items(groups[j + 1]) if j + 1 < len(groups) else []
        for t in range(max(len(cur), len(nxt))):
            if t < len(nxt):
                score_block((j + 1) % 2, *nxt[t])
            if len(ready) > PV_LAG or (ready and t >= len(cur)):
                pv_block(*ready.pop(0))
            if t < len(cur):
                h, qb, slot, n = cur[t]
                ready.append((weights_block(j % 2, h, qb, slot, n), h, qb, n))
    while ready:
        pv_block(*ready.pop(0))


def _moba_attention(k, qvt, kmean, batch, seq):
    t, a = k.shape
    hd = HEAD_DIM
    n_heads = a // hd
    n_blocks = seq // MOBA_BLOCK
    assert seq % MOBA_BLOCK == 0 and kmean.shape == (batch, n_blocks, a)
    assert n_heads % ATTN_HEADS == 0
    hw = ATTN_HEADS * hd
    steps = n_heads // ATTN_HEADS
    return pl.pallas_call(
        functools.partial(_moba_kernel, n_blocks=n_blocks, hd=hd),
        grid=(batch, steps),
        in_specs=[
            pl.BlockSpec((hw, seq), lambda b, h: (h, b)),
            pl.BlockSpec((seq, hw), lambda b, h: (b, h)),
            pl.BlockSpec((hw, seq), lambda b, h: (steps + h, b)),
            pl.BlockSpec((None, n_blocks, hw), lambda b, h: (b, 0, h)),
        ],
        out_specs=pl.BlockSpec((seq, hw), lambda b, h: (b, h)),
        out_shape=jax.ShapeDtypeStruct((t, a), BF16),
        scratch_shapes=[
            pltpu.VMEM((ATTN_HEADS, 2, n_blocks, MOBA_BLOCK, MOBA_BLOCK), F32),
            pltpu.VMEM((ATTN_HEADS, hd + BF16_SUBLANES, seq), BF16),
        ],
        compiler_params=pltpu.CompilerParams(
            dimension_semantics=("parallel", "parallel"), vmem_limit_bytes=VMEM_LIMIT),
        name="moba_attn",
    )(qvt, k, qvt, kmean)


def _tail_kernel(x_ref, a_ref, win_hbm, bg_ref, cw_ref, cb_ref, wpa_hbm, wpc_hbm, wo_hbm,
                 g_ref, b_ref, o_ref, u_scr, c_scr, mix_scr, wc_ref, wg_ref, wpa_ref, wpc_ref,
                 wo_ref, c_stage, g_stage, p_stage, sems, *, alpha, tiles_per_seq, layer):
    tm, d = x_ref.shape
    a = a_ref.shape[1]
    i = pl.program_id(0)

    @pl.when(i == 0)
    def _():
        n_conv, n_gate = wc_ref.shape[1], wg_ref.shape[1]
        conv_cols = win_hbm.at[layer, pl.ds(0, d), pl.ds(3 * a, n_conv)]
        gate_cols = win_hbm.at[layer, pl.ds(0, d), pl.ds(3 * a + n_conv, n_gate)]
        square = [pair for src, dst in ((wpa_hbm, wpa_ref), (wpc_hbm, wpc_ref), (wo_hbm, wo_ref))
                  for pair in _row_chunks(src.at[layer], dst, p_stage.shape[1])]
        _load_as_bf16([
            (_row_chunks(conv_cols, wc_ref, c_stage.shape[1]), c_stage, sems.at[0]),
            (_row_chunks(gate_cols, wg_ref, g_stage.shape[1]), g_stage, sems.at[1]),
            (square, p_stage, sems.at[2])])

    @pl.when(i % tiles_per_seq == 0)
    def _():
        u_scr[0:CONV_HALO, :] = jnp.zeros((CONV_HALO, d), F32)

    @pl.when(i % tiles_per_seq != 0)
    def _():
        u_scr[0:CONV_HALO, :] = u_scr[tm:tm + CONV_HALO, :]

    def conv_phase(r0):
        rows = slice(r0, r0 + SUB_TILE)
        xb = x_ref[rows, :].astype(BF16)
        for c0 in range(0, d, COL_CHUNK):
            cs = slice(c0, c0 + COL_CHUNK)
            h = _dot(xb, wc_ref[:, c0:c0 + COL_CHUNK])
            g_b = _dot(xb, wc_ref[:, d + c0:d + c0 + COL_CHUNK])
            g_c = _dot(xb, wc_ref[:, 2 * d + c0:2 * d + c0 + COL_CHUNK])
            u = g_c * h
            top = CONV_HALO + r0
            u_scr[top:top + SUB_TILE, cs] = u
            u1 = u_scr[top - 1:top - 1 + SUB_TILE, cs]
            u2 = u_scr[top - 2:top - 2 + SUB_TILE, cs]
            y = (cw_ref[0:1, cs] * u2 + cw_ref[1:2, cs] * u1 + cw_ref[2:3, cs] * u
                 + cb_ref[:, cs])
            c_scr[rows, cs] = (g_b * y).astype(BF16)

    def merge_phase(r0):
        rows = slice(r0, r0 + SUB_TILE)
        xb = x_ref[rows, :].astype(BF16)
        att = a_ref[rows, :]
        cc = c_scr[rows, :]
        for c0 in range(0, d, COL_CHUNK):
            cs = slice(c0, c0 + COL_CHUNK)
            y_attn = _dot(att, wpa_ref[:, cs])
            y_conv = _dot(cc, wpc_ref[:, cs])
            gate_a = jax.nn.sigmoid(_dot(xb, wg_ref[:, c0:c0 + COL_CHUNK]) + bg_ref[:, cs])
            gate_c = jax.nn.sigmoid(_dot(xb, wg_ref[:, d + c0:d + c0 + COL_CHUNK])
                                    + bg_ref[:, d + c0:d + c0 + COL_CHUNK])
            mix_scr[rows, cs] = (gate_a * y_attn + gate_c * y_conv).astype(BF16)

    def out_phase(r0):
        rows = slice(r0, r0 + SUB_TILE)
        mixed = _dot(mix_scr[rows, :], wo_ref[...])
        o_ref[rows, :] = _layer_norm(alpha * x_ref[rows, :] + mixed, g_ref[...], b_ref[...])

    conv_phase(0)
    for r0 in range(0, tm, SUB_TILE):
        merge_phase(r0)
        if r0 + SUB_TILE < tm:
            conv_phase(r0 + SUB_TILE)
        out_phase(r0)


def _mixer_tail(x, attn, w_in, b_gate, conv_w, conv_b, w_pa, w_pc, w_out, g, b, layer, alpha, seq):
    t, d = x.shape
    a = attn.shape[1]
    assert t % TAIL_TILE == 0 and seq % TAIL_TILE == 0 and d % COL_CHUNK == 0
    assert TAIL_TILE % SUB_TILE == 0 and d % TAIL_WEIGHT_CHUNKS == 0
    assert w_pa.shape[1:] == (a, d) and a == d
    assert w_pc.shape[1:] == (d, d) and w_out.shape[1:] == (d, d)
    assert conv_w.shape[1:] == (CONV_K, d) and w_in.shape[2] == 3 * a + 5 * d
    tile = lambda i: (i, 0)
    lyr = (layer, 0, 0)
    hbm = pl.BlockSpec(memory_space=pl.ANY)
    return pl.pallas_call(
        functools.partial(_tail_kernel, alpha=alpha, tiles_per_seq=seq // TAIL_TILE, layer=layer),
        grid=(t // TAIL_TILE,),
        in_specs=[
            pl.BlockSpec((TAIL_TILE, d), tile),
            pl.BlockSpec((TAIL_TILE, a), tile),
            hbm,
            _resident((None, 1, 2 * d), lyr),
            _resident((None, CONV_K, d), lyr),
            _resident((None, 1, d), lyr),
            hbm, hbm, hbm,
            _resident((None, 1, d), lyr),
            _resident((None, 1, d), lyr),
        ],
        out_specs=pl.BlockSpec((TAIL_TILE, d), tile),
        out_shape=jax.ShapeDtypeStruct((t, d), F32),
        scratch_shapes=[
            pltpu.VMEM((TAIL_TILE + CONV_HALO, d), F32),
            pltpu.VMEM((TAIL_TILE, d), BF16),
            pltpu.VMEM((TAIL_TILE, d), BF16),
            pltpu.VMEM((d, 3 * d), BF16),
            pltpu.VMEM((d, 2 * d), BF16),
            pltpu.VMEM((a, d), BF16),
            pltpu.VMEM((d, d), BF16),
            pltpu.VMEM((d, d), BF16),
            pltpu.VMEM((WEIGHT_SLOTS, d // TAIL_WEIGHT_CHUNKS, 3 * d), F32),
            pltpu.VMEM((WEIGHT_SLOTS, d // TAIL_WEIGHT_CHUNKS, 2 * d), F32),
            pltpu.VMEM((WEIGHT_SLOTS, 2 * d // TAIL_WEIGHT_CHUNKS, d), F32),
            pltpu.SemaphoreType.DMA((3, WEIGHT_SLOTS)),
        ],
        compiler_params=pltpu.CompilerParams(
            dimension_semantics=("arbitrary",), vmem_limit_bytes=VMEM_LIMIT),
        name="mixer_tail",
    )(x, attn, w_in, b_gate, conv_w, conv_b, w_pa, w_pc, w_out, g, b)


def kernel(x, ffn1_w_up, ffn1_w_down, ln1_g, ln1_b, w_in, b_gate, conv_w, conv_b,
           w_proj_attn, w_proj_conv, w_out, ln2_g, ln2_b, ffn2_w_up, ffn2_w_down,
           ln3_g, ln3_b):
    batch, seq, d = x.shape
    depth = w_in.shape[0]
    a = N_HEADS * HEAD_DIM
    assert w_in.shape[2] == 3 * a + 3 * d + 2 * d and w_proj_attn.shape[1] == a
    alpha = (2.0 * depth) ** 0.25
    rows = lambda v: v.reshape(depth, 1, -1)

    h = x.reshape(batch * seq, d)
    for l in range(depth):
        h = _ffn_ln(h, ffn1_w_up, ffn1_w_down, rows(ln1_g), rows(ln1_b), l, alpha)
        k, qvt, kmean = _qkv_proj(h, w_in, a, l)
        attn = _moba_attention(k, qvt, kmean.reshape(batch, seq // MOBA_BLOCK, a), batch, seq)
        h = _mixer_tail(h, attn, w_in, rows(b_gate), conv_w, rows(conv_b), w_proj_attn,
                        w_proj_conv, w_out, rows(ln2_g), rows(ln2_b), l, alpha, seq)
        h = _ffn_ln(h, ffn2_w_up, ffn2_w_down, rows(ln3_g), rows(ln3_b), l, alpha)
    return h.reshape(batch, seq, d)
```

```python
import functools
import math

import jax
import jax.numpy as jnp
from jax import lax
from jax.experimental import pallas as pl
from jax.experimental.pallas import tpu as pltpu

N_HEADS = 8
HEAD_DIM = 128
MOBA_BLOCK = 256
MOBA_TOPK = 3
CONV_K = 3
LN_EPS = 1e-5

F32 = jnp.float32
BF16 = jnp.bfloat16

TOKEN_TILE = 1024
FFN_TILE = 1024
TAIL_TILE = 512
SUB_TILE = 256
FF_CHUNK = 256
COL_CHUNK = 256
F32_SUBLANES = 8
BF16_SUBLANES = 16
CONV_HALO = F32_SUBLANES
UP_CHUNKS = 8
TAIL_WEIGHT_CHUNKS = 8
WEIGHT_SLOTS = 2
PV_LAG = 4
ATTN_HEADS = 2
VMEM_LIMIT = 56 * 1024 * 1024


def _resident(block, index):
    return pl.BlockSpec(block, lambda *_: index, pipeline_mode=pl.Buffered(1))


def _layer_norm(y, g, b):
    mu = jnp.mean(y, axis=-1, keepdims=True)
    d = y - mu
    var = jnp.mean(d * d, axis=-1, keepdims=True)
    return d * lax.rsqrt(var + LN_EPS) * g + b


def _dot(a, b):
    return jnp.dot(a, b, preferred_element_type=F32)


def _row_chunks(src_hbm, dst_scr, rows):
    assert src_hbm.shape == dst_scr.shape and src_hbm.shape[0] % rows == 0
    return [(src_hbm.at[pl.ds(r0, rows), :], dst_scr.at[pl.ds(r0, rows), :])
            for r0 in range(0, src_hbm.shape[0], rows)]


def _load_as_bf16(jobs):
    def copy(job, j):
        chunks, stage, sem = job
        slot = j % stage.shape[0]
        assert chunks[j][0].shape == stage.shape[1:] and stage.shape[1] % BF16_SUBLANES == 0
        return pltpu.make_async_copy(chunks[j][0], stage.at[slot], sem.at[slot])

    for job in jobs:
        for j in range(min(job[1].shape[0], len(job[0]))):
            copy(job, j).start()
    for j in range(max(len(job[0]) for job in jobs)):
        for job in jobs:
            chunks, stage, _ = job
            if j < len(chunks):
                copy(job, j).wait()
                chunks[j][1][...] = stage[j % stage.shape[0]].astype(BF16)
                if j + stage.shape[0] < len(chunks):
                    copy(job, j + stage.shape[0]).start()


def _ffn_ln_kernel(x_ref, wup_hbm, wdn_hbm, g_ref, b_ref, o_ref, wup_ref, wdn_ref,
                   up_stage, dn_stage, sems, *, alpha, d_ff, layer):
    @pl.when(pl.program_id(0) == 0)
    def _():
        _load_as_bf16([
            (_row_chunks(wup_hbm.at[layer], wup_ref, up_stage.shape[1]), up_stage, sems.at[0]),
            (_row_chunks(wdn_hbm.at[layer], wdn_ref, dn_stage.shape[1]), dn_stage, sems.at[1])])

    items = [(r0, c0) for r0 in range(0, x_ref.shape[0], SUB_TILE)
             for c0 in range(0, d_ff, FF_CHUNK)]
    xb_of = {}

    def gate_up(r0, c0):
        if r0 not in xb_of:
            xb_of[r0] = x_ref[r0:r0 + SUB_TILE, :].astype(BF16)
        xb = xb_of[r0]
        return (_dot(xb, wup_ref[:, c0:c0 + FF_CHUNK]),
                _dot(xb, wup_ref[:, d_ff + c0:d_ff + c0 + FF_CHUNK]))

    pending = gate_up(*items[0])
    acc = None
    for j, (r0, c0) in enumerate(items):
        gate, up = pending
        if j + 1 < len(items):
            pending = gate_up(*items[j + 1])
        h = (gate * jax.nn.sigmoid(gate) * up).astype(BF16)
        down = _dot(h, wdn_ref[c0:c0 + FF_CHUNK, :])
        acc = down if c0 == 0 else acc + down
        if c0 + FF_CHUNK == d_ff:
            rows = slice(r0, r0 + SUB_TILE)
            o_ref[rows, :] = _layer_norm(alpha * x_ref[rows, :] + 0.5 * acc,
                                         g_ref[...], b_ref[...])


def _ffn_ln(x, w_up, w_down, g, b, layer, alpha):
    t, d = x.shape
    d_ff = w_down.shape[1]
    assert t % FFN_TILE == 0 and FFN_TILE % SUB_TILE == 0 and d_ff % FF_CHUNK == 0
    up_rows, dn_rows = d // UP_CHUNKS, 2 * d_ff // UP_CHUNKS
    assert d % UP_CHUNKS == 0 and d_ff % dn_rows == 0
    return pl.pallas_call(
        functools.partial(_ffn_ln_kernel, alpha=alpha, d_ff=d_ff, layer=layer),
        grid=(t // FFN_TILE,),
        in_specs=[
            pl.BlockSpec((FFN_TILE, d), lambda i: (i, 0)),
            pl.BlockSpec(memory_space=pl.ANY),
            pl.BlockSpec(memory_space=pl.ANY),
            _resident((None, 1, d), (layer, 0, 0)),
            _resident((None, 1, d), (layer, 0, 0)),
        ],
        out_specs=pl.BlockSpec((FFN_TILE, d), lambda i: (i, 0)),
        out_shape=jax.ShapeDtypeStruct((t, d), F32),
        scratch_shapes=[
            pltpu.VMEM((d, 2 * d_ff), BF16),
            pltpu.VMEM((d_ff, d), BF16),
            pltpu.VMEM((WEIGHT_SLOTS, up_rows, 2 * d_ff), F32),
            pltpu.VMEM((WEIGHT_SLOTS, dn_rows, d), F32),
            pltpu.SemaphoreType.DMA((2, WEIGHT_SLOTS)),
        ],
        compiler_params=pltpu.CompilerParams(
            dimension_semantics=("arbitrary",), vmem_limit_bytes=VMEM_LIMIT),
        name="ffn_ln",
    )(x, w_up, w_down, g, b)


def _qkv_kernel(x_ref, wq_ref, wk_ref, wv_ref, k_ref, qvt_ref, kmean_ref, wqvt_scr, *, q_scale):
    a = k_ref.shape[1]

    @pl.when(pl.program_id(0) == 0)
    def _():
        for c0 in range(0, wq_ref.shape[0], COL_CHUNK):
            wqvt_scr[0:a, c0:c0 + COL_CHUNK] = wq_ref[c0:c0 + COL_CHUNK, :].T.astype(BF16)
            wqvt_scr[a:, c0:c0 + COL_CHUNK] = wv_ref[c0:c0 + COL_CHUNK, :].T.astype(BF16)

    for r0 in range(0, x_ref.shape[0], SUB_TILE):
        rows = slice(r0, r0 + SUB_TILE)
        xb = x_ref[rows, :].astype(BF16)
        k = _dot(xb, wk_ref[...])
        k_ref[rows, :] = k.astype(BF16)
        for j in range(SUB_TILE // MOBA_BLOCK):
            blk = k[j * MOBA_BLOCK:(j + 1) * MOBA_BLOCK]
            row = r0 // MOBA_BLOCK + j
            kmean_ref[0, row:row + 1, :] = jnp.sum(blk, axis=0, keepdims=True) * (1.0 / MOBA_BLOCK)
        qvt = lax.dot_general(wqvt_scr[...], xb, (((1,), (1,)), ((), ())),
                              preferred_element_type=F32)
        qvt_ref[0:a, rows] = (qvt[0:a] * q_scale).astype(BF16)
        qvt_ref[a:, rows] = qvt[a:].astype(BF16)


def _qkv_proj(x, w_in, a, layer):
    t, d = x.shape
    q_scale = (HEAD_DIM ** -0.5) * math.log2(math.e)
    assert t % TOKEN_TILE == 0 and TOKEN_TILE % SUB_TILE == 0 and SUB_TILE % MOBA_BLOCK == 0
    assert d % COL_CHUNK == 0
    n_tiles, blocks_per_tile = t // TOKEN_TILE, TOKEN_TILE // MOBA_BLOCK
    return pl.pallas_call(
        functools.partial(_qkv_kernel, q_scale=q_scale),
        grid=(n_tiles,),
        in_specs=[
            pl.BlockSpec((TOKEN_TILE, d), lambda i: (i, 0)),
            _resident((None, d, a), (layer, 0, 0)),
            _resident((None, d, a), (layer, 0, 1)),
            _resident((None, d, a), (layer, 0, 2)),
        ],
        out_specs=[
            pl.BlockSpec((TOKEN_TILE, a), lambda i: (i, 0)),
            pl.BlockSpec((2 * a, TOKEN_TILE), lambda i: (0, i)),
            pl.BlockSpec((1, blocks_per_tile, a), lambda i: (i, 0, 0)),
        ],
        out_shape=[
            jax.ShapeDtypeStruct((t, a), BF16),
            jax.ShapeDtypeStruct((2 * a, t), BF16),
            jax.ShapeDtypeStruct((n_tiles, blocks_per_tile, a), F32),
        ],
        scratch_shapes=[pltpu.VMEM((2 * a, d), BF16)],
        compiler_params=pltpu.CompilerParams(
            dimension_semantics=("arbitrary",), vmem_limit_bytes=VMEM_LIMIT),
        name="qkv_proj",
    )(x, w_in, w_in, w_in)


def _moba_kernel(qt_ref, k_ref, vt_ref, km_ref, o_ref, s_scr, vt1_scr, *, n_blocks, hd):
    blk = MOBA_BLOCK
    n_heads = qt_ref.shape[0] // hd
    blk_id = lax.broadcasted_iota(jnp.int32, (n_blocks, blk), 0)
    key_pos = lax.broadcasted_iota(jnp.int32, (blk, blk), 0)
    qry_pos = lax.broadcasted_iota(jnp.int32, (blk, blk), 1)
    causal = key_pos <= qry_pos

    def q_tile(h, qb):
        return qt_ref[h * hd:(h + 1) * hd, qb * blk:(qb + 1) * blk]

    def choice_bias(h, qb):
        km = km_ref[:, h * hd:(h + 1) * hd]
        km_hi = km.astype(BF16).astype(F32)
        km_split = jnp.concatenate([km_hi, km - km_hi], axis=0).astype(BF16)
        g2 = _dot(km_split, q_tile(h, qb))
        gate = g2[:n_blocks] + g2[n_blocks:]
        past = blk_id < qb
        gate = jnp.where(past, gate, -jnp.inf)
        beaten = jnp.zeros(gate.shape, jnp.int32)
        for m in range(qb):
            gm = gate[m:m + 1, :]
            wins = (gm > gate) | ((gm == gate) & (blk_id > m))
            beaten = beaten + wins.astype(jnp.int32)
        return jnp.where(past & (beaten < MOBA_TOPK), 0.0, -jnp.inf)

    biases = {(h, qb): choice_bias(h, qb)
              for h in range(n_heads) for qb in range(MOBA_TOPK + 1, n_blocks)}

    ones_rows = vt1_scr.shape[1] - hd
    for h in range(n_heads):
        vt1_scr[h, 0:hd, :] = vt_ref[h * hd:(h + 1) * hd, :]
        vt1_scr[h, hd:, :] = jnp.ones((ones_rows, vt1_scr.shape[2]), BF16)

    def slots(group):
        base, out = 0, []
        for qb in group:
            out.append(base)
            base += qb + 1
        assert base <= s_scr.shape[2]
        return out

    def items(group):
        per_head = [(qb, base + n, n) for qb, base in zip(group, slots(group))
                    for n in range(qb + 1)]
        return [(h,) + it for it in per_head for h in range(n_heads)]

    col_max8 = {}
    acc = {}

    def score_block(half, h, qb, slot, n):
        s = _dot(k_ref[n * blk:(n + 1) * blk, h * hd:(h + 1) * hd], q_tile(h, qb))
        if n == qb:
            s = jnp.where(causal, s, -jnp.inf)
        elif (h, qb) in biases:
            s = s + biases[h, qb][n:n + 1, :]
        s_scr[h, half, slot] = s
        mx = jnp.max(s.reshape(blk // F32_SUBLANES, F32_SUBLANES, blk), axis=0)
        col_max8[h, qb] = mx if n == 0 else jnp.maximum(col_max8[h, qb], mx)

    def weights_block(half, h, qb, slot, n):
        if n == 0:
            m_run = jnp.max(col_max8[h, qb], axis=0, keepdims=True)
            col_max8[h, qb] = jnp.broadcast_to(m_run, (F32_SUBLANES, blk))
        s = s_scr[h, half, slot].reshape(blk // F32_SUBLANES, F32_SUBLANES, blk)
        return jnp.exp2(s - col_max8[h, qb]).reshape(blk, blk).astype(BF16)

    def pv_block(p, h, qb, n):
        pv = _dot(vt1_scr[h, :, n * blk:(n + 1) * blk], p)
        acc[h, qb] = pv if n == 0 else acc[h, qb] + pv
        if n == qb:
            a = acc.pop((h, qb))
            out = (a[:hd] * (1.0 / a[hd:hd + 1])).T
            o_ref[qb * blk:(qb + 1) * blk, h * hd:(h + 1) * hd] = out.astype(o_ref.dtype)

    first = MOBA_TOPK
    rest = [qb for qb in range(n_blocks - 1, -1, -1) if qb != first]
    groups = [[first]]
    while rest:
        big = rest.pop(0)
        small = [qb for qb in rest if big + qb + 2 <= n_blocks][:1]
        for qb in small:
            rest.remove(qb)
        groups.append([big] + small)
    for item in items(groups[0]):
        score_block(0, *item)
    ready = []
    for j, group in enumerate(groups):
        cur = items(group)
        nxt = items(groups[j + 1]) if j + 1 < len(groups) else []
        for t in range(max(len(cur), len(nxt))):
            if t < len(nxt):
                score_block((j + 1) % 2, *nxt[t])
            if len(ready) > PV_LAG or (ready and t >= len(cur)):
                pv_block(*ready.pop(0))
            if t < len(cur):
                h, qb, slot, n = cur[t]
                ready.append((weights_block(j % 2, h, qb, slot, n), h, qb, n))
    while ready:
        pv_block(*ready.pop(0))


def _moba_attention(k, qvt, kmean, batch, seq):
    t, a = k.shape
    hd = HEAD_DIM
    n_heads = a // hd
    n_blocks = seq // MOBA_BLOCK
    assert seq % MOBA_BLOCK == 0 and kmean.shape == (batch, n_blocks, a)
    assert n_heads % ATTN_HEADS == 0
    hw = ATTN_HEADS * hd
    steps = n_heads // ATTN_HEADS
    return pl.pallas_call(
        functools.partial(_moba_kernel, n_blocks=n_blocks, hd=hd),
        grid=(batch, steps),
        in_specs=[
            pl.BlockSpec((hw, seq), lambda b, h: (h, b)),
            pl.BlockSpec((seq, hw), lambda b, h: (b, h)),
            pl.BlockSpec((hw, seq), lambda b, h: (steps + h, b)),
            pl.BlockSpec((None, n_blocks, hw), lambda b, h: (b, 0, h)),
        ],
        out_specs=pl.BlockSpec((seq, hw), lambda b, h: (b, h)),
        out_shape=jax.ShapeDtypeStruct((t, a), BF16),
        scratch_shapes=[
            pltpu.VMEM((ATTN_HEADS, 2, n_blocks, MOBA_BLOCK, MOBA_BLOCK), F32),
            pltpu.VMEM((ATTN_HEADS, hd + BF16_SUBLANES, seq), BF16),
        ],
        compiler_params=pltpu.CompilerParams(
            dimension_semantics=("parallel", "parallel"), vmem_limit_bytes=VMEM_LIMIT),
        name="moba_attn",
    )(qvt, k, qvt, kmean)


def _tail_kernel(x_ref, a_ref, win_hbm, bg_ref, cw_ref, cb_ref, wpa_hbm, wpc_hbm, wo_hbm,
                 g_ref, b_ref, o_ref, u_scr, c_scr, mix_scr, wc_ref, wg_ref, wpa_ref, wpc_ref,
                 wo_ref, c_stage, g_stage, p_stage, sems, *, alpha, tiles_per_seq, layer):
    tm, d = x_ref.shape
    a = a_ref.shape[1]
    i = pl.program_id(0)

    @pl.when(i == 0)
    def _():
        n_conv, n_gate = wc_ref.shape[1], wg_ref.shape[1]
        conv_cols = win_hbm.at[layer, pl.ds(0, d), pl.ds(3 * a, n_conv)]
        gate_cols = win_hbm.at[layer, pl.ds(0, d), pl.ds(3 * a + n_conv, n_gate)]
        square = [pair for src, dst in ((wpa_hbm, wpa_ref), (wpc_hbm, wpc_ref), (wo_hbm, wo_ref))
                  for pair in _row_chunks(src.at[layer], dst, p_stage.shape[1])]
        _load_as_bf16([
            (_row_chunks(conv_cols, wc_ref, c_stage.shape[1]), c_stage, sems.at[0]),
            (_row_chunks(gate_cols, wg_ref, g_stage.shape[1]), g_stage, sems.at[1]),
            (square, p_stage, sems.at[2])])

    @pl.when(i % tiles_per_seq == 0)
    def _():
        u_scr[...] = jnp.zeros(u_scr.shape, F32)

    row = lax.broadcasted_iota(jnp.int32, (SUB_TILE, COL_CHUNK), 0)

    def conv_phase(r0):
        rows = slice(r0, r0 + SUB_TILE)
        xb = x_ref[rows, :].astype(BF16)
        for c0 in range(0, d, COL_CHUNK):
            cs = slice(c0, c0 + COL_CHUNK)
            h = _dot(xb, wc_ref[:, c0:c0 + COL_CHUNK])
            g_b = _dot(xb, wc_ref[:, d + c0:d + c0 + COL_CHUNK])
            g_c = _dot(xb, wc_ref[:, 2 * d + c0:2 * d + c0 + COL_CHUNK])
            u = g_c * h
            before = u_scr[:, cs]
            u_scr[:, cs] = u[SUB_TILE - CONV_HALO:, :]
            last, last2 = before[CONV_HALO - 1:CONV_HALO, :], before[CONV_HALO - 2:CONV_HALO - 1, :]
            u1 = jnp.where(row == 0, last, pltpu.roll(u, 1, 0))
            u2 = jnp.where(row == 0, last2, jnp.where(row == 1, last, pltpu.roll(u, 2, 0)))
            y = (cw_ref[0:1, cs] * u2 + cw_ref[1:2, cs] * u1 + cw_ref[2:3, cs] * u
                 + cb_ref[:, cs])
            c_scr[rows, cs] = (g_b * y).astype(BF16)

    def merge_phase(r0):
        rows = slice(r0, r0 + SUB_TILE)
        xb = x_ref[rows, :].astype(BF16)
        att = a_ref[rows, :]
        cc = c_scr[rows, :]
        for c0 in range(0, d, COL_CHUNK):
            cs = slice(c0, c0 + COL_CHUNK)
            y_attn = _dot(att, wpa_ref[:, cs])
            y_conv = _dot(cc, wpc_ref[:, cs])
            gate_a = jax.nn.sigmoid(_dot(xb, wg_ref[:, c0:c0 + COL_CHUNK]) + bg_ref[:, cs])
            gate_c = jax.nn.sigmoid(_dot(xb, wg_ref[:, d + c0:d + c0 + COL_CHUNK])
                                    + bg_ref[:, d + c0:d + c0 + COL_CHUNK])
            mix_scr[rows, cs] = (gate_a * y_attn + gate_c * y_conv).astype(BF16)

    def out_phase(r0):
        rows = slice(r0, r0 + SUB_TILE)
        mixed = _dot(mix_scr[rows, :], wo_ref[...])
        o_ref[rows, :] = _layer_norm(alpha * x_ref[rows, :] + mixed, g_ref[...], b_ref[...])

    conv_phase(0)
    for r0 in range(0, tm, SUB_TILE):
        merge_phase(r0)
        if r0 + SUB_TILE < tm:
            conv_phase(r0 + SUB_TILE)
        out_phase(r0)


def _mixer_tail(x, attn, w_in, b_gate, conv_w, conv_b, w_pa, w_pc, w_out, g, b, layer, alpha, seq):
    t, d = x.shape
    a = attn.shape[1]
    assert t % TAIL_TILE == 0 and seq % TAIL_TILE == 0 and d % COL_CHUNK == 0
    assert TAIL_TILE % SUB_TILE == 0 and d % TAIL_WEIGHT_CHUNKS == 0
    assert w_pa.shape[1:] == (a, d) and a == d
    assert w_pc.shape[1:] == (d, d) and w_out.shape[1:] == (d, d)
    assert conv_w.shape[1:] == (CONV_K, d) and w_in.shape[2] == 3 * a + 5 * d
    tile = lambda i: (i, 0)
    lyr = (layer, 0, 0)
    hbm = pl.BlockSpec(memory_space=pl.ANY)
    return pl.pallas_call(
        functools.partial(_tail_kernel, alpha=alpha, tiles_per_seq=seq // TAIL_TILE, layer=layer),
        grid=(t // TAIL_TILE,),
        in_specs=[
            pl.BlockSpec((TAIL_TILE, d), tile),
            pl.BlockSpec((TAIL_TILE, a), tile),
            hbm,
            _resident((None, 1, 2 * d), lyr),
            _resident((None, CONV_K, d), lyr),
            _resident((None, 1, d), lyr),
            hbm, hbm, hbm,
            _resident((None, 1, d), lyr),
            _resident((None, 1, d), lyr),
        ],
        out_specs=pl.BlockSpec((TAIL_TILE, d), tile),
        out_shape=jax.ShapeDtypeStruct((t, d), F32),
        scratch_shapes=[
            pltpu.VMEM((CONV_HALO, d), F32),
            pltpu.VMEM((TAIL_TILE, d), BF16),
            pltpu.VMEM((TAIL_TILE, d), BF16),
            pltpu.VMEM((d, 3 * d), BF16),
            pltpu.VMEM((d, 2 * d), BF16),
            pltpu.VMEM((a, d), BF16),
            pltpu.VMEM((d, d), BF16),
            pltpu.VMEM((d, d), BF16),
            pltpu.VMEM((WEIGHT_SLOTS, d // TAIL_WEIGHT_CHUNKS, 3 * d), F32),
            pltpu.VMEM((WEIGHT_SLOTS, d // TAIL_WEIGHT_CHUNKS, 2 * d), F32),
            pltpu.VMEM((WEIGHT_SLOTS, 2 * d // TAIL_WEIGHT_CHUNKS, d), F32),
            pltpu.SemaphoreType.DMA((3, WEIGHT_SLOTS)),
        ],
        compiler_params=pltpu.CompilerParams(
            dimension_semantics=("arbitrary",), vmem_limit_bytes=VMEM_LIMIT),
        name="mixer_tail",
    )(x, attn, w_in, b_gate, conv_w, conv_b, w_pa, w_pc, w_out, g, b)


def kernel(x, ffn1_w_up, ffn1_w_down, ln1_g, ln1_b, w_in, b_gate, conv_w, conv_b,
           w_proj_attn, w_proj_conv, w_out, ln2_g, ln2_b, ffn2_w_up, ffn2_w_down,
           ln3_g, ln3_b):
    batch, seq, d = x.shape
    depth = w_in.shape[0]
    a = N_HEADS * HEAD_DIM
    assert w_in.shape[2] == 3 * a + 3 * d + 2 * d and w_proj_attn.shape[1] == a
    alpha = (2.0 * depth) ** 0.25
    rows = lambda v: v.reshape(depth, 1, -1)

    h = x.reshape(batch * seq, d)
    for l in range(depth):
        h = _ffn_ln(h, ffn1_w_up, ffn1_w_down, rows(ln1_g), rows(ln1_b), l, alpha)
        k, qvt, kmean = _qkv_proj(h, w_in, a, l)
        attn = _moba_attention(k, qvt, kmean.reshape(batch, seq // MOBA_BLOCK, a), batch, seq)
        h = _mixer_tail(h, attn, w_in, rows(b_gate), conv_w, rows(conv_b), w_proj_attn,
                        w_proj_conv, w_out, rows(ln2_g), rows(ln2_b), l, alpha, seq)
        h = _ffn_ln(h, ffn2_w_up, ffn2_w_down, rows(ln3_g), rows(ln3_b), l, alpha)
    return h.reshape(batch, seq, d)
```

```python
import functools
import math

import jax
import jax.numpy as jnp
from jax import lax
from jax.experimental import pallas as pl
from jax.experimental.pallas import tpu as pltpu

N_HEADS = 8
HEAD_DIM = 128
MOBA_BLOCK = 256
MOBA_TOPK = 3
CONV_K = 3
LN_EPS = 1e-5

F32 = jnp.float32
BF16 = jnp.bfloat16

TOKEN_TILE = 1024
FFN_TILE = 1024
TAIL_TILE = 1024
SUB_TILE = 256
FF_CHUNK = 256
COL_CHUNK = 256
F32_SUBLANES = 8
BF16_SUBLANES = 16
CONV_HALO = F32_SUBLANES
UP_CHUNKS = 8
TAIL_WEIGHT_CHUNKS = 8
WEIGHT_SLOTS = 2
PV_LAG = 4
ATTN_HEADS = 2
VMEM_LIMIT = 56 * 1024 * 1024


def _resident(block, index):
    return pl.BlockSpec(block, lambda *_: index, pipeline_mode=pl.Buffered(1))


def _layer_norm(y, g, b):
    mu = jnp.mean(y, axis=-1, keepdims=True)
    d = y - mu
    var = jnp.mean(d * d, axis=-1, keepdims=True)
    return d * lax.rsqrt(var + LN_EPS) * g + b


def _dot(a, b):
    return jnp.dot(a, b, preferred_element_type=F32)


def _row_chunks(src_hbm, dst_scr, rows):
    assert src_hbm.shape == dst_scr.shape and src_hbm.shape[0] % rows == 0
    return [(src_hbm.at[pl.ds(r0, rows), :], dst_scr.at[pl.ds(r0, rows), :])
            for r0 in range(0, src_hbm.shape[0], rows)]


def _load_as_bf16(jobs):
    def copy(job, j):
        chunks, stage, sem = job
        slot = j % stage.shape[0]
        assert chunks[j][0].shape == stage.shape[1:] and stage.shape[1] % BF16_SUBLANES == 0
        return pltpu.make_async_copy(chunks[j][0], stage.at[slot], sem.at[slot])

    for job in jobs:
        for j in range(min(job[1].shape[0], len(job[0]))):
            copy(job, j).start()
    for j in range(max(len(job[0]) for job in jobs)):
        for job in jobs:
            chunks, stage, _ = job
            if j < len(chunks):
                copy(job, j).wait()
                chunks[j][1][...] = stage[j % stage.shape[0]].astype(BF16)
                if j + stage.shape[0] < len(chunks):
                    copy(job, j + stage.shape[0]).start()


def _ffn_ln_kernel(x_ref, wup_hbm, wdn_hbm, g_ref, b_ref, o_ref, wup_ref, wdn_ref,
                   up_stage, dn_stage, sems, *, alpha, d_ff, layer):
    @pl.when(pl.program_id(0) == 0)
    def _():
        _load_as_bf16([
            (_row_chunks(wup_hbm.at[layer], wup_ref, up_stage.shape[1]), up_stage, sems.at[0]),
            (_row_chunks(wdn_hbm.at[layer], wdn_ref, dn_stage.shape[1]), dn_stage, sems.at[1])])

    items = [(r0, c0) for r0 in range(0, x_ref.shape[0], SUB_TILE)
             for c0 in range(0, d_ff, FF_CHUNK)]
    xb_of = {}

    def gate_up(r0, c0):
        if r0 not in xb_of:
            xb_of[r0] = x_ref[r0:r0 + SUB_TILE, :].astype(BF16)
        xb = xb_of[r0]
        return (_dot(xb, wup_ref[:, c0:c0 + FF_CHUNK]),
                _dot(xb, wup_ref[:, d_ff + c0:d_ff + c0 + FF_CHUNK]))

    pending = gate_up(*items[0])
    acc = None
    for j, (r0, c0) in enumerate(items):
        gate, up = pending
        if j + 1 < len(items):
            pending = gate_up(*items[j + 1])
        h = (gate * jax.nn.sigmoid(gate) * up).astype(BF16)
        down = _dot(h, wdn_ref[c0:c0 + FF_CHUNK, :])
        acc = down if c0 == 0 else acc + down
        if c0 + FF_CHUNK == d_ff:
            rows = slice(r0, r0 + SUB_TILE)
            o_ref[rows, :] = _layer_norm(alpha * x_ref[rows, :] + 0.5 * acc,
                                         g_ref[...], b_ref[...])


def _ffn_ln(x, w_up, w_down, g, b, layer, alpha):
    t, d = x.shape
    d_ff = w_down.shape[1]
    assert t % FFN_TILE == 0 and FFN_TILE % SUB_TILE == 0 and d_ff % FF_CHUNK == 0
    up_rows, dn_rows = d // UP_CHUNKS, 2 * d_ff // UP_CHUNKS
    assert d % UP_CHUNKS == 0 and d_ff % dn_rows == 0
    return pl.pallas_call(
        functools.partial(_ffn_ln_kernel, alpha=alpha, d_ff=d_ff, layer=layer),
        grid=(t // FFN_TILE,),
        in_specs=[
            pl.BlockSpec((FFN_TILE, d), lambda i: (i, 0)),
            pl.BlockSpec(memory_space=pl.ANY),
            pl.BlockSpec(memory_space=pl.ANY),
            _resident((None, 1, d), (layer, 0, 0)),
            _resident((None, 1, d), (layer, 0, 0)),
        ],
        out_specs=pl.BlockSpec((FFN_TILE, d), lambda i: (i, 0)),
        out_shape=jax.ShapeDtypeStruct((t, d), F32),
        scratch_shapes=[
            pltpu.VMEM((d, 2 * d_ff), BF16),
            pltpu.VMEM((d_ff, d), BF16),
            pltpu.VMEM((WEIGHT_SLOTS, up_rows, 2 * d_ff), F32),
            pltpu.VMEM((WEIGHT_SLOTS, dn_rows, d), F32),
            pltpu.SemaphoreType.DMA((2, WEIGHT_SLOTS)),
        ],
        compiler_params=pltpu.CompilerParams(
            dimension_semantics=("arbitrary",), vmem_limit_bytes=VMEM_LIMIT),
        name="ffn_ln",
    )(x, w_up, w_down, g, b)


def _qkv_kernel(x_ref, wq_ref, wk_ref, wv_ref, k_ref, qvt_ref, kmean_ref, wqvt_scr, *, q_scale):
    a = k_ref.shape[1]

    @pl.when(pl.program_id(0) == 0)
    def _():
        for c0 in range(0, wq_ref.shape[0], COL_CHUNK):
            wqvt_scr[0:a, c0:c0 + COL_CHUNK] = wq_ref[c0:c0 + COL_CHUNK, :].T.astype(BF16)
            wqvt_scr[a:, c0:c0 + COL_CHUNK] = wv_ref[c0:c0 + COL_CHUNK, :].T.astype(BF16)

    for r0 in range(0, x_ref.shape[0], SUB_TILE):
        rows = slice(r0, r0 + SUB_TILE)
        xb = x_ref[rows, :].astype(BF16)
        k = _dot(xb, wk_ref[...])
        k_ref[rows, :] = k.astype(BF16)
        for j in range(SUB_TILE // MOBA_BLOCK):
            blk = k[j * MOBA_BLOCK:(j + 1) * MOBA_BLOCK]
            row = r0 // MOBA_BLOCK + j
            kmean_ref[0, row:row + 1, :] = jnp.sum(blk, axis=0, keepdims=True) * (1.0 / MOBA_BLOCK)
        qvt = lax.dot_general(wqvt_scr[...], xb, (((1,), (1,)), ((), ())),
                              preferred_element_type=F32)
        qvt_ref[0:a, rows] = (qvt[0:a] * q_scale).astype(BF16)
        qvt_ref[a:, rows] = qvt[a:].astype(BF16)


def _qkv_proj(x, w_in, a, layer):
    t, d = x.shape
    q_scale = (HEAD_DIM ** -0.5) * math.log2(math.e)
    assert t % TOKEN_TILE == 0 and TOKEN_TILE % SUB_TILE == 0 and SUB_TILE % MOBA_BLOCK == 0
    assert d % COL_CHUNK == 0
    n_tiles, blocks_per_tile = t // TOKEN_TILE, TOKEN_TILE // MOBA_BLOCK
    return pl.pallas_call(
        functools.partial(_qkv_kernel, q_scale=q_scale),
        grid=(n_tiles,),
        in_specs=[
            pl.BlockSpec((TOKEN_TILE, d), lambda i: (i, 0)),
            _resident((None, d, a), (layer, 0, 0)),
            _resident((None, d, a), (layer, 0, 1)),
            _resident((None, d, a), (layer, 0, 2)),
        ],
        out_specs=[
            pl.BlockSpec((TOKEN_TILE, a), lambda i: (i, 0)),
            pl.BlockSpec((2 * a, TOKEN_TILE), lambda i: (0, i)),
            pl.BlockSpec((1, blocks_per_tile, a), lambda i: (i, 0, 0)),
        ],
        out_shape=[
            jax.ShapeDtypeStruct((t, a), BF16),
            jax.ShapeDtypeStruct((2 * a, t), BF16),
            jax.ShapeDtypeStruct((n_tiles, blocks_per_tile, a), F32),
        ],
        scratch_shapes=[pltpu.VMEM((2 * a, d), BF16)],
        compiler_params=pltpu.CompilerParams(
            dimension_semantics=("arbitrary",), vmem_limit_bytes=VMEM_LIMIT),
        name="qkv_proj",
    )(x, w_in, w_in, w_in)


def _moba_kernel(qt_ref, k_ref, vt_ref, km_ref, o_ref, s_scr, vt1_scr, *, n_blocks, hd):
    blk = MOBA_BLOCK
    n_heads = qt_ref.shape[0] // hd
    blk_id = lax.broadcasted_iota(jnp.int32, (n_blocks, blk), 0)
    key_pos = lax.broadcasted_iota(jnp.int32, (blk, blk), 0)
    qry_pos = lax.broadcasted_iota(jnp.int32, (blk, blk), 1)
    causal = key_pos <= qry_pos

    def q_tile(h, qb):
        return qt_ref[h * hd:(h + 1) * hd, qb * blk:(qb + 1) * blk]

    def choice_bias(h, qb):
        km = km_ref[:, h * hd:(h + 1) * hd]
        km_hi = km.astype(BF16).astype(F32)
        km_split = jnp.concatenate([km_hi, km - km_hi], axis=0).astype(BF16)
        g2 = _dot(km_split, q_tile(h, qb))
        gate = g2[:n_blocks] + g2[n_blocks:]
        past = blk_id < qb
        gate = jnp.where(past, gate, -jnp.inf)
        beaten = jnp.zeros(gate.shape, jnp.int32)
        for m in range(qb):
            gm = gate[m:m + 1, :]
            wins = (gm > gate) | ((gm == gate) & (blk_id > m))
            beaten = beaten + wins.astype(jnp.int32)
        return jnp.where(past & (beaten < MOBA_TOPK), 0.0, -jnp.inf)

    biases = {(h, qb): choice_bias(h, qb)
              for h in range(n_heads) for qb in range(MOBA_TOPK + 1, n_blocks)}

    ones_rows = vt1_scr.shape[1] - hd
    for h in range(n_heads):
        vt1_scr[h, 0:hd, :] = vt_ref[h * hd:(h + 1) * hd, :]
        vt1_scr[h, hd:, :] = jnp.ones((ones_rows, vt1_scr.shape[2]), BF16)

    def slots(group):
        base, out = 0, []
        for qb in group:
            out.append(base)
            base += qb + 1
        assert base <= s_scr.shape[2]
        return out

    def items(group):
        per_head = [(qb, base + n, n) for qb, base in zip(group, slots(group))
                    for n in range(qb + 1)]
        return [(h,) + it for it in per_head for h in range(n_heads)]

    col_max8 = {}
    acc = {}

    def score_block(half, h, qb, slot, n):
        s = _dot(k_ref[n * blk:(n + 1) * blk, h * hd:(h + 1) * hd], q_tile(h, qb))
        if n == qb:
            s = jnp.where(causal, s, -jnp.inf)
        elif (h, qb) in biases:
            s = s + biases[h, qb][n:n + 1, :]
        s_scr[h, half, slot] = s
        mx = jnp.max(s.reshape(blk // F32_SUBLANES, F32_SUBLANES, blk), axis=0)
        col_max8[h, qb] = mx if n == 0 else jnp.maximum(col_max8[h, qb], mx)

    def weights_block(half, h, qb, slot, n):
        if n == 0:
            m_run = jnp.max(col_max8[h, qb], axis=0, keepdims=True)
            col_max8[h, qb] = jnp.broadcast_to(m_run, (F32_SUBLANES, blk))
        s = s_scr[h, half, slot].reshape(blk // F32_SUBLANES, F32_SUBLANES, blk)
        return jnp.exp2(s - col_max8[h, qb]).reshape(blk, blk).astype(BF16)

    def pv_block(p, h, qb, n):
        pv = _dot(vt1_scr[h, :, n * blk:(n + 1) * blk], p)
        acc[h, qb] = pv if n == 0 else acc[h, qb] + pv
        if n == qb:
            a = acc.pop((h, qb))
            out = (a[:hd] * (1.0 / a[hd:hd + 1])).T
            o_ref[qb * blk:(qb + 1) * blk, h * hd:(h + 1) * hd] = out.astype(o_ref.dtype)

    first = MOBA_TOPK
    rest = [qb for qb in range(n_blocks - 1, -1, -1) if qb != first]
    groups = [[first]]
    while rest:
        big = rest.pop(0)
        small = [qb for qb in rest if big + qb + 2 <= n_blocks][:1]
        for qb in small:
            rest.remove(qb)
        groups.append([big] + small)
    for item in items(groups[0]):
        score_block(0, *item)
    ready = []
    for j, group in enumerate(groups):
        cur = items(group)
        nxt = items(groups[j + 1]) if j + 1 < len(groups) else []
        for t in range(max(len(cur), len(nxt))):
            if t < len(nxt):
                score_block((j + 1) % 2, *nxt[t])
            if len(ready) > PV_LAG or (ready and t >= len(cur)):
                pv_block(*ready.pop(0))
            if t < len(cur):
                h, qb, slot, n = cur[t]
                ready.append((weights_block(j % 2, h, qb, slot, n), h, qb, n))
    while ready:
        pv_block(*ready.pop(0))


def _moba_attention(k, qvt, kmean, batch, seq):
    t, a = k.shape
    hd = HEAD_DIM
    n_heads = a // hd
    n_blocks = seq // MOBA_BLOCK
    assert seq % MOBA_BLOCK == 0 and kmean.shape == (batch, n_blocks, a)
    assert n_heads % ATTN_HEADS == 0
    hw = ATTN_HEADS * hd
    steps = n_heads // ATTN_HEADS
    return pl.pallas_call(
        functools.partial(_moba_kernel, n_blocks=n_blocks, hd=hd),
        grid=(batch, steps),
        in_specs=[
            pl.BlockSpec((hw, seq), lambda b, h: (h, b)),
            pl.BlockSpec((seq, hw), lambda b, h: (b, h)),
            pl.BlockSpec((hw, seq), lambda b, h: (steps + h, b)),
            pl.BlockSpec((None, n_blocks, hw), lambda b, h: (b, 0, h)),
        ],
        out_specs=pl.BlockSpec((seq, hw), lambda b, h: (b, h)),
        out_shape=jax.ShapeDtypeStruct((t, a), BF16),
        scratch_shapes=[
            pltpu.VMEM((ATTN_HEADS, 2, n_blocks, MOBA_BLOCK, MOBA_BLOCK), F32),
            pltpu.VMEM((ATTN_HEADS, hd + BF16_SUBLANES, seq), BF16),
        ],
        compiler_params=pltpu.CompilerParams(
            dimension_semantics=("parallel", "parallel"), vmem_limit_bytes=VMEM_LIMIT),
        name="moba_attn",
    )(qvt, k, qvt, kmean)


def _tail_kernel(x_ref, a_ref, win_hbm, bg_ref, cw_ref, cb_ref, wpa_hbm, wpc_hbm, wo_hbm,
                 g_ref, b_ref, o_ref, u_scr, c_scr, mix_scr, wc_ref, wg_ref, wpa_ref, wpc_ref,
                 wo_ref, c_stage, g_stage, p_stage, sems, *, alpha, tiles_per_seq, layer):
    tm, d = x_ref.shape
    a = a_ref.shape[1]
    i = pl.program_id(0)

    @pl.when(i == 0)
    def _():
        n_conv, n_gate = wc_ref.shape[1], wg_ref.shape[1]
        conv_cols = win_hbm.at[layer, pl.ds(0, d), pl.ds(3 * a, n_conv)]
        gate_cols = win_hbm.at[layer, pl.ds(0, d), pl.ds(3 * a + n_conv, n_gate)]
        square = [pair for src, dst in ((wpa_hbm, wpa_ref), (wpc_hbm, wpc_ref), (wo_hbm, wo_ref))
                  for pair in _row_chunks(src.at[layer], dst, p_stage.shape[1])]
        _load_as_bf16([
            (_row_chunks(conv_cols, wc_ref, c_stage.shape[1]), c_stage, sems.at[0]),
            (_row_chunks(gate_cols, wg_ref, g_stage.shape[1]), g_stage, sems.at[1]),
            (square, p_stage, sems.at[2])])

    @pl.when(i % tiles_per_seq == 0)
    def _():
        u_scr[...] = jnp.zeros(u_scr.shape, F32)

    row = lax.broadcasted_iota(jnp.int32, (SUB_TILE, COL_CHUNK), 0)

    def conv_phase(r0):
        rows = slice(r0, r0 + SUB_TILE)
        xb = x_ref[rows, :].astype(BF16)
        for c0 in range(0, d, COL_CHUNK):
            cs = slice(c0, c0 + COL_CHUNK)
            h = _dot(xb, wc_ref[:, c0:c0 + COL_CHUNK])
            g_b = _dot(xb, wc_ref[:, d + c0:d + c0 + COL_CHUNK])
            g_c = _dot(xb, wc_ref[:, 2 * d + c0:2 * d + c0 + COL_CHUNK])
            u = g_c * h
            before = u_scr[:, cs]
            u_scr[:, cs] = u[SUB_TILE - CONV_HALO:, :]
            last, last2 = before[CONV_HALO - 1:CONV_HALO, :], before[CONV_HALO - 2:CONV_HALO - 1, :]
            u1 = jnp.where(row == 0, last, pltpu.roll(u, 1, 0))
            u2 = jnp.where(row == 0, last2, jnp.where(row == 1, last, pltpu.roll(u, 2, 0)))
            y = (cw_ref[0:1, cs] * u2 + cw_ref[1:2, cs] * u1 + cw_ref[2:3, cs] * u
                 + cb_ref[:, cs])
            c_scr[rows, cs] = (g_b * y).astype(BF16)

    def merge_phase(r0):
        rows = slice(r0, r0 + SUB_TILE)
        xb = x_ref[rows, :].astype(BF16)
        att = a_ref[rows, :]
        cc = c_scr[rows, :]
        for c0 in range(0, d, COL_CHUNK):
            cs = slice(c0, c0 + COL_CHUNK)
            y_attn = _dot(att, wpa_ref[:, cs])
            y_conv = _dot(cc, wpc_ref[:, cs])
            gate_a = jax.nn.sigmoid(_dot(xb, wg_ref[:, c0:c0 + COL_CHUNK]) + bg_ref[:, cs])
            gate_c = jax.nn.sigmoid(_dot(xb, wg_ref[:, d + c0:d + c0 + COL_CHUNK])
                                    + bg_ref[:, d + c0:d + c0 + COL_CHUNK])
            mix_scr[rows, cs] = (gate_a * y_attn + gate_c * y_conv).astype(BF16)

    def out_phase(r0):
        rows = slice(r0, r0 + SUB_TILE)
        mixed = _dot(mix_scr[rows, :], wo_ref[...])
        o_ref[rows, :] = _layer_norm(alpha * x_ref[rows, :] + mixed, g_ref[...], b_ref[...])

    conv_phase(0)
    for r0 in range(0, tm, SUB_TILE):
        merge_phase(r0)
        if r0 + SUB_TILE < tm:
            conv_phase(r0 + SUB_TILE)
        out_phase(r0)


def _mixer_tail(x, attn, w_in, b_gate, conv_w, conv_b, w_pa, w_pc, w_out, g, b, layer, alpha, seq):
    t, d = x.shape
    a = attn.shape[1]
    assert t % TAIL_TILE == 0 and seq % TAIL_TILE == 0 and d % COL_CHUNK == 0
    assert TAIL_TILE % SUB_TILE == 0 and d % TAIL_WEIGHT_CHUNKS == 0
    assert w_pa.shape[1:] == (a, d) and a == d
    assert w_pc.shape[1:] == (d, d) and w_out.shape[1:] == (d, d)
    assert conv_w.shape[1:] == (CONV_K, d) and w_in.shape[2] == 3 * a + 5 * d
    tile = lambda i: (i, 0)
    lyr = (layer, 0, 0)
    hbm = pl.BlockSpec(memory_space=pl.ANY)
    return pl.pallas_call(
        functools.partial(_tail_kernel, alpha=alpha, tiles_per_seq=seq // TAIL_TILE, layer=layer),
        grid=(t // TAIL_TILE,),
        in_specs=[
            pl.BlockSpec((TAIL_TILE, d), tile),
            pl.BlockSpec((TAIL_TILE, a), tile),
            hbm,
            _resident((None, 1, 2 * d), lyr),
            _resident((None, CONV_K, d), lyr),
            _resident((None, 1, d), lyr),
            hbm, hbm, hbm,
            _resident((None, 1, d), lyr),
            _resident((None, 1, d), lyr),
        ],
        out_specs=pl.BlockSpec((TAIL_TILE, d), tile),
        out_shape=jax.ShapeDtypeStruct((t, d), F32),
        scratch_shapes=[
            pltpu.VMEM((CONV_HALO, d), F32),
            pltpu.VMEM((TAIL_TILE, d), BF16),
            pltpu.VMEM((TAIL_TILE, d), BF16),
            pltpu.VMEM((d, 3 * d), BF16),
            pltpu.VMEM((d, 2 * d), BF16),
            pltpu.VMEM((a, d), BF16),
            pltpu.VMEM((d, d), BF16),
            pltpu.VMEM((d, d), BF16),
            pltpu.VMEM((WEIGHT_SLOTS, d // TAIL_WEIGHT_CHUNKS, 3 * d), F32),
            pltpu.VMEM((WEIGHT_SLOTS, d // TAIL_WEIGHT_CHUNKS, 2 * d), F32),
            pltpu.VMEM((WEIGHT_SLOTS, 2 * d // TAIL_WEIGHT_CHUNKS, d), F32),
            pltpu.SemaphoreType.DMA((3, WEIGHT_SLOTS)),
        ],
        compiler_params=pltpu.CompilerParams(
            dimension_semantics=("arbitrary",), vmem_limit_bytes=VMEM_LIMIT),
        name="mixer_tail",
    )(x, attn, w_in, b_gate, conv_w, conv_b, w_pa, w_pc, w_out, g, b)


def kernel(x, ffn1_w_up, ffn1_w_down, ln1_g, ln1_b, w_in, b_gate, conv_w, conv_b,
           w_proj_attn, w_proj_conv, w_out, ln2_g, ln2_b, ffn2_w_up, ffn2_w_down,
           ln3_g, ln3_b):
    batch, seq, d = x.shape
    depth = w_in.shape[0]
    a = N_HEADS * HEAD_DIM
    assert w_in.shape[2] == 3 * a + 3 * d + 2 * d and w_proj_attn.shape[1] == a
    alpha = (2.0 * depth) ** 0.25
    rows = lambda v: v.reshape(depth, 1, -1)

    h = x.reshape(batch * seq, d)
    for l in range(depth):
        h = _ffn_ln(h, ffn1_w_up, ffn1_w_down, rows(ln1_g), rows(ln1_b), l, alpha)
        k, qvt, kmean = _qkv_proj(h, w_in, a, l)
        attn = _moba_attention(k, qvt, kmean.reshape(batch, seq // MOBA_BLOCK, a), batch, seq)
        h = _mixer_tail(h, attn, w_in, rows(b_gate), conv_w, rows(conv_b), w_proj_attn,
                        w_proj_conv, w_out, rows(ln2_g), rows(ln2_b), l, alpha, seq)
        h = _ffn_ln(h, ffn2_w_up, ffn2_w_down, rows(ln3_g), rows(ln3_b), l, alpha)
    return h.reshape(batch, seq, d)
```

```python
import functools
import math

import jax
import jax.numpy as jnp
from jax import lax
from jax.experimental import pallas as pl
from jax.experimental.pallas import tpu as pltpu

N_HEADS = 8
HEAD_DIM = 128
MOBA_BLOCK = 256
MOBA_TOPK = 3
CONV_K = 3
LN_EPS = 1e-5

F32 = jnp.float32
BF16 = jnp.bfloat16

TOKEN_TILE = 1024
FFN_TILE = 1024
TAIL_TILE = 512
SUB_TILE = 256
FF_CHUNK = 256
COL_CHUNK = 256
F32_SUBLANES = 8
BF16_SUBLANES = 16
CONV_HALO = F32_SUBLANES
UP_CHUNKS = 8
TAIL_WEIGHT_CHUNKS = 8
WEIGHT_SLOTS = 2
TILES_IN_FLIGHT = 8
ATTN_HEADS = 2
VMEM_LIMIT = 56 * 1024 * 1024


def _resident(block, index):
    return pl.BlockSpec(block, lambda *_: index, pipeline_mode=pl.Buffered(1))


def _layer_norm(y, g, b):
    mu = jnp.mean(y, axis=-1, keepdims=True)
    d = y - mu
    var = jnp.mean(d * d, axis=-1, keepdims=True)
    return d * lax.rsqrt(var + LN_EPS) * g + b


def _dot(a, b):
    return jnp.dot(a, b, preferred_element_type=F32)


def _row_chunks(src_hbm, dst_scr, rows):
    assert src_hbm.shape == dst_scr.shape and src_hbm.shape[0] % rows == 0
    return [(src_hbm.at[pl.ds(r0, rows), :], dst_scr.at[pl.ds(r0, rows), :])
            for r0 in range(0, src_hbm.shape[0], rows)]


def _load_as_bf16(jobs):
    def copy(job, j):
        chunks, stage, sem = job
        slot = j % stage.shape[0]
        assert chunks[j][0].shape == stage.shape[1:] and stage.shape[1] % BF16_SUBLANES == 0
        return pltpu.make_async_copy(chunks[j][0], stage.at[slot], sem.at[slot])

    for job in jobs:
        for j in range(min(job[1].shape[0], len(job[0]))):
            copy(job, j).start()
    for j in range(max(len(job[0]) for job in jobs)):
        for job in jobs:
            chunks, stage, _ = job
            if j < len(chunks):
                copy(job, j).wait()
                chunks[j][1][...] = stage[j % stage.shape[0]].astype(BF16)
                if j + stage.shape[0] < len(chunks):
                    copy(job, j + stage.shape[0]).start()


def _ffn_ln_kernel(x_ref, wup_hbm, wdn_hbm, g_ref, b_ref, o_ref, wup_ref, wdn_ref,
                   up_stage, dn_stage, sems, *, alpha, d_ff, layer):
    @pl.when(pl.program_id(0) == 0)
    def _():
        _load_as_bf16([
            (_row_chunks(wup_hbm.at[layer], wup_ref, up_stage.shape[1]), up_stage, sems.at[0]),
            (_row_chunks(wdn_hbm.at[layer], wdn_ref, dn_stage.shape[1]), dn_stage, sems.at[1])])

    items = [(r0, c0) for r0 in range(0, x_ref.shape[0], SUB_TILE)
             for c0 in range(0, d_ff, FF_CHUNK)]
    xb_of = {}

    def gate_up(r0, c0):
        if r0 not in xb_of:
            xb_of[r0] = x_ref[r0:r0 + SUB_TILE, :].astype(BF16)
        xb = xb_of[r0]
        return (_dot(xb, wup_ref[:, c0:c0 + FF_CHUNK]),
                _dot(xb, wup_ref[:, d_ff + c0:d_ff + c0 + FF_CHUNK]))

    pending = gate_up(*items[0])
    acc = None
    for j, (r0, c0) in enumerate(items):
        gate, up = pending
        if j + 1 < len(items):
            pending = gate_up(*items[j + 1])
        h = (gate * jax.nn.sigmoid(gate) * up).astype(BF16)
        down = _dot(h, wdn_ref[c0:c0 + FF_CHUNK, :])
        acc = down if c0 == 0 else acc + down
        if c0 + FF_CHUNK == d_ff:
            rows = slice(r0, r0 + SUB_TILE)
            o_ref[rows, :] = _layer_norm(alpha * x_ref[rows, :] + 0.5 * acc,
                                         g_ref[...], b_ref[...])


def _ffn_ln(x, w_up, w_down, g, b, layer, alpha):
    t, d = x.shape
    d_ff = w_down.shape[1]
    assert t % FFN_TILE == 0 and FFN_TILE % SUB_TILE == 0 and d_ff % FF_CHUNK == 0
    up_rows, dn_rows = d // UP_CHUNKS, 2 * d_ff // UP_CHUNKS
    assert d % UP_CHUNKS == 0 and d_ff % dn_rows == 0
    return pl.pallas_call(
        functools.partial(_ffn_ln_kernel, alpha=alpha, d_ff=d_ff, layer=layer),
        grid=(t // FFN_TILE,),
        in_specs=[
            pl.BlockSpec((FFN_TILE, d), lambda i: (i, 0)),
            pl.BlockSpec(memory_space=pl.ANY),
            pl.BlockSpec(memory_space=pl.ANY),
            _resident((None, 1, d), (layer, 0, 0)),
            _resident((None, 1, d), (layer, 0, 0)),
        ],
        out_specs=pl.BlockSpec((FFN_TILE, d), lambda i: (i, 0)),
        out_shape=jax.ShapeDtypeStruct((t, d), F32),
        scratch_shapes=[
            pltpu.VMEM((d, 2 * d_ff), BF16),
            pltpu.VMEM((d_ff, d), BF16),
            pltpu.VMEM((WEIGHT_SLOTS, up_rows, 2 * d_ff), F32),
            pltpu.VMEM((WEIGHT_SLOTS, dn_rows, d), F32),
            pltpu.SemaphoreType.DMA((2, WEIGHT_SLOTS)),
        ],
        compiler_params=pltpu.CompilerParams(
            dimension_semantics=("arbitrary",), vmem_limit_bytes=VMEM_LIMIT),
        name="ffn_ln",
    )(x, w_up, w_down, g, b)


def _qkv_kernel(x_ref, wq_ref, wk_ref, wv_ref, k_ref, qvt_ref, kmean_ref, wqvt_scr, *, q_scale):
    a = k_ref.shape[1]

    @pl.when(pl.program_id(0) == 0)
    def _():
        for c0 in range(0, wq_ref.shape[0], COL_CHUNK):
            wqvt_scr[0:a, c0:c0 + COL_CHUNK] = wq_ref[c0:c0 + COL_CHUNK, :].T.astype(BF16)
            wqvt_scr[a:, c0:c0 + COL_CHUNK] = wv_ref[c0:c0 + COL_CHUNK, :].T.astype(BF16)

    for r0 in range(0, x_ref.shape[0], SUB_TILE):
        rows = slice(r0, r0 + SUB_TILE)
        xb = x_ref[rows, :].astype(BF16)
        k = _dot(xb, wk_ref[...])
        k_ref[rows, :] = k.astype(BF16)
        for j in range(SUB_TILE // MOBA_BLOCK):
            blk = k[j * MOBA_BLOCK:(j + 1) * MOBA_BLOCK]
            row = r0 // MOBA_BLOCK + j
            kmean_ref[0, row:row + 1, :] = jnp.sum(blk, axis=0, keepdims=True) * (1.0 / MOBA_BLOCK)
        qvt = lax.dot_general(wqvt_scr[...], xb, (((1,), (1,)), ((), ())),
                              preferred_element_type=F32)
        qvt_ref[0:a, rows] = (qvt[0:a] * q_scale).astype(BF16)
        qvt_ref[a:, rows] = qvt[a:].astype(BF16)


def _qkv_proj(x, w_in, a, layer):
    t, d = x.shape
    q_scale = (HEAD_DIM ** -0.5) * math.log2(math.e)
    assert t % TOKEN_TILE == 0 and TOKEN_TILE % SUB_TILE == 0 and SUB_TILE % MOBA_BLOCK == 0
    assert d % COL_CHUNK == 0
    n_tiles, blocks_per_tile = t // TOKEN_TILE, TOKEN_TILE // MOBA_BLOCK
    return pl.pallas_call(
        functools.partial(_qkv_kernel, q_scale=q_scale),
        grid=(n_tiles,),
        in_specs=[
            pl.BlockSpec((TOKEN_TILE, d), lambda i: (i, 0)),
            _resident((None, d, a), (layer, 0, 0)),
            _resident((None, d, a), (layer, 0, 1)),
            _resident((None, d, a), (layer, 0, 2)),
        ],
        out_specs=[
            pl.BlockSpec((TOKEN_TILE, a), lambda i: (i, 0)),
            pl.BlockSpec((2 * a, TOKEN_TILE), lambda i: (0, i)),
            pl.BlockSpec((1, blocks_per_tile, a), lambda i: (i, 0, 0)),
        ],
        out_shape=[
            jax.ShapeDtypeStruct((t, a), BF16),
            jax.ShapeDtypeStruct((2 * a, t), BF16),
            jax.ShapeDtypeStruct((n_tiles, blocks_per_tile, a), F32),
        ],
        scratch_shapes=[pltpu.VMEM((2 * a, d), BF16)],
        compiler_params=pltpu.CompilerParams(
            dimension_semantics=("arbitrary",), vmem_limit_bytes=VMEM_LIMIT),
        name="qkv_proj",
    )(x, w_in, w_in, w_in)


def _moba_kernel(qt_ref, k_ref, vt_ref, km_ref, o_ref, vt1_scr, *, n_blocks, hd):
    blk = MOBA_BLOCK
    n_heads = qt_ref.shape[0] // hd
    blk_id = lax.broadcasted_iota(jnp.int32, (n_blocks, blk), 0)
    key_pos = lax.broadcasted_iota(jnp.int32, (blk, blk), 0)
    qry_pos = lax.broadcasted_iota(jnp.int32, (blk, blk), 1)
    causal = key_pos <= qry_pos

    def q_tile(h, qb):
        return qt_ref[h * hd:(h + 1) * hd, qb * blk:(qb + 1) * blk]

    def choice_bias(h, qb):
        km = km_ref[:, h * hd:(h + 1) * hd]
        km_hi = km.astype(BF16).astype(F32)
        km_split = jnp.concatenate([km_hi, km - km_hi], axis=0).astype(BF16)
        g2 = _dot(km_split, q_tile(h, qb))
        gate = g2[:n_blocks] + g2[n_blocks:]
        past = blk_id < qb
        gate = jnp.where(past, gate, -jnp.inf)
        beaten = jnp.zeros(gate.shape, jnp.int32)
        for m in range(qb):
            gm = gate[m:m + 1, :]
            wins = (gm > gate) | ((gm == gate) & (blk_id > m))
            beaten = beaten + wins.astype(jnp.int32)
        return jnp.where(past & (beaten < MOBA_TOPK), 0.0, -jnp.inf)

    biases = {(h, qb): choice_bias(h, qb)
              for h in range(n_heads) for qb in range(MOBA_TOPK + 1, n_blocks)}

    ones_rows = vt1_scr.shape[1] - hd
    for h in range(n_heads):
        vt1_scr[h, 0:hd, :] = vt_ref[h * hd:(h + 1) * hd, :]
        vt1_scr[h, hd:, :] = jnp.ones((ones_rows, vt1_scr.shape[2]), BF16)

    def scores(h, qb, n):
        s = _dot(k_ref[n * blk:(n + 1) * blk, h * hd:(h + 1) * hd], q_tile(h, qb))
        if n == qb:
            return jnp.where(causal, s, -jnp.inf)
        if (h, qb) in biases:
            return s + biases[h, qb][n:n + 1, :]
        return s

    def tile_stream(h, qb):
        order = [qb] + list(range(qb))
        s_next = scores(h, qb, order[0])
        m_run = acc = None
        for idx, n in enumerate(order):
            s = s_next
            if idx + 1 < len(order):
                s_next = scores(h, qb, order[idx + 1])
            m8 = jnp.max(s.reshape(blk // F32_SUBLANES, F32_SUBLANES, blk), axis=0)
            m_blk = jnp.max(m8, axis=0, keepdims=True)
            m_new = m_blk if m_run is None else jnp.maximum(m_run, m_blk)
            p = jnp.exp2(s - m_new).astype(BF16)
            pv = _dot(vt1_scr[h, :, n * blk:(n + 1) * blk], p)
            acc = pv if acc is None else acc * jnp.exp2(m_run - m_new) + pv
            m_run = m_new
            yield
        out = (acc[:hd] * (1.0 / acc[hd:hd + 1])).T
        o_ref[qb * blk:(qb + 1) * blk, h * hd:(h + 1) * hd] = out.astype(o_ref.dtype)

    tiles = [MOBA_TOPK] + [qb for qb in range(n_blocks - 1, -1, -1) if qb != MOBA_TOPK]
    waiting = [tile_stream(h, qb) for qb in tiles for h in range(n_heads)]
    running = []
    while waiting or running:
        while waiting and len(running) < TILES_IN_FLIGHT:
            running.append(waiting.pop(0))
        for stream in list(running):
            try:
                next(stream)
            except StopIteration:
                running.remove(stream)


def _moba_attention(k, qvt, kmean, batch, seq):
    t, a = k.shape
    hd = HEAD_DIM
    n_heads = a // hd
    n_blocks = seq // MOBA_BLOCK
    assert seq % MOBA_BLOCK == 0 and kmean.shape == (batch, n_blocks, a)
    assert n_heads % ATTN_HEADS == 0
    hw = ATTN_HEADS * hd
    steps = n_heads // ATTN_HEADS
    return pl.pallas_call(
        functools.partial(_moba_kernel, n_blocks=n_blocks, hd=hd),
        grid=(batch, steps),
        in_specs=[
            pl.BlockSpec((hw, seq), lambda b, h: (h, b)),
            pl.BlockSpec((seq, hw), lambda b, h: (b, h)),
            pl.BlockSpec((hw, seq), lambda b, h: (steps + h, b)),
            pl.BlockSpec((None, n_blocks, hw), lambda b, h: (b, 0, h)),
        ],
        out_specs=pl.BlockSpec((seq, hw), lambda b, h: (b, h)),
        out_shape=jax.ShapeDtypeStruct((t, a), BF16),
        scratch_shapes=[
            pltpu.VMEM((ATTN_HEADS, hd + BF16_SUBLANES, seq), BF16),
        ],
        compiler_params=pltpu.CompilerParams(
            dimension_semantics=("parallel", "parallel"), vmem_limit_bytes=VMEM_LIMIT),
        name="moba_attn",
    )(qvt, k, qvt, kmean)


def _tail_kernel(x_ref, a_ref, win_hbm, bg_ref, cw_ref, cb_ref, wpa_hbm, wpc_hbm, wo_hbm,
                 g_ref, b_ref, o_ref, u_scr, c_scr, mix_scr, wc_ref, wg_ref, wpa_ref, wpc_ref,
                 wo_ref, c_stage, g_stage, p_stage, sems, *, alpha, tiles_per_seq, layer):
    tm, d = x_ref.shape
    a = a_ref.shape[1]
    i = pl.program_id(0)

    @pl.when(i == 0)
    def _():
        n_conv, n_gate = wc_ref.shape[1], wg_ref.shape[1]
        conv_cols = win_hbm.at[layer, pl.ds(0, d), pl.ds(3 * a, n_conv)]
        gate_cols = win_hbm.at[layer, pl.ds(0, d), pl.ds(3 * a + n_conv, n_gate)]
        square = [pair for src, dst in ((wpa_hbm, wpa_ref), (wpc_hbm, wpc_ref), (wo_hbm, wo_ref))
                  for pair in _row_chunks(src.at[layer], dst, p_stage.shape[1])]
        _load_as_bf16([
            (_row_chunks(conv_cols, wc_ref, c_stage.shape[1]), c_stage, sems.at[0]),
            (_row_chunks(gate_cols, wg_ref, g_stage.shape[1]), g_stage, sems.at[1]),
            (square, p_stage, sems.at[2])])

    @pl.when(i % tiles_per_seq == 0)
    def _():
        u_scr[...] = jnp.zeros(u_scr.shape, F32)

    row = lax.broadcasted_iota(jnp.int32, (SUB_TILE, COL_CHUNK), 0)

    def conv_phase(r0):
        rows = slice(r0, r0 + SUB_TILE)
        xb = x_ref[rows, :].astype(BF16)
        for c0 in range(0, d, COL_CHUNK):
            cs = slice(c0, c0 + COL_CHUNK)
            h = _dot(xb, wc_ref[:, c0:c0 + COL_CHUNK])
            g_b = _dot(xb, wc_ref[:, d + c0:d + c0 + COL_CHUNK])
            g_c = _dot(xb, wc_ref[:, 2 * d + c0:2 * d + c0 + COL_CHUNK])
            u = g_c * h
            before = u_scr[:, cs]
            u_scr[:, cs] = u[SUB_TILE - CONV_HALO:, :]
            last, last2 = before[CONV_HALO - 1:CONV_HALO, :], before[CONV_HALO - 2:CONV_HALO - 1, :]
            u1 = jnp.where(row == 0, last, pltpu.roll(u, 1, 0))
            u2 = jnp.where(row == 0, last2, jnp.where(row == 1, last, pltpu.roll(u, 2, 0)))
            y = (cw_ref[0:1, cs] * u2 + cw_ref[1:2, cs] * u1 + cw_ref[2:3, cs] * u
                 + cb_ref[:, cs])
            c_scr[rows, cs] = (g_b * y).astype(BF16)

    def merge_phase(r0):
        rows = slice(r0, r0 + SUB_TILE)
        xb = x_ref[rows, :].astype(BF16)
        att = a_ref[rows, :]
        cc = c_scr[rows, :]
        for c0 in range(0, d, COL_CHUNK):
            cs = slice(c0, c0 + COL_CHUNK)
            y_attn = _dot(att, wpa_ref[:, cs])
            y_conv = _dot(cc, wpc_ref[:, cs])
            gate_a = jax.nn.sigmoid(_dot(xb, wg_ref[:, c0:c0 + COL_CHUNK]) + bg_ref[:, cs])
            gate_c = jax.nn.sigmoid(_dot(xb, wg_ref[:, d + c0:d + c0 + COL_CHUNK])
                                    + bg_ref[:, d + c0:d + c0 + COL_CHUNK])
            mix_scr[rows, cs] = (gate_a * y_attn + gate_c * y_conv).astype(BF16)

    def out_phase(r0):
        rows = slice(r0, r0 + SUB_TILE)
        mixed = _dot(mix_scr[rows, :], wo_ref[...])
        o_ref[rows, :] = _layer_norm(alpha * x_ref[rows, :] + mixed, g_ref[...], b_ref[...])

    conv_phase(0)
    for r0 in range(0, tm, SUB_TILE):
        merge_phase(r0)
        if r0 + SUB_TILE < tm:
            conv_phase(r0 + SUB_TILE)
        out_phase(r0)


def _mixer_tail(x, attn, w_in, b_gate, conv_w, conv_b, w_pa, w_pc, w_out, g, b, layer, alpha, seq):
    t, d = x.shape
    a = attn.shape[1]
    assert t % TAIL_TILE == 0 and seq % TAIL_TILE == 0 and d % COL_CHUNK == 0
    assert TAIL_TILE % SUB_TILE == 0 and d % TAIL_WEIGHT_CHUNKS == 0
    assert w_pa.shape[1:] == (a, d) and a == d
    assert w_pc.shape[1:] == (d, d) and w_out.shape[1:] == (d, d)
    assert conv_w.shape[1:] == (CONV_K, d) and w_in.shape[2] == 3 * a + 5 * d
    tile = lambda i: (i, 0)
    lyr = (layer, 0, 0)
    hbm = pl.BlockSpec(memory_space=pl.ANY)
    return pl.pallas_call(
        functools.partial(_tail_kernel, alpha=alpha, tiles_per_seq=seq // TAIL_TILE, layer=layer),
        grid=(t // TAIL_TILE,),
        in_specs=[
            pl.BlockSpec((TAIL_TILE, d), tile),
            pl.BlockSpec((TAIL_TILE, a), tile),
            hbm,
            _resident((None, 1, 2 * d), lyr),
            _resident((None, CONV_K, d), lyr),
            _resident((None, 1, d), lyr),
            hbm, hbm, hbm,
            _resident((None, 1, d), lyr),
            _resident((None, 1, d), lyr),
        ],
        out_specs=pl.BlockSpec((TAIL_TILE, d), tile),
        out_shape=jax.ShapeDtypeStruct((t, d), F32),
        scratch_shapes=[
            pltpu.VMEM((CONV_HALO, d), F32),
            pltpu.VMEM((TAIL_TILE, d), BF16),
            pltpu.VMEM((TAIL_TILE, d), BF16),
            pltpu.VMEM((d, 3 * d), BF16),
            pltpu.VMEM((d, 2 * d), BF16),
            pltpu.VMEM((a, d), BF16),
            pltpu.VMEM((d, d), BF16),
            pltpu.VMEM((d, d), BF16),
            pltpu.VMEM((WEIGHT_SLOTS, d // TAIL_WEIGHT_CHUNKS, 3 * d), F32),
            pltpu.VMEM((WEIGHT_SLOTS, d // TAIL_WEIGHT_CHUNKS, 2 * d), F32),
            pltpu.VMEM((WEIGHT_SLOTS, 2 * d // TAIL_WEIGHT_CHUNKS, d), F32),
            pltpu.SemaphoreType.DMA((3, WEIGHT_SLOTS)),
        ],
        compiler_params=pltpu.CompilerParams(
            dimension_semantics=("arbitrary",), vmem_limit_bytes=VMEM_LIMIT),
        name="mixer_tail",
    )(x, attn, w_in, b_gate, conv_w, conv_b, w_pa, w_pc, w_out, g, b)


def kernel(x, ffn1_w_up, ffn1_w_down, ln1_g, ln1_b, w_in, b_gate, conv_w, conv_b,
           w_proj_attn, w_proj_conv, w_out, ln2_g, ln2_b, ffn2_w_up, ffn2_w_down,
           ln3_g, ln3_b):
    batch, seq, d = x.shape
    depth = w_in.shape[0]
    a = N_HEADS * HEAD_DIM
    assert w_in.shape[2] == 3 * a + 3 * d + 2 * d and w_proj_attn.shape[1] == a
    alpha = (2.0 * depth) ** 0.25
    rows = lambda v: v.reshape(depth, 1, -1)

    h = x.reshape(batch * seq, d)
    for l in range(depth):
        h = _ffn_ln(h, ffn1_w_up, ffn1_w_down, rows(ln1_g), rows(ln1_b), l, alpha)
        k, qvt, kmean = _qkv_proj(h, w_in, a, l)
        attn = _moba_attention(k, qvt, kmean.reshape(batch, seq // MOBA_BLOCK, a), batch, seq)
        h = _mixer_tail(h, attn, w_in, rows(b_gate), conv_w, rows(conv_b), w_proj_attn,
                        w_proj_conv, w_out, rows(ln2_g), rows(ln2_b), l, alpha, seq)
        h = _ffn_ln(h, ffn2_w_up, ffn2_w_down, rows(ln3_g), rows(ln3_b), l, alpha)
    return h.reshape(batch, seq, d)
```

```python
import functools
import math

import jax
import jax.numpy as jnp
from jax import lax
from jax.experimental import pallas as pl
from jax.experimental.pallas import tpu as pltpu

N_HEADS = 8
HEAD_DIM = 128
MOBA_BLOCK = 256
MOBA_TOPK = 3
CONV_K = 3
LN_EPS = 1e-5

F32 = jnp.float32
BF16 = jnp.bfloat16

TOKEN_TILE = 1024
FFN_TILE = 1024
TAIL_TILE = 512
SUB_TILE = 256
FF_CHUNK = 256
COL_CHUNK = 256
F32_SUBLANES = 8
BF16_SUBLANES = 16
CONV_HALO = F32_SUBLANES
UP_CHUNKS = 8
TAIL_WEIGHT_CHUNKS = 8
WEIGHT_SLOTS = 2
PV_LAG = 4
ATTN_HEADS = 2
VMEM_LIMIT = 56 * 1024 * 1024


def _resident(block, index):
    return pl.BlockSpec(block, lambda *_: index, pipeline_mode=pl.Buffered(1))


def _layer_norm(y, g, b):
    mu = jnp.mean(y, axis=-1, keepdims=True)
    d = y - mu
    var = jnp.mean(d * d, axis=-1, keepdims=True)
    return d * lax.rsqrt(var + LN_EPS) * g + b


def _dot(a, b):
    return jnp.dot(a, b, preferred_element_type=F32)


def _row_chunks(src_hbm, dst_scr, rows):
    assert src_hbm.shape == dst_scr.shape and src_hbm.shape[0] % rows == 0
    return [(src_hbm.at[pl.ds(r0, rows), :], dst_scr.at[pl.ds(r0, rows), :])
            for r0 in range(0, src_hbm.shape[0], rows)]


def _load_as_bf16(jobs):
    def copy(job, j):
        chunks, stage, sem = job
        slot = j % stage.shape[0]
        assert chunks[j][0].shape == stage.shape[1:] and stage.shape[1] % BF16_SUBLANES == 0
        return pltpu.make_async_copy(chunks[j][0], stage.at[slot], sem.at[slot])

    for job in jobs:
        for j in range(min(job[1].shape[0], len(job[0]))):
            copy(job, j).start()
    for j in range(max(len(job[0]) for job in jobs)):
        for job in jobs:
            chunks, stage, _ = job
            if j < len(chunks):
                copy(job, j).wait()
                chunks[j][1][...] = stage[j % stage.shape[0]].astype(BF16)
                if j + stage.shape[0] < len(chunks):
                    copy(job, j + stage.shape[0]).start()


def _ffn_ln_kernel(x_ref, wup_hbm, wdn_hbm, g_ref, b_ref, o_ref, wup_ref, wdn_ref,
                   up_stage, dn_stage, sems, *, alpha, d_ff, layer):
    @pl.when(pl.program_id(0) == 0)
    def _():
        _load_as_bf16([
            (_row_chunks(wup_hbm.at[layer], wup_ref, up_stage.shape[1]), up_stage, sems.at[0]),
            (_row_chunks(wdn_hbm.at[layer], wdn_ref, dn_stage.shape[1]), dn_stage, sems.at[1])])

    items = [(r0, c0) for r0 in range(0, x_ref.shape[0], SUB_TILE)
             for c0 in range(0, d_ff, FF_CHUNK)]
    xb_of = {}

    def gate_up(r0, c0):
        if r0 not in xb_of:
            xb_of[r0] = x_ref[r0:r0 + SUB_TILE, :].astype(BF16)
        xb = xb_of[r0]
        return (_dot(xb, wup_ref[:, c0:c0 + FF_CHUNK]),
                _dot(xb, wup_ref[:, d_ff + c0:d_ff + c0 + FF_CHUNK]))

    pending = gate_up(*items[0])
    acc = None
    for j, (r0, c0) in enumerate(items):
        gate, up = pending
        if j + 1 < len(items):
            pending = gate_up(*items[j + 1])
        h = (gate * jax.nn.sigmoid(gate) * up).astype(BF16)
        down = _dot(h, wdn_ref[c0:c0 + FF_CHUNK, :])
        acc = down if c0 == 0 else acc + down
        if c0 + FF_CHUNK == d_ff:
            rows = slice(r0, r0 + SUB_TILE)
            o_ref[rows, :] = _layer_norm(alpha * x_ref[rows, :] + 0.5 * acc,
                                         g_ref[...], b_ref[...])


def _ffn_ln(x, w_up, w_down, g, b, layer, alpha):
    t, d = x.shape
    d_ff = w_down.shape[1]
    assert t % FFN_TILE == 0 and FFN_TILE % SUB_TILE == 0 and d_ff % FF_CHUNK == 0
    up_rows, dn_rows = d // UP_CHUNKS, 2 * d_ff // UP_CHUNKS
    assert d % UP_CHUNKS == 0 and d_ff % dn_rows == 0
    return pl.pallas_call(
        functools.partial(_ffn_ln_kernel, alpha=alpha, d_ff=d_ff, layer=layer),
        grid=(t // FFN_TILE,),
        in_specs=[
            pl.BlockSpec((FFN_TILE, d), lambda i: (i, 0)),
            pl.BlockSpec(memory_space=pl.ANY),
            pl.BlockSpec(memory_space=pl.ANY),
            _resident((None, 1, d), (layer, 0, 0)),
            _resident((None, 1, d), (layer, 0, 0)),
        ],
        out_specs=pl.BlockSpec((FFN_TILE, d), lambda i: (i, 0)),
        out_shape=jax.ShapeDtypeStruct((t, d), F32),
        scratch_shapes=[
            pltpu.VMEM((d, 2 * d_ff), BF16),
            pltpu.VMEM((d_ff, d), BF16),
            pltpu.VMEM((WEIGHT_SLOTS, up_rows, 2 * d_ff), F32),
            pltpu.VMEM((WEIGHT_SLOTS, dn_rows, d), F32),
            pltpu.SemaphoreType.DMA((2, WEIGHT_SLOTS)),
        ],
        compiler_params=pltpu.CompilerParams(
            dimension_semantics=("arbitrary",), vmem_limit_bytes=VMEM_LIMIT),
        name="ffn_ln",
    )(x, w_up, w_down, g, b)


def _qkv_kernel(x_ref, wq_ref, wk_ref, wv_ref, k_ref, qvt_ref, kmean_ref, wqvt_scr, *, q_scale):
    a = k_ref.shape[1]

    @pl.when(pl.program_id(0) == 0)
    def _():
        for c0 in range(0, wq_ref.shape[0], COL_CHUNK):
            wqvt_scr[0:a, c0:c0 + COL_CHUNK] = wq_ref[c0:c0 + COL_CHUNK, :].T.astype(BF16)
            wqvt_scr[a:, c0:c0 + COL_CHUNK] = wv_ref[c0:c0 + COL_CHUNK, :].T.astype(BF16)

    for r0 in range(0, x_ref.shape[0], SUB_TILE):
        rows = slice(r0, r0 + SUB_TILE)
        xb = x_ref[rows, :].astype(BF16)
        k = _dot(xb, wk_ref[...])
        k_ref[rows, :] = k.astype(BF16)
        for j in range(SUB_TILE // MOBA_BLOCK):
            blk = k[j * MOBA_BLOCK:(j + 1) * MOBA_BLOCK]
            row = r0 // MOBA_BLOCK + j
            kmean_ref[0, row:row + 1, :] = jnp.sum(blk, axis=0, keepdims=True) * (1.0 / MOBA_BLOCK)
        qvt = lax.dot_general(wqvt_scr[...], xb, (((1,), (1,)), ((), ())),
                              preferred_element_type=F32)
        qvt_ref[0:a, rows] = (qvt[0:a] * q_scale).astype(BF16)
        qvt_ref[a:, rows] = qvt[a:].astype(BF16)


def _qkv_proj(x, w_in, a, layer):
    t, d = x.shape
    q_scale = (HEAD_DIM ** -0.5) * math.log2(math.e)
    assert t % TOKEN_TILE == 0 and TOKEN_TILE % SUB_TILE == 0 and SUB_TILE % MOBA_BLOCK == 0
    assert d % COL_CHUNK == 0
    n_tiles, blocks_per_tile = t // TOKEN_TILE, TOKEN_TILE // MOBA_BLOCK
    return pl.pallas_call(
        functools.partial(_qkv_kernel, q_scale=q_scale),
        grid=(n_tiles,),
        in_specs=[
            pl.BlockSpec((TOKEN_TILE, d), lambda i: (i, 0)),
            _resident((None, d, a), (layer, 0, 0)),
            _resident((None, d, a), (layer, 0, 1)),
            _resident((None, d, a), (layer, 0, 2)),
        ],
        out_specs=[
            pl.BlockSpec((TOKEN_TILE, a), lambda i: (i, 0)),
            pl.BlockSpec((2 * a, TOKEN_TILE), lambda i: (0, i)),
            pl.BlockSpec((1, blocks_per_tile, a), lambda i: (i, 0, 0)),
        ],
        out_shape=[
            jax.ShapeDtypeStruct((t, a), BF16),
            jax.ShapeDtypeStruct((2 * a, t), BF16),
            jax.ShapeDtypeStruct((n_tiles, blocks_per_tile, a), F32),
        ],
        scratch_shapes=[pltpu.VMEM((2 * a, d), BF16)],
        compiler_params=pltpu.CompilerParams(
            dimension_semantics=("arbitrary",), vmem_limit_bytes=VMEM_LIMIT),
        name="qkv_proj",
    )(x, w_in, w_in, w_in)


def _moba_kernel(qt_ref, k_ref, vt_ref, km_ref, o_ref, s_scr, vt1_scr, *, n_blocks, hd):
    blk = MOBA_BLOCK
    n_heads = qt_ref.shape[0] // hd
    blk_id = lax.broadcasted_iota(jnp.int32, (n_blocks, blk), 0)
    key_pos = lax.broadcasted_iota(jnp.int32, (blk, blk), 0)
    qry_pos = lax.broadcasted_iota(jnp.int32, (blk, blk), 1)
    causal = key_pos <= qry_pos

    def q_tile(h, qb):
        return qt_ref[h * hd:(h + 1) * hd, qb * blk:(qb + 1) * blk]

    def choice_bias(h, qb):
        km = km_ref[:, h * hd:(h + 1) * hd]
        km_hi = km.astype(BF16).astype(F32)
        km_split = jnp.concatenate([km_hi, km - km_hi], axis=0).astype(BF16)
        g2 = _dot(km_split, q_tile(h, qb))
        gate = g2[:n_blocks] + g2[n_blocks:]
        past = blk_id < qb
        gate = jnp.where(past, gate, -jnp.inf)
        beaten = jnp.zeros(gate.shape, jnp.int32)
        for m in range(qb):
            gm = gate[m:m + 1, :]
            wins = (gm > gate) | ((gm == gate) & (blk_id > m))
            beaten = beaten + wins.astype(jnp.int32)
        return jnp.where(past & (beaten < MOBA_TOPK), 0.0, -jnp.inf)

    biases = {(h, qb): choice_bias(h, qb)
              for h in range(n_heads) for qb in range(MOBA_TOPK + 1, n_blocks)}

    ones_rows = vt1_scr.shape[1] - hd
    for h in range(n_heads):
        vt1_scr[h, 0:hd, :] = vt_ref[h * hd:(h + 1) * hd, :]
        vt1_scr[h, hd:, :] = jnp.ones((ones_rows, vt1_scr.shape[2]), BF16)

    def slots(group):
        base, out = 0, []
        for qb in group:
            out.append(base)
            base += qb + 1
        assert base <= s_scr.shape[2]
        return out

    def items(group):
        per_head = [(qb, base + n, n) for qb, base in zip(group, slots(group))
                    for n in range(qb + 1)]
        return [(h,) + it for it in per_head for h in range(n_heads)]

    col_max8 = {}
    acc = {}

    def score_block(half, h, qb, slot, n):
        s = _dot(k_ref[n * blk:(n + 1) * blk, h * hd:(h + 1) * hd], q_tile(h, qb))
        if n == qb:
            s = jnp.where(causal, s, -jnp.inf)
        elif (h, qb) in biases:
            s = s + biases[h, qb][n:n + 1, :]
        s_scr[h, half, slot] = s
        mx = jnp.max(s.reshape(blk // F32_SUBLANES, F32_SUBLANES, blk), axis=0)
        col_max8[h, qb] = mx if n == 0 else jnp.maximum(col_max8[h, qb], mx)

    def weights_block(half, h, qb, slot, n):
        if n == 0:
            m_run = jnp.max(col_max8[h, qb], axis=0, keepdims=True)
            col_max8[h, qb] = jnp.broadcast_to(m_run, (F32_SUBLANES, blk))
        s = s_scr[h, half, slot].reshape(blk // F32_SUBLANES, F32_SUBLANES, blk)
        return jnp.exp2(s - col_max8[h, qb]).reshape(blk, blk).astype(BF16)

    def pv_block(p, h, qb, n):
        pv = _dot(vt1_scr[h, :, n * blk:(n + 1) * blk], p)
        acc[h, qb] = pv if n == 0 else acc[h, qb] + pv
        if n == qb:
            a = acc.pop((h, qb))
            out = (a[:hd] * (1.0 / a[hd:hd + 1])).T
            o_ref[qb * blk:(qb + 1) * blk, h * hd:(h + 1) * hd] = out.astype(o_ref.dtype)

    first = MOBA_TOPK
    rest = [qb for qb in range(n_blocks - 1, -1, -1) if qb != first]
    groups = [[first]]
    while rest:
        big = rest.pop(0)
        small = [qb for qb in rest if big + qb + 2 <= n_blocks][:1]
        for qb in small:
            rest.remove(qb)
        groups.append([big] + small)
    for item in items(groups[0]):
        score_block(0, *item)
    ready = []
    for j, group in enumerate(groups):
        cur = items(group)
        nxt = items(groups[j + 1]) if j + 1 < len(groups) else []
        for t in range(max(len(cur), len(nxt))):
            if t < len(nxt):
                score_block((j + 1) % 2, *nxt[t])
            if len(ready) > PV_LAG or (ready and t >= len(cur)):
                pv_block(*ready.pop(0))
            if t < len(cur):
                h, qb, slot, n = cur[t]
                ready.append((weights_block(j % 2, h, qb, slot, n), h, qb, n))
    while ready:
        pv_block(*ready.pop(0))


def _moba_attention(k, qvt, kmean, batch, seq):
    t, a = k.shape
    hd = HEAD_DIM
    n_heads = a // hd
    n_blocks = seq // MOBA_BLOCK
    assert seq % MOBA_BLOCK == 0 and kmean.shape == (batch, n_blocks, a)
    assert n_heads % ATTN_HEADS == 0
    hw = ATTN_HEADS * hd
    steps = n_heads // ATTN_HEADS
    return pl.pallas_call(
        functools.partial(_moba_kernel, n_blocks=n_blocks, hd=hd),
        grid=(batch, steps),
        in_specs=[
            pl.BlockSpec((hw, seq), lambda b, h: (h, b)),
            pl.BlockSpec((seq, hw), lambda b, h: (b, h)),
            pl.BlockSpec((hw, seq), lambda b, h: (steps + h, b)),
            pl.BlockSpec((None, n_blocks, hw), lambda b, h: (b, 0, h)),
        ],
        out_specs=pl.BlockSpec((seq, hw), lambda b, h: (b, h)),
        out_shape=jax.ShapeDtypeStruct((t, a), BF16),
        scratch_shapes=[
            pltpu.VMEM((ATTN_HEADS, 2, n_blocks, MOBA_BLOCK, MOBA_BLOCK), F32),
            pltpu.VMEM((ATTN_HEADS, hd + BF16_SUBLANES, seq), BF16),
        ],
        compiler_params=pltpu.CompilerParams(
            dimension_semantics=("parallel", "parallel"), vmem_limit_bytes=VMEM_LIMIT),
        name="moba_attn",
    )(qvt, k, qvt, kmean)


def _tail_kernel(x_ref, a_ref, win_hbm, bg_ref, cw_ref, cb_ref, wpa_hbm, wpc_hbm, wo_hbm,
                 g_ref, b_ref, o_ref, u_scr, c_scr, mix_scr, wc_ref, wg_ref, wpa_ref, wpc_ref,
                 wo_ref, c_stage, g_stage, p_stage, sems, *, alpha, tiles_per_seq, layer):
    tm, d = x_ref.shape
    a = a_ref.shape[1]
    i = pl.program_id(0)

    @pl.when(i == 0)
    def _():
        n_conv, n_gate = wc_ref.shape[1], wg_ref.shape[1]
        conv_cols = win_hbm.at[layer, pl.ds(0, d), pl.ds(3 * a, n_conv)]
        gate_cols = win_hbm.at[layer, pl.ds(0, d), pl.ds(3 * a + n_conv, n_gate)]
        square = [pair for src, dst in ((wpa_hbm, wpa_ref), (wpc_hbm, wpc_ref), (wo_hbm, wo_ref))
                  for pair in _row_chunks(src.at[layer], dst, p_stage.shape[1])]
        _load_as_bf16([
            (_row_chunks(conv_cols, wc_ref, c_stage.shape[1]), c_stage, sems.at[0]),
            (_row_chunks(gate_cols, wg_ref, g_stage.shape[1]), g_stage, sems.at[1]),
            (square, p_stage, sems.at[2])])

    @pl.when(i % tiles_per_seq == 0)
    def _():
        u_scr[...] = jnp.zeros(u_scr.shape, F32)

    row = lax.broadcasted_iota(jnp.int32, (SUB_TILE, COL_CHUNK), 0)

    def conv_phase(r0):
        rows = slice(r0, r0 + SUB_TILE)
        xb = x_ref[rows, :].astype(BF16)
        for c0 in range(0, d, COL_CHUNK):
            cs = slice(c0, c0 + COL_CHUNK)
            h = _dot(xb, wc_ref[:, c0:c0 + COL_CHUNK])
            g_b = _dot(xb, wc_ref[:, d + c0:d + c0 + COL_CHUNK])
            g_c = _dot(xb, wc_ref[:, 2 * d + c0:2 * d + c0 + COL_CHUNK])
            u = g_c * h
            before = u_scr[:, cs]
            u_scr[:, cs] = u[SUB_TILE - CONV_HALO:, :]
            last, last2 = before[CONV_HALO - 1:CONV_HALO, :], before[CONV_HALO - 2:CONV_HALO - 1, :]
            u1 = jnp.where(row == 0, last, pltpu.roll(u, 1, 0))
            u2 = jnp.where(row == 0, last2, jnp.where(row == 1, last, pltpu.roll(u, 2, 0)))
            y = (cw_ref[0:1, cs] * u2 + cw_ref[1:2, cs] * u1 + cw_ref[2:3, cs] * u
                 + cb_ref[:, cs])
            c_scr[rows, cs] = (g_b * y).astype(BF16)

    def merge_phase(r0):
        rows = slice(r0, r0 + SUB_TILE)
        xb = x_ref[rows, :].astype(BF16)
        for c0 in range(0, d, COL_CHUNK):
            cs = slice(c0, c0 + COL_CHUNK)
            y_attn = _dot(a_ref[rows, :], wpa_ref[:, cs])
            y_conv = _dot(c_scr[rows, :], wpc_ref[:, cs])
            gate_a = jax.nn.sigmoid(_dot(xb, wg_ref[:, c0:c0 + COL_CHUNK]) + bg_ref[:, cs])
            gate_c = jax.nn.sigmoid(_dot(xb, wg_ref[:, d + c0:d + c0 + COL_CHUNK])
                                    + bg_ref[:, d + c0:d + c0 + COL_CHUNK])
            mix_scr[rows, cs] = (gate_a * y_attn + gate_c * y_conv).astype(BF16)

    def out_phase(r0):
        rows = slice(r0, r0 + SUB_TILE)
        mixed = _dot(mix_scr[rows, :], wo_ref[...])
        o_ref[rows, :] = _layer_norm(alpha * x_ref[rows, :] + mixed, g_ref[...], b_ref[...])

    conv_phase(0)
    for r0 in range(0, tm, SUB_TILE):
        merge_phase(r0)
        if r0 + SUB_TILE < tm:
            conv_phase(r0 + SUB_TILE)
        out_phase(r0)


def _mixer_tail(x, attn, w_in, b_gate, conv_w, conv_b, w_pa, w_pc, w_out, g, b, layer, alpha, seq):
    t, d = x.shape
    a = attn.shape[1]
    assert t % TAIL_TILE == 0 and seq % TAIL_TILE == 0 and d % COL_CHUNK == 0
    assert TAIL_TILE % SUB_TILE == 0 and d % TAIL_WEIGHT_CHUNKS == 0
    assert w_pa.shape[1:] == (a, d) and a == d
    assert w_pc.shape[1:] == (d, d) and w_out.shape[1:] == (d, d)
    assert conv_w.shape[1:] == (CONV_K, d) and w_in.shape[2] == 3 * a + 5 * d
    tile = lambda i: (i, 0)
    lyr = (layer, 0, 0)
    hbm = pl.BlockSpec(memory_space=pl.ANY)
    return pl.pallas_call(
        functools.partial(_tail_kernel, alpha=alpha, tiles_per_seq=seq // TAIL_TILE, layer=layer),
        grid=(t // TAIL_TILE,),
        in_specs=[
            pl.BlockSpec((TAIL_TILE, d), tile),
            pl.BlockSpec((TAIL_TILE, a), tile),
            hbm,
            _resident((None, 1, 2 * d), lyr),
            _resident((None, CONV_K, d), lyr),
            _resident((None, 1, d), lyr),
            hbm, hbm, hbm,
            _resident((None, 1, d), lyr),
            _resident((None, 1, d), lyr),
        ],
        out_specs=pl.BlockSpec((TAIL_TILE, d), tile),
        out_shape=jax.ShapeDtypeStruct((t, d), F32),
        scratch_shapes=[
            pltpu.VMEM((CONV_HALO, d), F32),
            pltpu.VMEM((TAIL_TILE, d), BF16),
            pltpu.VMEM((TAIL_TILE, d), BF16),
            pltpu.VMEM((d, 3 * d), BF16),
            pltpu.VMEM((d, 2 * d), BF16),
            pltpu.VMEM((a, d), BF16),
            pltpu.VMEM((d, d), BF16),
            pltpu.VMEM((d, d), BF16),
            pltpu.VMEM((WEIGHT_SLOTS, d // TAIL_WEIGHT_CHUNKS, 3 * d), F32),
            pltpu.VMEM((WEIGHT_SLOTS, d // TAIL_WEIGHT_CHUNKS, 2 * d), F32),
            pltpu.VMEM((WEIGHT_SLOTS, 2 * d // TAIL_WEIGHT_CHUNKS, d), F32),
            pltpu.SemaphoreType.DMA((3, WEIGHT_SLOTS)),
        ],
        compiler_params=pltpu.CompilerParams(
            dimension_semantics=("arbitrary",), vmem_limit_bytes=VMEM_LIMIT),
        name="mixer_tail",
    )(x, attn, w_in, b_gate, conv_w, conv_b, w_pa, w_pc, w_out, g, b)


def kernel(x, ffn1_w_up, ffn1_w_down, ln1_g, ln1_b, w_in, b_gate, conv_w, conv_b,
           w_proj_attn, w_proj_conv, w_out, ln2_g, ln2_b, ffn2_w_up, ffn2_w_down,
           ln3_g, ln3_b):
    batch, seq, d = x.shape
    depth = w_in.shape[0]
    a = N_HEADS * HEAD_DIM
    assert w_in.shape[2] == 3 * a + 3 * d + 2 * d and w_proj_attn.shape[1] == a
    alpha = (2.0 * depth) ** 0.25
    rows = lambda v: v.reshape(depth, 1, -1)

    h = x.reshape(batch * seq, d)
    for l in range(depth):
        h = _ffn_ln(h, ffn1_w_up, ffn1_w_down, rows(ln1_g), rows(ln1_b), l, alpha)
        k, qvt, kmean = _qkv_proj(h, w_in, a, l)
        attn = _moba_attention(k, qvt, kmean.reshape(batch, seq // MOBA_BLOCK, a), batch, seq)
        h = _mixer_tail(h, attn, w_in, rows(b_gate), conv_w, rows(conv_b), w_proj_attn,
                        w_proj_conv, w_out, rows(ln2_g), rows(ln2_b), l, alpha, seq)
        h = _ffn_ln(h, ffn2_w_up, ffn2_w_down, rows(ln3_g), rows(ln3_b), l, alpha)
    return h.reshape(batch, seq, d)
```
